```python
import jax, jax.numpy as jnp
from jax import lax
import numpy as np

D_MODEL = 2048
BATCH = 1
SEQ = 16384
DEPTH = 1
DEC_BATCH = 4
DEC_SEQ = 4096
PAST_LEN = 128

PLE_DIM = 256
ROPE_THETA = 10000.0
EPS = 1e-6
BLOCK = 128
MLA_HEADS = 8
MLA_Q_LORA = 512
MLA_KV_LORA = 256
MLA_NOPE = 128
MLA_ROPE = 64
MLA_V = 128
MLA_QK = MLA_NOPE + MLA_ROPE
SWA_HEADS = 8
SWA_KV_HEADS = 2
SWA_GROUP = SWA_HEADS // SWA_KV_HEADS
SWA_HEAD_DIM = 128
WINDOW = 128
MIX_MLA = MLA_HEADS * MLA_V
MIX_SWA = SWA_HEADS * SWA_HEAD_DIM
MIX_WIDTH = MIX_MLA + MIX_SWA
D_FF = 4 * D_MODEL
IN_SIZES = (MLA_Q_LORA, MLA_KV_LORA, MLA_ROPE, SWA_HEADS * SWA_HEAD_DIM,
            SWA_KV_HEADS * SWA_HEAD_DIM, SWA_KV_HEADS * SWA_HEAD_DIM)
IN_WIDTH = sum(IN_SIZES)
IN_OFFSETS = tuple(int(v) for v in np.cumsum(IN_SIZES)[:-1])

kernel_name = "hybrid_mla_swa_sink_encoder"


def rms_norm(x, g):
    xf = x.astype(jnp.float32)
    y = xf * lax.rsqrt(jnp.mean(xf * xf, axis=-1, keepdims=True) + EPS)
    return (y * g.astype(jnp.float32)).astype(x.dtype)


def rope_tables(seq, dim):
    inv = 1.0 / (ROPE_THETA ** (jnp.arange(0, dim, 2, dtype=jnp.float32) / dim))
    ang = jnp.arange(seq, dtype=jnp.float32)[:, None] * inv[None, :]
    return jnp.cos(ang), jnp.sin(ang)


def apply_rope(x, cos, sin):
    xf = x.astype(jnp.float32)
    half = xf.shape[-1] // 2
    x1, x2 = xf[..., :half], xf[..., half:]
    c = cos[None, :, None, :]
    s = sin[None, :, None, :]
    return jnp.concatenate([x1 * c - x2 * s, x2 * c + x1 * s], axis=-1).astype(x.dtype)


def dense_attention(q, k, v, scale):
    b, s, h, d = q.shape
    nb = s // BLOCK
    qb = q.reshape(b, nb, BLOCK, h, d).transpose(1, 0, 2, 3, 4)

    def one_block(qblk):
        logits = jnp.einsum('bqhd,bkhd->bhqk', qblk, k).astype(jnp.float32) * scale
        probs = jax.nn.softmax(logits, axis=-1).astype(v.dtype)
        return jnp.einsum('bhqk,bkhd->bqhd', probs, v)

    o = lax.map(one_block, qb)
    return o.transpose(1, 0, 2, 3, 4).reshape(b, s, h * v.shape[-1])


def mla_mixer(c_q, c_kv, k_pe, g_qa, w_qb, g_kva, w_kvb, g_qn, g_kn):
    b, s, _ = c_q.shape
    q = (rms_norm(c_q, g_qa) @ w_qb).reshape(b, s, MLA_HEADS, MLA_QK)
    q = rms_norm(q, g_qn)
    kv = (rms_norm(c_kv, g_kva) @ w_kvb).reshape(b, s, MLA_HEADS, MLA_NOPE + MLA_V)
    k_nope, v = kv[..., :MLA_NOPE], kv[..., MLA_NOPE:]
    k_rope = jnp.broadcast_to(k_pe[:, :, None, :], (b, s, MLA_HEADS, MLA_ROPE))
    k = rms_norm(jnp.concatenate([k_nope, k_rope], axis=-1), g_kn)
    cos, sin = rope_tables(s, MLA_ROPE)
    q = jnp.concatenate([q[..., :MLA_NOPE], apply_rope(q[..., MLA_NOPE:], cos, sin)], axis=-1)
    k = jnp.concatenate([k[..., :MLA_NOPE], apply_rope(k[..., MLA_NOPE:], cos, sin)], axis=-1)
    return dense_attention(q, k, v, MLA_QK ** -0.5)


def window_sink_mixer(q, k, v, g_q, g_k, sink):
    b, s, _ = q.shape
    nb = s // BLOCK
    q = rms_norm(q.reshape(b, s, SWA_HEADS, SWA_HEAD_DIM), g_q)
    k = rms_norm(k.reshape(b, s, SWA_KV_HEADS, SWA_HEAD_DIM), g_k)
    v = v.reshape(b, s, SWA_KV_HEADS, SWA_HEAD_DIM)
    cos, sin = rope_tables(s, SWA_HEAD_DIM)
    q = apply_rope(q, cos, sin)
    k = apply_rope(k, cos, sin)
    pad = ((0, 0), (BLOCK, BLOCK), (0, 0), (0, 0))
    kp = jnp.pad(k, pad).reshape(b, nb + 2, BLOCK, SWA_KV_HEADS, SWA_HEAD_DIM)
    vp = jnp.pad(v, pad).reshape(b, nb + 2, BLOCK, SWA_KV_HEADS, SWA_HEAD_DIM)
    kw = jnp.concatenate([kp[:, :-2], kp[:, 1:-1], kp[:, 2:]], axis=2)
    vw = jnp.concatenate([vp[:, :-2], vp[:, 1:-1], vp[:, 2:]], axis=2)
    qg = q.reshape(b, nb, BLOCK, SWA_KV_HEADS, SWA_GROUP, SWA_HEAD_DIM)
    logits = jnp.einsum('bnqkgd,bnskd->bnkgqs', qg, kw).astype(jnp.float32) * (SWA_HEAD_DIM ** -0.5)
    blk = jnp.arange(nb)[:, None]
    qpos = blk * BLOCK + jnp.arange(BLOCK)[None, :]
    kpos = (blk - 1) * BLOCK + jnp.arange(3 * BLOCK)[None, :]
    mask = (jnp.abs(qpos[:, :, None] - kpos[:, None, :]) <= WINDOW) & (kpos[:, None, :] >= 0) & (kpos[:, None, :] < s)
    logits = jnp.where(mask[None, :, None, None, :, :], logits, -jnp.inf)
    sink_b = sink.astype(jnp.float32).reshape(SWA_KV_HEADS, SWA_GROUP)[None, None, :, :, None, None]
    m = jnp.maximum(jnp.max(logits, axis=-1, keepdims=True), sink_b)
    p = jnp.exp(logits - m)
    denom = jnp.sum(p, axis=-1, keepdims=True) + jnp.exp(sink_b - m)
    probs = (p / denom).astype(v.dtype)
    o = jnp.einsum('bnkgqs,bnskd->bnqkgd', probs, vw)
    return o.reshape(b, s, MIX_SWA)


def layer(x, p_l, g_attn, w_in, g_qa, w_qb, g_kva, w_kvb, g_qn_mla, g_kn_mla,
          g_q_swa, g_k_swa, sink, g_out_mla, g_out_swa, w_out, g_mlp, w_up, w_down,
          g_ple, w_ple, w_gate):
    h = rms_norm(x, g_attn)
    c_q, c_kv, k_pe, q_s, k_s, v_s = jnp.split(h @ w_in, IN_OFFSETS, axis=-1)
    o_a = rms_norm(mla_mixer(c_q, c_kv, k_pe, g_qa, w_qb, g_kva, w_kvb, g_qn_mla, g_kn_mla), g_out_mla)
    o_b = rms_norm(window_sink_mixer(q_s, k_s, v_s, g_q_swa, g_k_swa, sink), g_out_swa)
    x = x + jnp.concatenate([o_a, o_b], axis=-1) @ w_out
    hm = rms_norm(x, g_mlp)
    x = x + jnp.square(jax.nn.relu(hm @ w_up)) @ w_down
    gate = jax.nn.sigmoid(rms_norm(x, g_ple) @ w_gate)
    return x + (p_l @ w_ple) * gate


def setup_inputs(seed: int = 0) -> dict:
    key = jax.random.key(seed)
    ks = iter(jax.random.split(key, 32))

    def w(shape, fan_in):
        return jax.random.normal(next(ks), (DEPTH,) + shape, jnp.float32) * (fan_in ** -0.5)

    def g(dim):
        return 1.0 + 0.01 * jax.random.normal(next(ks), (DEPTH, dim), jnp.float32)

    return {
        "x_prompt": jax.random.normal(next(ks), (BATCH, SEQ, D_MODEL), jnp.float32),
        "x_sample": jax.random.normal(next(ks), (DEC_BATCH, DEC_SEQ, D_MODEL), jnp.float32),
        "p_prompt": jax.random.normal(next(ks), (DEPTH, BATCH, SEQ, PLE_DIM), jnp.float32),
        "p_sample": jax.random.normal(next(ks), (DEPTH, DEC_BATCH, DEC_SEQ, PLE_DIM), jnp.float32),
        "g_attn": g(D_MODEL),
        "w_in": w((D_MODEL, IN_WIDTH), D_MODEL),
        "g_qa": g(MLA_Q_LORA),
        "w_qb": w((MLA_Q_LORA, MLA_HEADS * MLA_QK), MLA_Q_LORA),
        "g_kva": g(MLA_KV_LORA),
        "w_kvb": w((MLA_KV_LORA, MLA_HEADS * (MLA_NOPE + MLA_V)), MLA_KV_LORA),
        "g_qn_mla": g(MLA_QK),
        "g_kn_mla": g(MLA_QK),
        "g_q_swa": g(SWA_HEAD_DIM),
        "g_k_swa": g(SWA_HEAD_DIM),
        "sink": 0.5 * jax.random.normal(next(ks), (DEPTH, SWA_HEADS), jnp.float32),
        "g_out_mla": g(MIX_MLA),
        "g_out_swa": g(MIX_SWA),
        "w_out": w((MIX_WIDTH, D_MODEL), MIX_WIDTH),
        "g_mlp": g(D_MODEL),
        "w_up": w((D_MODEL, D_FF), D_MODEL),
        "w_down": w((D_FF, D_MODEL), D_FF),
        "g_ple": g(D_MODEL),
        "w_ple": w((PLE_DIM, D_MODEL), PLE_DIM),
        "w_gate": w((D_MODEL, D_MODEL), D_MODEL),
    }


def reference(x_prompt, x_sample, p_prompt, p_sample, g_attn, w_in, g_qa, w_qb, g_kva, w_kvb,
              g_qn_mla, g_kn_mla, g_q_swa, g_k_swa, sink, g_out_mla, g_out_swa, w_out,
              g_mlp, w_up, w_down, g_ple, w_ple, w_gate):
    y_prompt = x_prompt
    y_sample = x_sample
    for l in range(DEPTH):
        params = (g_attn[l], w_in[l], g_qa[l], w_qb[l], g_kva[l], w_kvb[l], g_qn_mla[l], g_kn_mla[l],
                  g_q_swa[l], g_k_swa[l], sink[l], g_out_mla[l], g_out_swa[l], w_out[l],
                  g_mlp[l], w_up[l], w_down[l], g_ple[l], w_ple[l], w_gate[l])
        y_prompt = layer(y_prompt, p_prompt[l], *params)
        y_sample = layer(y_sample, p_sample[l], *params)
    return (y_prompt, y_sample)
```

```python
import functools
import math

import jax
import jax.numpy as jnp
from jax import lax
from jax.experimental import pallas as pl
from jax.experimental.pallas import tpu as pltpu

EPS = 1e-6
ROPE_THETA = 10000.0
WINDOW = 128
MLA_HEADS = 8
MLA_Q_LORA = 512
MLA_KV_LORA = 256
MLA_NOPE = 128
MLA_ROPE = 64
MLA_V = 128
MLA_QK = MLA_NOPE + MLA_ROPE
SWA_HEADS = 8
SWA_KV_HEADS = 2
SWA_GROUP = SWA_HEADS // SWA_KV_HEADS
SWA_HEAD_DIM = 128
LANES = 128
MLA_PAD = 2 * LANES
LOG2E = math.log2(math.e)
VMEM_LIMIT = 56 * 1024 * 1024

BF16 = jnp.bfloat16
F32 = jnp.float32


def _params(*sem):
    return pltpu.CompilerParams(dimension_semantics=sem, vmem_limit_bytes=VMEM_LIMIT)


def _resident(shape):
    nd = len(shape)
    return pl.BlockSpec(shape, lambda *_: (0,) * nd, pipeline_mode=pl.Buffered(1))


def _rms(x, g):
    return x * lax.rsqrt(jnp.mean(x * x, axis=-1, keepdims=True) + EPS) * g


def _rope(x, t_cos, t_sin):
    return x * t_cos + pltpu.roll(x, LANES // 2, axis=1) * t_sin


def _proj_kernel(x_ref, g_attn_ref, w_in_ref, g_qa_ref, w_qb_ref, g_kva_ref, w_kvb_ref,
                 gq_a_ref, gq_b_ref, gk_a_ref, gk_b_ref, gqs_ref, gks_ref,
                 tm_cos_ref, tm_sin_ref, ts_cos_ref, ts_sin_ref,
                 q_ref, k_ref, vt_ref, qs_ref, ks_ref, vs_ref):
    h = _rms(x_ref[...], g_attn_ref[...]).astype(BF16)
    proj = jnp.dot(h, w_in_ref[...], preferred_element_type=F32)
    o_ckv = MLA_Q_LORA
    o_kpe = o_ckv + MLA_KV_LORA
    o_qs = o_kpe + LANES
    o_ks = o_qs + SWA_HEADS * SWA_HEAD_DIM
    o_vs = o_ks + SWA_KV_HEADS * SWA_HEAD_DIM

    tm_cos, tm_sin = tm_cos_ref[...], tm_sin_ref[...]
    ts_cos, ts_sin = ts_cos_ref[...], ts_sin_ref[...]

    cqn = _rms(proj[:, :o_ckv], g_qa_ref[...]).astype(BF16)
    q = jnp.dot(cqn, w_qb_ref[...], preferred_element_type=F32)
    q_scale = MLA_QK ** -0.5 * LOG2E
    for hd in range(MLA_HEADS):
        a = q[:, hd * MLA_PAD: hd * MLA_PAD + LANES]
        b = q[:, hd * MLA_PAD + LANES: (hd + 1) * MLA_PAD]
        ss = jnp.sum(a * a, axis=-1, keepdims=True) + 0.5 * jnp.sum(b * b, axis=-1, keepdims=True)
        r = lax.rsqrt(ss * (1.0 / MLA_QK) + EPS) * q_scale
        bo = _rope(b * gq_b_ref[...], tm_cos, tm_sin)
        q_ref[hd, :, :LANES] = (a * gq_a_ref[...] * r).astype(BF16)
        q_ref[hd, :, LANES:] = (bo * r).astype(BF16)

    ckvn = _rms(proj[:, o_ckv:o_kpe], g_kva_ref[...]).astype(BF16)
    kv = jnp.dot(ckvn, w_kvb_ref[...], preferred_element_type=F32)
    kb = proj[:, o_kpe:o_qs]
    s_pe = 0.5 * jnp.sum(kb * kb, axis=-1, keepdims=True)
    kr = _rope(kb * gk_b_ref[...], tm_cos, tm_sin)
    for hd in range(MLA_HEADS):
        kn = kv[:, hd * MLA_PAD: hd * MLA_PAD + LANES]
        v = kv[:, hd * MLA_PAD + LANES: (hd + 1) * MLA_PAD]
        ss = jnp.sum(kn * kn, axis=-1, keepdims=True) + s_pe
        r = lax.rsqrt(ss * (1.0 / MLA_QK) + EPS)
        k_ref[hd, :, :LANES] = (kn * gk_a_ref[...] * r).astype(BF16)
        k_ref[hd, :, LANES:] = (kr * r).astype(BF16)
        vt_ref[hd] = v.T.astype(BF16)

    s_scale = SWA_HEAD_DIM ** -0.5 * LOG2E
    for hd in range(SWA_HEADS):
        xs = proj[:, o_qs + hd * SWA_HEAD_DIM: o_qs + (hd + 1) * SWA_HEAD_DIM]
        qn = _rope(_rms(xs, gqs_ref[...]), ts_cos, ts_sin) * s_scale
        qs_ref[:, hd * SWA_HEAD_DIM:(hd + 1) * SWA_HEAD_DIM] = qn.astype(BF16)
    for hd in range(SWA_KV_HEADS):
        xs = proj[:, o_ks + hd * SWA_HEAD_DIM: o_ks + (hd + 1) * SWA_HEAD_DIM]
        kn = _rope(_rms(xs, gks_ref[...]), ts_cos, ts_sin)
        ks_ref[:, hd * SWA_HEAD_DIM:(hd + 1) * SWA_HEAD_DIM] = kn.astype(BF16)
    vs_ref[...] = proj[:, o_vs:].astype(BF16)


def _proj(x, seq, wts, tabs, tm):
    n, d = x.shape
    tm = min(tm, seq)
    nt_seq = seq // tm
    row = lambda i: (i, 0)
    tab = lambda i: (i % nt_seq, 0)
    head_rows = lambda i: (0, i, 0)
    vec_names = ("g_attn", "w_in", "g_qa", "w_qb", "g_kva", "w_kvb",
                 "gq_a", "gq_b", "gk_a", "gk_b", "g_q_swa", "g_k_swa")
    vec_args = [wts[k] for k in vec_names]
    in_specs = ([pl.BlockSpec((tm, d), row)] + [_resident(a.shape) for a in vec_args]
                + [pl.BlockSpec((tm, LANES), tab)] * 4)
    n_qs = SWA_HEADS * SWA_HEAD_DIM
    n_ks = SWA_KV_HEADS * SWA_HEAD_DIM
    out_shape = (
        jax.ShapeDtypeStruct((MLA_HEADS, n, MLA_PAD), BF16),
        jax.ShapeDtypeStruct((MLA_HEADS, n, MLA_PAD), BF16),
        jax.ShapeDtypeStruct((MLA_HEADS, MLA_V, n), BF16),
        jax.ShapeDtypeStruct((n, n_qs), BF16),
        jax.ShapeDtypeStruct((n, n_ks), BF16),
        jax.ShapeDtypeStruct((n, n_ks), BF16),
    )
    out_specs = (
        pl.BlockSpec((MLA_HEADS, tm, MLA_PAD), head_rows),
        pl.BlockSpec((MLA_HEADS, tm, MLA_PAD), head_rows),
        pl.BlockSpec((MLA_HEADS, MLA_V, tm), lambda i: (0, 0, i)),
        pl.BlockSpec((tm, n_qs), row),
        pl.BlockSpec((tm, n_ks), row),
        pl.BlockSpec((tm, n_ks), row),
    )
    return pl.pallas_call(
        _proj_kernel, grid=(n // tm,), in_specs=in_specs, out_specs=out_specs,
        out_shape=out_shape, compiler_params=_params("parallel"), name="proj",
    )(x, *vec_args, *tabs)


def _mla_kernel(q_ref, k_ref, vt_ref, o_ref, m_ref, l_ref, acc_ref):
    j = pl.program_id(3)

    @pl.when(j == 0)
    def _():
        m_ref[...] = jnp.full_like(m_ref, -jnp.inf)
        l_ref[...] = jnp.zeros_like(l_ref)
        acc_ref[...] = jnp.zeros_like(acc_ref)

    s = lax.dot_general(k_ref[0], q_ref[0], (((1,), (1,)), ((), ())), preferred_element_type=F32)
    m_prev = m_ref[...]
    m_new = jnp.maximum(m_prev, jnp.max(s, axis=0, keepdims=True))
    alpha = jnp.exp2(m_prev - m_new)
    p = jnp.exp2(s - m_new)
    l_ref[...] = alpha * l_ref[...] + jnp.sum(p, axis=0, keepdims=True)
    acc_ref[...] = alpha * acc_ref[...] + jnp.dot(vt_ref[0], p.astype(BF16), preferred_element_type=F32)
    m_ref[...] = m_new

    @pl.when(j == pl.num_programs(3) - 1)
    def _():
        o_ref[...] = (acc_ref[...] / l_ref[...]).T.astype(o_ref.dtype)


def _mla_attention(q, k, vt, batch, seq, tq, tk):
    n = batch * seq
    tq, tk = min(tq, seq), min(tk, seq)
    nq, nk = seq // tq, seq // tk
    return pl.pallas_call(
        _mla_kernel,
        grid=(batch, MLA_HEADS, nq, nk),
        in_specs=[
            pl.BlockSpec((1, tq, MLA_PAD), lambda b, h, i, j: (h, b * nq + i, 0)),
            pl.BlockSpec((1, tk, MLA_PAD), lambda b, h, i, j: (h, b * nk + j, 0)),
            pl.BlockSpec((1, MLA_V, tk), lambda b, h, i, j: (h, 0, b * nk + j)),
        ],
        out_specs=pl.BlockSpec((tq, MLA_V), lambda b, h, i, j: (b * nq + i, h)),
        out_shape=jax.ShapeDtypeStruct((n, MLA_HEADS * MLA_V), BF16),
        scratch_shapes=[pltpu.VMEM((1, tq), F32), pltpu.VMEM((1, tq), F32), pltpu.VMEM((MLA_V, tq), F32)],
        compiler_params=_params("parallel", "parallel", "parallel", "arbitrary"),
        name="mla_attn",
    )(q, k, vt)


def _swa_kernel(sink_ref, q_ref, kp_ref, km_ref, kn_ref, vp_ref, vm_ref, vn_ref, o_ref, *, tq, nq):
    g = pl.program_id(1)
    i = pl.program_id(2)
    neg = -jnp.inf
    rows = lax.broadcasted_iota(jnp.int32, (tq, tq), 0)
    cols = lax.broadcasted_iota(jnp.int32, (tq, tq), 1)
    mask_m = jnp.abs(rows - cols) <= WINDOW
    rows_e = lax.broadcasted_iota(jnp.int32, (tq, WINDOW), 0)
    cols_e = lax.broadcasted_iota(jnp.int32, (tq, WINDOW), 1)
    mask_p = (cols_e >= rows_e) & (i > 0)
    mask_n = (rows_e - cols_e >= tq - WINDOW) & (i < nq - 1)
    nt = (((1,), (1,)), ((), ()))
    for hd in range(SWA_GROUP):
        q = q_ref[:, hd * SWA_HEAD_DIM:(hd + 1) * SWA_HEAD_DIM]
        s_p = jnp.where(mask_p, lax.dot_general(q, kp_ref[...], nt, preferred_element_type=F32), neg)
        s_m = jnp.where(mask_m, lax.dot_general(q, km_ref[...], nt, preferred_element_type=F32), neg)
        s_n = jnp.where(mask_n, lax.dot_general(q, kn_ref[...], nt, preferred_element_type=F32), neg)
        sink = sink_ref[g * SWA_GROUP + hd] * LOG2E
        m = jnp.maximum(jnp.maximum(jnp.max(s_p, axis=-1, keepdims=True), jnp.max(s_m, axis=-1, keepdims=True)),
                        jnp.maximum(jnp.max(s_n, axis=-1, keepdims=True), sink))
        p_p, p_m, p_n = jnp.exp2(s_p - m), jnp.exp2(s_m - m), jnp.exp2(s_n - m)
        denom = (jnp.sum(p_p, axis=-1, keepdims=True) + jnp.sum(p_m, axis=-1, keepdims=True)
                 + jnp.sum(p_n, axis=-1, keepdims=True) + jnp.exp2(sink - m))
        o = (jnp.dot(p_p.astype(BF16), vp_ref[...], preferred_element_type=F32)
             + jnp.dot(p_m.astype(BF16), vm_ref[...], preferred_element_type=F32)
             + jnp.dot(p_n.astype(BF16), vn_ref[...], preferred_element_type=F32))
        o_ref[:, hd * SWA_HEAD_DIM:(hd + 1) * SWA_HEAD_DIM] = (o / denom).astype(o_ref.dtype)


def _swa_attention(sink, q, k, v, batch, seq, tq):
    n = batch * seq
    tq = min(tq, seq)
    nq = seq // tq
    r = tq // WINDOW
    nb = seq // WINDOW
    gw = SWA_GROUP * SWA_HEAD_DIM
    main = lambda b, g, i: (b * nq + i, g)
    prev = lambda b, g, i: (b * nb + jnp.maximum(i * r - 1, 0), g)
    nxt = lambda b, g, i: (b * nb + jnp.minimum((i + 1) * r, nb - 1), g)
    edge = (WINDOW, SWA_HEAD_DIM)
    return pl.pallas_call(
        functools.partial(_swa_kernel, tq=tq, nq=nq),
        grid=(batch, SWA_KV_HEADS, nq),
        in_specs=[
            pl.BlockSpec(memory_space=pltpu.SMEM),
            pl.BlockSpec((tq, gw), main),
            pl.BlockSpec(edge, prev), pl.BlockSpec((tq, SWA_HEAD_DIM), main), pl.BlockSpec(edge, nxt),
            pl.BlockSpec(edge, prev), pl.BlockSpec((tq, SWA_HEAD_DIM), main), pl.BlockSpec(edge, nxt),
        ],
        out_specs=pl.BlockSpec((tq, gw), main),
        out_shape=jax.ShapeDtypeStruct((n, SWA_HEADS * SWA_HEAD_DIM), BF16),
        compiler_params=_params("parallel", "parallel", "parallel"),
        name="swa_attn",
    )(sink, q, k, k, k, v, v, v)


def _out_proj_kernel(x_ref, oa_ref, ob_ref, ga_ref, gb_ref, wa_ref, wb_ref, g_mlp_ref, x1_ref, hm_ref):
    oa = _rms(oa_ref[...].astype(F32), ga_ref[...]).astype(BF16)
    ob = _rms(ob_ref[...].astype(F32), gb_ref[...]).astype(BF16)
    x1 = (x_ref[...] + jnp.dot(oa, wa_ref[...], preferred_element_type=F32)
          + jnp.dot(ob, wb_ref[...], preferred_element_type=F32))
    x1_ref[...] = x1
    hm_ref[...] = _rms(x1, g_mlp_ref[...]).astype(BF16)


def _out_proj(x, oa, ob, wts, tm):
    n, d = x.shape
    tm = min(tm, n)
    row = lambda i: (i, 0)
    consts = [wts[k] for k in ("g_out_mla", "g_out_swa", "w_out_a", "w_out_b", "g_mlp")]
    return pl.pallas_call(
        _out_proj_kernel, grid=(n // tm,),
        in_specs=[pl.BlockSpec((tm, d), row), pl.BlockSpec((tm, oa.shape[1]), row),
                  pl.BlockSpec((tm, ob.shape[1]), row)] + [_resident(a.shape) for a in consts],
        out_specs=(pl.BlockSpec((tm, d), row), pl.BlockSpec((tm, d), row)),
        out_shape=(jax.ShapeDtypeStruct((n, d), F32), jax.ShapeDtypeStruct((n, d), BF16)),
        compiler_params=_params("parallel"), name="out_proj",
    )(x, oa, ob, *consts)


def _mlp_kernel(x1_ref, hm_ref, wu_ref, wd_ref, o_ref):
    j = pl.program_id(1)
    up = jnp.dot(hm_ref[...], wu_ref[...], preferred_element_type=F32)
    act = jnp.square(jnp.maximum(up, 0.0)).astype(BF16)
    contrib = jnp.dot(act, wd_ref[...], preferred_element_type=F32)

    @pl.when(j == 0)
    def _():
        o_ref[...] = x1_ref[...] + contrib

    @pl.when(j > 0)
    def _():
        o_ref[...] += contrib


def _mlp(x1, hm, w_up, w_down, tm, tf):
    n, d = x1.shape
    f = w_up.shape[1]
    tm, tf = min(tm, n), min(tf, f)
    row = lambda i, j: (i, 0)
    return pl.pallas_call(
        _mlp_kernel, grid=(n // tm, f // tf),
        in_specs=[pl.BlockSpec((tm, d), row), pl.BlockSpec((tm, d), row),
                  pl.BlockSpec((d, tf), lambda i, j: (0, j)), pl.BlockSpec((tf, d), lambda i, j: (j, 0))],
        out_specs=pl.BlockSpec((tm, d), row),
        out_shape=jax.ShapeDtypeStruct((n, d), F32),
        compiler_params=_params("parallel", "arbitrary"), name="mlp",
    )(x1, hm, w_up, w_down)


def _ple_kernel(x_ref, p_ref, g_ref, wg_ref, wp_ref, o_ref):
    x = x_ref[...]
    z = jnp.dot(_rms(x, g_ref[...]).astype(BF16), wg_ref[...], preferred_element_type=F32)
    gate = 1.0 / (1.0 + jnp.exp(-z))
    e = jnp.dot(p_ref[...].astype(BF16), wp_ref[...], preferred_element_type=F32)
    o_ref[...] = x + e * gate


def _ple(x2, p, wts, tm):
    n, d = x2.shape
    tm = min(tm, n)
    row = lambda i: (i, 0)
    consts = [wts[k] for k in ("g_ple", "w_gate", "w_ple")]
    return pl.pallas_call(
        _ple_kernel, grid=(n // tm,),
        in_specs=[pl.BlockSpec((tm, d), row), pl.BlockSpec((tm, p.shape[1]), row)]
                 + [_resident(a.shape) for a in consts],
        out_specs=pl.BlockSpec((tm, d), row),
        out_shape=jax.ShapeDtypeStruct((n, d), F32),
        compiler_params=_params("parallel"), name="ple",
    )(x2, p, *consts)


def _swap_halves(a):
    half = a.shape[-1] // 2
    return jnp.concatenate([a[..., half:], a[..., :half]], axis=-1)


def _rope_tables(seq, dim):
    inv = 1.0 / (ROPE_THETA ** (jnp.arange(0, dim, 2, dtype=F32) / dim))
    ang = jnp.arange(seq, dtype=F32)[:, None] * inv[None, :]
    cos, sin = jnp.cos(ang), jnp.sin(ang)
    pad = jnp.zeros((seq, LANES - dim), F32)
    return (jnp.concatenate([cos, cos, pad], axis=-1), jnp.concatenate([-sin, sin, pad], axis=-1))


def _prepare_weights(g_attn, w_in, g_qa, w_qb, g_kva, w_kvb, g_qn_mla, g_kn_mla, g_q_swa, g_k_swa,
                     g_out_mla, g_out_swa, w_out, g_mlp, w_up, w_down, g_ple, w_ple, w_gate):
    row = lambda g: g.reshape(1, -1)
    o_kpe = MLA_Q_LORA + MLA_KV_LORA
    kpe = w_in[:, o_kpe:o_kpe + MLA_ROPE]
    w_in_r = jnp.concatenate([w_in[:, :o_kpe], kpe, _swap_halves(kpe), w_in[:, o_kpe + MLA_ROPE:]], axis=1)
    wq = w_qb.reshape(MLA_Q_LORA, MLA_HEADS, MLA_QK)
    rope = wq[..., MLA_NOPE:]
    wq_r = jnp.concatenate([wq[..., :MLA_NOPE], rope, _swap_halves(rope)], axis=-1)
    rope_gain = lambda g: jnp.concatenate([g[MLA_NOPE:], _swap_halves(g[MLA_NOPE:])]).reshape(1, -1)
    n_a = MLA_HEADS * MLA_V
    return {
        "g_attn": row(g_attn), "w_in": w_in_r.astype(BF16),
        "g_qa": row(g_qa), "w_qb": wq_r.reshape(MLA_Q_LORA, MLA_HEADS * MLA_PAD).astype(BF16),
        "g_kva": row(g_kva), "w_kvb": w_kvb.astype(BF16),
        "gq_a": row(g_qn_mla[:MLA_NOPE]), "gq_b": rope_gain(g_qn_mla),
        "gk_a": row(g_kn_mla[:MLA_NOPE]), "gk_b": rope_gain(g_kn_mla),
        "g_q_swa": row(g_q_swa), "g_k_swa": row(g_k_swa),
        "g_out_mla": row(g_out_mla), "g_out_swa": row(g_out_swa),
        "w_out_a": w_out[:n_a].astype(BF16), "w_out_b": w_out[n_a:].astype(BF16),
        "g_mlp": row(g_mlp), "w_up": w_up.astype(BF16), "w_down": w_down.astype(BF16),
        "g_ple": row(g_ple), "w_gate": w_gate.astype(BF16), "w_ple": w_ple.astype(BF16),
    }


def _layer(x, p, sink, wts):
    batch, seq, d = x.shape
    n = batch * seq
    xf = x.reshape(n, d)
    tabs = _rope_tables(seq, MLA_ROPE) + _rope_tables(seq, SWA_HEAD_DIM)
    q, k, vt, qs, ks, vs = _proj(xf, seq, wts, tabs, tm=256)
    o_a = _mla_attention(q, k, vt, batch, seq, tq=512, tk=512)
    o_b = _swa_attention(sink, qs, ks, vs, batch, seq, tq=512)
    x1, hm = _out_proj(xf, o_a, o_b, wts, tm=512)
    x2 = _mlp(x1, hm, wts["w_up"], wts["w_down"], tm=512, tf=512)
    y = _ple(x2, p.reshape(n, -1), wts, tm=512)
    return y.reshape(batch, seq, d)


def kernel(x_prompt, x_sample, p_prompt, p_sample, g_attn, w_in, g_qa, w_qb, g_kva, w_kvb, g_qn_mla, g_kn_mla,
           g_q_swa, g_k_swa, sink, g_out_mla, g_out_swa, w_out, g_mlp, w_up, w_down, g_ple, w_ple, w_gate):
    y_prompt, y_sample = x_prompt, x_sample
    for l in range(g_attn.shape[0]):
        wts = _prepare_weights(g_attn[l], w_in[l], g_qa[l], w_qb[l], g_kva[l], w_kvb[l], g_qn_mla[l],
                               g_kn_mla[l], g_q_swa[l], g_k_swa[l], g_out_mla[l], g_out_swa[l], w_out[l],
                               g_mlp[l], w_up[l], w_down[l], g_ple[l], w_ple[l], w_gate[l])
        y_prompt = _layer(y_prompt, p_prompt[l], sink[l], wts)
        y_sample = _layer(y_sample, p_sample[l], sink[l], wts)
    return (y_prompt, y_sample)
```

```python
import functools
import math

import jax
import jax.numpy as jnp
from jax import lax
from jax.experimental import pallas as pl
from jax.experimental.pallas import tpu as pltpu

EPS = 1e-6
ROPE_THETA = 10000.0
WINDOW = 128
MLA_HEADS = 8
MLA_Q_LORA = 512
MLA_KV_LORA = 256
MLA_NOPE = 128
MLA_ROPE = 64
MLA_V = 128
MLA_QK = MLA_NOPE + MLA_ROPE
SWA_HEADS = 8
SWA_KV_HEADS = 2
SWA_GROUP = SWA_HEADS // SWA_KV_HEADS
SWA_HEAD_DIM = 128
LANES = 128
SUBLANES = 8
SAFE_LOG2_RANGE = 100.0
MLA_PAD = 2 * LANES
LOG2E = math.log2(math.e)
VMEM_LIMIT = 56 * 1024 * 1024

BF16 = jnp.bfloat16
F32 = jnp.float32


def _params(*sem):
    return pltpu.CompilerParams(dimension_semantics=sem, vmem_limit_bytes=VMEM_LIMIT)


def _resident(shape):
    nd = len(shape)
    return pl.BlockSpec(shape, lambda *_: (0,) * nd, pipeline_mode=pl.Buffered(1))


def _rms(x, g):
    return x * lax.rsqrt(jnp.mean(x * x, axis=-1, keepdims=True) + EPS) * g


def _rope(x, t_cos, t_sin):
    return x * t_cos + pltpu.roll(x, LANES // 2, axis=1) * t_sin


def _proj_kernel(x_ref, g_attn_ref, w_in_ref, g_qa_ref, w_qb_ref, g_kva_ref, w_kvb_ref,
                 gq_a_ref, gq_b_ref, gk_a_ref, gk_b_ref, gqs_ref, gks_ref,
                 tm_cos_ref, tm_sin_ref, ts_cos_ref, ts_sin_ref,
                 qt_ref, k_ref, vt_ref, qs_ref, ks_ref, vs_ref):
    h = _rms(x_ref[...], g_attn_ref[...]).astype(BF16)
    proj = jnp.dot(h, w_in_ref[...], preferred_element_type=F32)
    o_ckv = MLA_Q_LORA
    o_kpe = o_ckv + MLA_KV_LORA
    o_qs = o_kpe + LANES
    o_ks = o_qs + SWA_HEADS * SWA_HEAD_DIM
    o_vs = o_ks + SWA_KV_HEADS * SWA_HEAD_DIM

    tm_cos, tm_sin = tm_cos_ref[...], tm_sin_ref[...]
    ts_cos, ts_sin = ts_cos_ref[...], ts_sin_ref[...]

    shift_lane = (lax.broadcasted_iota(jnp.int32, (1, LANES), 1) == MLA_QK - LANES).astype(F32)
    k_bound = MLA_QK ** 0.5 * jnp.maximum(jnp.max(jnp.abs(gk_a_ref[...]), axis=-1, keepdims=True),
                                          jnp.max(jnp.abs(gk_b_ref[...]), axis=-1, keepdims=True))

    cqn = _rms(proj[:, :o_ckv], g_qa_ref[...]).astype(BF16)
    q = jnp.dot(cqn, w_qb_ref[...], preferred_element_type=F32)
    q_scale = MLA_QK ** -0.5 * LOG2E
    for hd in range(MLA_HEADS):
        a = q[:, hd * MLA_PAD: hd * MLA_PAD + LANES]
        b = q[:, hd * MLA_PAD + LANES: (hd + 1) * MLA_PAD]
        ss = jnp.sum(a * a, axis=-1, keepdims=True) + 0.5 * jnp.sum(b * b, axis=-1, keepdims=True)
        r = lax.rsqrt(ss * (1.0 / MLA_QK) + EPS) * q_scale
        qa = a * gq_a_ref[...] * r
        qb = _rope(b * gq_b_ref[...], tm_cos, tm_sin) * r
        q_norm = jnp.sqrt(jnp.sum(qa * qa, axis=-1, keepdims=True) + jnp.sum(qb * qb, axis=-1, keepdims=True))
        qb = qb - shift_lane * (q_norm * k_bound)
        qt_ref[hd, :LANES, :] = qa.T.astype(BF16)
        qt_ref[hd, LANES:, :] = qb.T.astype(BF16)

    ckvn = _rms(proj[:, o_ckv:o_kpe], g_kva_ref[...]).astype(BF16)
    kv = jnp.dot(ckvn, w_kvb_ref[...], preferred_element_type=F32)
    kb = proj[:, o_kpe:o_qs]
    s_pe = 0.5 * jnp.sum(kb * kb, axis=-1, keepdims=True)
    kr = _rope(kb * gk_b_ref[...], tm_cos, tm_sin)
    for hd in range(MLA_HEADS):
        kn = kv[:, hd * MLA_PAD: hd * MLA_PAD + LANES]
        v = kv[:, hd * MLA_PAD + LANES: (hd + 1) * MLA_PAD]
        ss = jnp.sum(kn * kn, axis=-1, keepdims=True) + s_pe
        r = lax.rsqrt(ss * (1.0 / MLA_QK) + EPS)
        k_ref[hd, :, :LANES] = (kn * gk_a_ref[...] * r).astype(BF16)
        k_ref[hd, :, LANES:] = (kr * r + shift_lane).astype(BF16)
        vt_ref[hd] = v.T.astype(BF16)

    s_scale = SWA_HEAD_DIM ** -0.5 * LOG2E
    for hd in range(SWA_HEADS):
        xs = proj[:, o_qs + hd * SWA_HEAD_DIM: o_qs + (hd + 1) * SWA_HEAD_DIM]
        qn = _rope(_rms(xs, gqs_ref[...]), ts_cos, ts_sin) * s_scale
        qs_ref[:, hd * SWA_HEAD_DIM:(hd + 1) * SWA_HEAD_DIM] = qn.astype(BF16)
    for hd in range(SWA_KV_HEADS):
        xs = proj[:, o_ks + hd * SWA_HEAD_DIM: o_ks + (hd + 1) * SWA_HEAD_DIM]
        kn = _rope(_rms(xs, gks_ref[...]), ts_cos, ts_sin)
        ks_ref[:, hd * SWA_HEAD_DIM:(hd + 1) * SWA_HEAD_DIM] = kn.astype(BF16)
    vs_ref[...] = proj[:, o_vs:].astype(BF16)


def _proj(x, seq, wts, tabs, tm):
    n, d = x.shape
    tm = min(tm, seq)
    nt_seq = seq // tm
    row = lambda i: (i, 0)
    tab = lambda i: (i % nt_seq, 0)
    head_rows = lambda i: (0, i, 0)
    vec_names = ("g_attn", "w_in", "g_qa", "w_qb", "g_kva", "w_kvb",
                 "gq_a", "gq_b", "gk_a", "gk_b", "g_q_swa", "g_k_swa")
    vec_args = [wts[k] for k in vec_names]
    in_specs = ([pl.BlockSpec((tm, d), row)] + [_resident(a.shape) for a in vec_args]
                + [pl.BlockSpec((tm, LANES), tab)] * 4)
    n_qs = SWA_HEADS * SWA_HEAD_DIM
    n_ks = SWA_KV_HEADS * SWA_HEAD_DIM
    out_shape = (
        jax.ShapeDtypeStruct((MLA_HEADS, MLA_PAD, n), BF16),
        jax.ShapeDtypeStruct((MLA_HEADS, n, MLA_PAD), BF16),
        jax.ShapeDtypeStruct((MLA_HEADS, MLA_V, n), BF16),
        jax.ShapeDtypeStruct((n, n_qs), BF16),
        jax.ShapeDtypeStruct((n, n_ks), BF16),
        jax.ShapeDtypeStruct((n, n_ks), BF16),
    )
    out_specs = (
        pl.BlockSpec((MLA_HEADS, MLA_PAD, tm), lambda i: (0, 0, i)),
        pl.BlockSpec((MLA_HEADS, tm, MLA_PAD), head_rows),
        pl.BlockSpec((MLA_HEADS, MLA_V, tm), lambda i: (0, 0, i)),
        pl.BlockSpec((tm, n_qs), row),
        pl.BlockSpec((tm, n_ks), row),
        pl.BlockSpec((tm, n_ks), row),
    )
    return pl.pallas_call(
        _proj_kernel, grid=(n // tm,), in_specs=in_specs, out_specs=out_specs,
        out_shape=out_shape, compiler_params=_params("parallel"), name="proj",
    )(x, *vec_args, *tabs)


def _mla_kernel(qt_ref, k_ref, vt_ref, o_ref, *scratch, ck, track_max):
    j = pl.program_id(3)
    tk, tq = k_ref.shape[1], qt_ref.shape[2]
    if track_max:
        m_ref, l_ref, acc_ref = scratch
    else:
        l_ref, acc_ref, p_ref = scratch

    @pl.when(j == 0)
    def _():
        if track_max:
            m_ref[...] = jnp.full_like(m_ref, -jnp.inf)
        l_ref[...] = jnp.zeros_like(l_ref)
        acc_ref[...] = jnp.zeros_like(acc_ref)

    qt = qt_ref[0]
    if track_max:
        s = jnp.dot(k_ref[0], qt, preferred_element_type=F32)
        m_prev = m_ref[...]
        m_new = jnp.maximum(m_prev, jnp.max(s, axis=0, keepdims=True))
        alpha = jnp.exp2(m_prev - m_new)
        p = jnp.exp2(s - m_new)
        l_ref[...] = alpha * l_ref[...] + jnp.sum(p, axis=0, keepdims=True)
        acc_ref[...] = alpha * acc_ref[...] + jnp.dot(vt_ref[0], p.astype(BF16), preferred_element_type=F32)
        m_ref[...] = m_new
    else:
        l_part = l_ref[...]
        for c in range(tk // ck):
            s = jnp.dot(k_ref[0, c * ck:(c + 1) * ck, :], qt, preferred_element_type=F32)
            p = jnp.exp2(s)
            l_part = l_part + jnp.sum(p.reshape(ck // SUBLANES, SUBLANES, tq), axis=0)
            p_ref[c * ck:(c + 1) * ck, :] = p.astype(BF16)
        l_ref[...] = l_part
        acc_ref[...] += jnp.dot(vt_ref[0], p_ref[...], preferred_element_type=F32)

    @pl.when(j == pl.num_programs(3) - 1)
    def _():
        l = jnp.sum(l_ref[...], axis=0, keepdims=True)
        o_ref[...] = (acc_ref[...] / l).T.astype(o_ref.dtype)


def _mla_attention(qt, k, vt, batch, seq, tq, tk, ck, track_max):
    n = batch * seq
    tq, tk = min(tq, seq), min(tk, seq)
    nq, nk = seq // tq, seq // tk
    if track_max:
        scratch = [pltpu.VMEM((1, tq), F32), pltpu.VMEM((1, tq), F32), pltpu.VMEM((MLA_V, tq), F32)]
    else:
        scratch = [pltpu.VMEM((SUBLANES, tq), F32), pltpu.VMEM((MLA_V, tq), F32), pltpu.VMEM((tk, tq), BF16)]
    return pl.pallas_call(
        functools.partial(_mla_kernel, ck=min(ck, tk), track_max=track_max),
        grid=(batch, MLA_HEADS, nq, nk),
        in_specs=[
            pl.BlockSpec((1, MLA_PAD, tq), lambda b, h, i, j: (h, 0, b * nq + i)),
            pl.BlockSpec((1, tk, MLA_PAD), lambda b, h, i, j: (h, b * nk + j, 0)),
            pl.BlockSpec((1, MLA_V, tk), lambda b, h, i, j: (h, 0, b * nk + j)),
        ],
        out_specs=pl.BlockSpec((tq, MLA_V), lambda b, h, i, j: (b * nq + i, h)),
        out_shape=jax.ShapeDtypeStruct((n, MLA_HEADS * MLA_V), BF16),
        scratch_shapes=scratch,
        compiler_params=_params("parallel", "parallel", "parallel", "arbitrary"),
        name="mla_attn_online" if track_max else "mla_attn",
    )(qt, k, vt)


def _swa_kernel(sink_ref, q_ref, kp_ref, km_ref, kn_ref, vp_ref, vm_ref, vn_ref, o_ref, *, tq, nq):
    g = pl.program_id(1)
    i = pl.program_id(2)
    neg = -jnp.inf
    rows = lax.broadcasted_iota(jnp.int32, (tq, tq), 0)
    cols = lax.broadcasted_iota(jnp.int32, (tq, tq), 1)
    mask_m = jnp.abs(rows - cols) <= WINDOW
    rows_e = lax.broadcasted_iota(jnp.int32, (tq, WINDOW), 0)
    cols_e = lax.broadcasted_iota(jnp.int32, (tq, WINDOW), 1)
    mask_p = (cols_e >= rows_e) & (i > 0)
    mask_n = (rows_e - cols_e >= tq - WINDOW) & (i < nq - 1)
    nt = (((1,), (1,)), ((), ()))
    for hd in range(SWA_GROUP):
        q = q_ref[:, hd * SWA_HEAD_DIM:(hd + 1) * SWA_HEAD_DIM]
        s_p = jnp.where(mask_p, lax.dot_general(q, kp_ref[...], nt, preferred_element_type=F32), neg)
        s_m = jnp.where(mask_m, lax.dot_general(q, km_ref[...], nt, preferred_element_type=F32), neg)
        s_n = jnp.where(mask_n, lax.dot_general(q, kn_ref[...], nt, preferred_element_type=F32), neg)
        sink = sink_ref[g * SWA_GROUP + hd] * LOG2E
        m = jnp.maximum(jnp.maximum(jnp.max(s_p, axis=-1, keepdims=True), jnp.max(s_m, axis=-1, keepdims=True)),
                        jnp.maximum(jnp.max(s_n, axis=-1, keepdims=True), sink))
        p_p, p_m, p_n = jnp.exp2(s_p - m), jnp.exp2(s_m - m), jnp.exp2(s_n - m)
        denom = (jnp.sum(p_p, axis=-1, keepdims=True) + jnp.sum(p_m, axis=-1, keepdims=True)
                 + jnp.sum(p_n, axis=-1, keepdims=True) + jnp.exp2(sink - m))
        o = (jnp.dot(p_p.astype(BF16), vp_ref[...], preferred_element_type=F32)
             + jnp.dot(p_m.astype(BF16), vm_ref[...], preferred_element_type=F32)
             + jnp.dot(p_n.astype(BF16), vn_ref[...], preferred_element_type=F32))
        o_ref[:, hd * SWA_HEAD_DIM:(hd + 1) * SWA_HEAD_DIM] = (o / denom).astype(o_ref.dtype)


def _swa_attention(sink, q, k, v, batch, seq, tq):
    n = batch * seq
    tq = min(tq, seq)
    nq = seq // tq
    r = tq // WINDOW
    nb = seq // WINDOW
    gw = SWA_GROUP * SWA_HEAD_DIM
    main = lambda b, g, i: (b * nq + i, g)
    prev = lambda b, g, i: (b * nb + jnp.maximum(i * r - 1, 0), g)
    nxt = lambda b, g, i: (b * nb + jnp.minimum((i + 1) * r, nb - 1), g)
    edge = (WINDOW, SWA_HEAD_DIM)
    return pl.pallas_call(
        functools.partial(_swa_kernel, tq=tq, nq=nq),
        grid=(batch, SWA_KV_HEADS, nq),
        in_specs=[
            pl.BlockSpec(memory_space=pltpu.SMEM),
            pl.BlockSpec((tq, gw), main),
            pl.BlockSpec(edge, prev), pl.BlockSpec((tq, SWA_HEAD_DIM), main), pl.BlockSpec(edge, nxt),
            pl.BlockSpec(edge, prev), pl.BlockSpec((tq, SWA_HEAD_DIM), main), pl.BlockSpec(edge, nxt),
        ],
        out_specs=pl.BlockSpec((tq, gw), main),
        out_shape=jax.ShapeDtypeStruct((n, SWA_HEADS * SWA_HEAD_DIM), BF16),
        compiler_params=_params("parallel", "parallel", "parallel"),
        name="swa_attn",
    )(sink, q, k, k, k, v, v, v)


def _out_proj_kernel(x_ref, oa_ref, ob_ref, ga_ref, gb_ref, wa_ref, wb_ref, g_mlp_ref, x1_ref, hm_ref):
    oa = _rms(oa_ref[...].astype(F32), ga_ref[...]).astype(BF16)
    ob = _rms(ob_ref[...].astype(F32), gb_ref[...]).astype(BF16)
    x1 = (x_ref[...] + jnp.dot(oa, wa_ref[...], preferred_element_type=F32)
          + jnp.dot(ob, wb_ref[...], preferred_element_type=F32))
    x1_ref[...] = x1
    hm_ref[...] = _rms(x1, g_mlp_ref[...]).astype(BF16)


def _out_proj(x, oa, ob, wts, tm):
    n, d = x.shape
    tm = min(tm, n)
    row = lambda i: (i, 0)
    consts = [wts[k] for k in ("g_out_mla", "g_out_swa", "w_out_a", "w_out_b", "g_mlp")]
    return pl.pallas_call(
        _out_proj_kernel, grid=(n // tm,),
        in_specs=[pl.BlockSpec((tm, d), row), pl.BlockSpec((tm, oa.shape[1]), row),
                  pl.BlockSpec((tm, ob.shape[1]), row)] + [_resident(a.shape) for a in consts],
        out_specs=(pl.BlockSpec((tm, d), row), pl.BlockSpec((tm, d), row)),
        out_shape=(jax.ShapeDtypeStruct((n, d), F32), jax.ShapeDtypeStruct((n, d), BF16)),
        compiler_params=_params("parallel"), name="out_proj",
    )(x, oa, ob, *consts)


def _mlp_kernel(x1_ref, hm_ref, wu_ref, wd_ref, o_ref):
    j = pl.program_id(1)
    up = jnp.dot(hm_ref[...], wu_ref[...], preferred_element_type=F32)
    act = jnp.square(jnp.maximum(up, 0.0)).astype(BF16)
    contrib = jnp.dot(act, wd_ref[...], preferred_element_type=F32)

    @pl.when(j == 0)
    def _():
        o_ref[...] = x1_ref[...] + contrib

    @pl.when(j > 0)
    def _():
        o_ref[...] += contrib


def _mlp(x1, hm, w_up, w_down, tm, tf):
    n, d = x1.shape
    f = w_up.shape[1]
    tm, tf = min(tm, n), min(tf, f)
    row = lambda i, j: (i, 0)
    return pl.pallas_call(
        _mlp_kernel, grid=(n // tm, f // tf),
        in_specs=[pl.BlockSpec((tm, d), row), pl.BlockSpec((tm, d), row),
                  pl.BlockSpec((d, tf), lambda i, j: (0, j)), pl.BlockSpec((tf, d), lambda i, j: (j, 0))],
        out_specs=pl.BlockSpec((tm, d), row),
        out_shape=jax.ShapeDtypeStruct((n, d), F32),
        compiler_params=_params("parallel", "arbitrary"), name="mlp",
    )(x1, hm, w_up, w_down)


def _ple_kernel(x_ref, p_ref, g_ref, wg_ref, wp_ref, o_ref):
    x = x_ref[...]
    z = jnp.dot(_rms(x, g_ref[...]).astype(BF16), wg_ref[...], preferred_element_type=F32)
    gate = 1.0 / (1.0 + jnp.exp(-z))
    e = jnp.dot(p_ref[...].astype(BF16), wp_ref[...], preferred_element_type=F32)
    o_ref[...] = x + e * gate


def _ple(x2, p, wts, tm):
    n, d = x2.shape
    tm = min(tm, n)
    row = lambda i: (i, 0)
    consts = [wts[k] for k in ("g_ple", "w_gate", "w_ple")]
    return pl.pallas_call(
        _ple_kernel, grid=(n // tm,),
        in_specs=[pl.BlockSpec((tm, d), row), pl.BlockSpec((tm, p.shape[1]), row)]
                 + [_resident(a.shape) for a in consts],
        out_specs=pl.BlockSpec((tm, d), row),
        out_shape=jax.ShapeDtypeStruct((n, d), F32),
        compiler_params=_params("parallel"), name="ple",
    )(x2, p, *consts)


def _swap_halves(a):
    half = a.shape[-1] // 2
    return jnp.concatenate([a[..., half:], a[..., :half]], axis=-1)


def _rope_tables(seq, dim):
    inv = 1.0 / (ROPE_THETA ** (jnp.arange(0, dim, 2, dtype=F32) / dim))
    ang = jnp.arange(seq, dtype=F32)[:, None] * inv[None, :]
    cos, sin = jnp.cos(ang), jnp.sin(ang)
    pad = jnp.zeros((seq, LANES - dim), F32)
    return (jnp.concatenate([cos, cos, pad], axis=-1), jnp.concatenate([-sin, sin, pad], axis=-1))


def _prepare_weights(g_attn, w_in, g_qa, w_qb, g_kva, w_kvb, g_qn_mla, g_kn_mla, g_q_swa, g_k_swa,
                     g_out_mla, g_out_swa, w_out, g_mlp, w_up, w_down, g_ple, w_ple, w_gate):
    row = lambda g: g.reshape(1, -1)
    o_kpe = MLA_Q_LORA + MLA_KV_LORA
    kpe = w_in[:, o_kpe:o_kpe + MLA_ROPE]
    w_in_r = jnp.concatenate([w_in[:, :o_kpe], kpe, _swap_halves(kpe), w_in[:, o_kpe + MLA_ROPE:]], axis=1)
    wq = w_qb.reshape(MLA_Q_LORA, MLA_HEADS, MLA_QK)
    rope = wq[..., MLA_NOPE:]
    wq_r = jnp.concatenate([wq[..., :MLA_NOPE], rope, _swap_halves(rope)], axis=-1)
    rope_gain = lambda g: jnp.concatenate([g[MLA_NOPE:], _swap_halves(g[MLA_NOPE:])]).reshape(1, -1)
    n_a = MLA_HEADS * MLA_V
    logit_span = 2.0 * MLA_QK ** 0.5 * LOG2E * jnp.max(jnp.abs(g_qn_mla)) * jnp.max(jnp.abs(g_kn_mla))
    return {
        "mla_logit_span": logit_span,
        "g_attn": row(g_attn), "w_in": w_in_r.astype(BF16),
        "g_qa": row(g_qa), "w_qb": wq_r.reshape(MLA_Q_LORA, MLA_HEADS * MLA_PAD).astype(BF16),
        "g_kva": row(g_kva), "w_kvb": w_kvb.astype(BF16),
        "gq_a": row(g_qn_mla[:MLA_NOPE]), "gq_b": rope_gain(g_qn_mla),
        "gk_a": row(g_kn_mla[:MLA_NOPE]), "gk_b": rope_gain(g_kn_mla),
        "g_q_swa": row(g_q_swa), "g_k_swa": row(g_k_swa),
        "g_out_mla": row(g_out_mla), "g_out_swa": row(g_out_swa),
        "w_out_a": w_out[:n_a].astype(BF16), "w_out_b": w_out[n_a:].astype(BF16),
        "g_mlp": row(g_mlp), "w_up": w_up.astype(BF16), "w_down": w_down.astype(BF16),
        "g_ple": row(g_ple), "w_gate": w_gate.astype(BF16), "w_ple": w_ple.astype(BF16),
    }


def _layer(x, p, sink, wts):
    batch, seq, d = x.shape
    n = batch * seq
    xf = x.reshape(n, d)
    tabs = _rope_tables(seq, MLA_ROPE) + _rope_tables(seq, SWA_HEAD_DIM)
    qt, k, vt, qs, ks, vs = _proj(xf, seq, wts, tabs, tm=256)
    o_a = lax.cond(
        wts["mla_logit_span"] <= SAFE_LOG2_RANGE,
        functools.partial(_mla_attention, batch=batch, seq=seq, tq=512, tk=2048, ck=256, track_max=False),
        functools.partial(_mla_attention, batch=batch, seq=seq, tq=512, tk=512, ck=512, track_max=True),
        qt, k, vt)
    o_b = _swa_attention(sink, qs, ks, vs, batch, seq, tq=512)
    x1, hm = _out_proj(xf, o_a, o_b, wts, tm=512)
    x2 = _mlp(x1, hm, wts["w_up"], wts["w_down"], tm=512, tf=512)
    y = _ple(x2, p.reshape(n, -1), wts, tm=512)
    return y.reshape(batch, seq, d)


def kernel(x_prompt, x_sample, p_prompt, p_sample, g_attn, w_in, g_qa, w_qb, g_kva, w_kvb, g_qn_mla, g_kn_mla,
           g_q_swa, g_k_swa, sink, g_out_mla, g_out_swa, w_out, g_mlp, w_up, w_down, g_ple, w_ple, w_gate):
    y_prompt, y_sample = x_prompt, x_sample
    for l in range(g_attn.shape[0]):
        wts = _prepare_weights(g_attn[l], w_in[l], g_qa[l], w_qb[l], g_kva[l], w_kvb[l], g_qn_mla[l],
                               g_kn_mla[l], g_q_swa[l], g_k_swa[l], g_out_mla[l], g_out_swa[l], w_out[l],
                               g_mlp[l], w_up[l], w_down[l], g_ple[l], w_ple[l], w_gate[l])
        y_prompt = _layer(y_prompt, p_prompt[l], sink[l], wts)
        y_sample = _layer(y_sample, p_sample[l], sink[l], wts)
    return (y_prompt, y_sample)
```

```python
import functools
import math

import jax
import jax.numpy as jnp
from jax import lax
from jax.experimental import pallas as pl
from jax.experimental.pallas import tpu as pltpu

EPS = 1e-6
ROPE_THETA = 10000.0
WINDOW = 128
MLA_HEADS = 8
MLA_Q_LORA = 512
MLA_KV_LORA = 256
MLA_NOPE = 128
MLA_ROPE = 64
MLA_V = 128
MLA_QK = MLA_NOPE + MLA_ROPE
SWA_HEADS = 8
SWA_KV_HEADS = 2
SWA_GROUP = SWA_HEADS // SWA_KV_HEADS
SWA_HEAD_DIM = 128
LANES = 128
SUBLANES = 8
SAFE_LOG2_RANGE = 100.0
MLA_PAD = 2 * LANES
LOG2E = math.log2(math.e)
VMEM_LIMIT = 56 * 1024 * 1024

BF16 = jnp.bfloat16
F32 = jnp.float32


def _params(*sem):
    return pltpu.CompilerParams(dimension_semantics=sem, vmem_limit_bytes=VMEM_LIMIT)


def _resident(shape):
    nd = len(shape)
    return pl.BlockSpec(shape, lambda *_: (0,) * nd, pipeline_mode=pl.Buffered(1))


def _rms(x, g):
    return x * lax.rsqrt(jnp.mean(x * x, axis=-1, keepdims=True) + EPS) * g


def _rope(x, t_cos, t_sin):
    return x * t_cos + pltpu.roll(x, LANES // 2, axis=1) * t_sin


def _proj_kernel(x_ref, g_attn_ref, w_in_ref, g_qa_ref, w_qb_ref, g_kva_ref, w_kvb_ref,
                 gq_a_ref, gq_b_ref, gk_a_ref, gk_b_ref, gqs_ref, gks_ref,
                 tm_cos_ref, tm_sin_ref, ts_cos_ref, ts_sin_ref,
                 qt_ref, k_ref, vt_ref, qs_ref, ks_ref, vs_ref):
    h = _rms(x_ref[...], g_attn_ref[...]).astype(BF16)
    proj = jnp.dot(h, w_in_ref[...], preferred_element_type=F32)
    o_ckv = MLA_Q_LORA
    o_kpe = o_ckv + MLA_KV_LORA
    o_qs = o_kpe + LANES
    o_ks = o_qs + SWA_HEADS * SWA_HEAD_DIM
    o_vs = o_ks + SWA_KV_HEADS * SWA_HEAD_DIM

    tm_cos, tm_sin = tm_cos_ref[...], tm_sin_ref[...]
    ts_cos, ts_sin = ts_cos_ref[...], ts_sin_ref[...]

    shift_lane = (lax.broadcasted_iota(jnp.int32, (1, LANES), 1) == MLA_QK - LANES).astype(F32)
    k_bound = MLA_QK ** 0.5 * jnp.maximum(jnp.max(jnp.abs(gk_a_ref[...]), axis=-1, keepdims=True),
                                          jnp.max(jnp.abs(gk_b_ref[...]), axis=-1, keepdims=True))

    cqn = _rms(proj[:, :o_ckv], g_qa_ref[...]).astype(BF16)
    q = jnp.dot(cqn, w_qb_ref[...], preferred_element_type=F32)
    q_scale = MLA_QK ** -0.5 * LOG2E
    for hd in range(MLA_HEADS):
        a = q[:, hd * MLA_PAD: hd * MLA_PAD + LANES]
        b = q[:, hd * MLA_PAD + LANES: (hd + 1) * MLA_PAD]
        ss = jnp.sum(a * a, axis=-1, keepdims=True) + 0.5 * jnp.sum(b * b, axis=-1, keepdims=True)
        r = lax.rsqrt(ss * (1.0 / MLA_QK) + EPS) * q_scale
        qa = a * gq_a_ref[...] * r
        qb = _rope(b * gq_b_ref[...], tm_cos, tm_sin) * r
        q_norm = jnp.sqrt(jnp.sum(qa * qa, axis=-1, keepdims=True) + jnp.sum(qb * qb, axis=-1, keepdims=True))
        qb = qb - shift_lane * (q_norm * k_bound)
        qt_ref[hd, :LANES, :] = qa.T.astype(BF16)
        qt_ref[hd, LANES:, :] = qb.T.astype(BF16)

    ckvn = _rms(proj[:, o_ckv:o_kpe], g_kva_ref[...]).astype(BF16)
    kv = jnp.dot(ckvn, w_kvb_ref[...], preferred_element_type=F32)
    kb = proj[:, o_kpe:o_qs]
    s_pe = 0.5 * jnp.sum(kb * kb, axis=-1, keepdims=True)
    kr = _rope(kb * gk_b_ref[...], tm_cos, tm_sin)
    for hd in range(MLA_HEADS):
        kn = kv[:, hd * MLA_PAD: hd * MLA_PAD + LANES]
        v = kv[:, hd * MLA_PAD + LANES: (hd + 1) * MLA_PAD]
        ss = jnp.sum(kn * kn, axis=-1, keepdims=True) + s_pe
        r = lax.rsqrt(ss * (1.0 / MLA_QK) + EPS)
        k_ref[hd, :, :LANES] = (kn * gk_a_ref[...] * r).astype(BF16)
        k_ref[hd, :, LANES:] = (kr * r + shift_lane).astype(BF16)
        vt_ref[hd] = v.T.astype(BF16)

    s_scale = SWA_HEAD_DIM ** -0.5 * LOG2E
    for hd in range(SWA_HEADS):
        xs = proj[:, o_qs + hd * SWA_HEAD_DIM: o_qs + (hd + 1) * SWA_HEAD_DIM]
        qn = _rope(_rms(xs, gqs_ref[...]), ts_cos, ts_sin) * s_scale
        qs_ref[:, hd * SWA_HEAD_DIM:(hd + 1) * SWA_HEAD_DIM] = qn.astype(BF16)
    for hd in range(SWA_KV_HEADS):
        xs = proj[:, o_ks + hd * SWA_HEAD_DIM: o_ks + (hd + 1) * SWA_HEAD_DIM]
        kn = _rope(_rms(xs, gks_ref[...]), ts_cos, ts_sin)
        ks_ref[:, hd * SWA_HEAD_DIM:(hd + 1) * SWA_HEAD_DIM] = kn.astype(BF16)
    vs_ref[...] = proj[:, o_vs:].astype(BF16)


def _proj(x, seq, wts, tabs, tm):
    n, d = x.shape
    tm = min(tm, seq)
    nt_seq = seq // tm
    row = lambda i: (i, 0)
    tab = lambda i: (i % nt_seq, 0)
    head_rows = lambda i: (0, i, 0)
    vec_names = ("g_attn", "w_in", "g_qa", "w_qb", "g_kva", "w_kvb",
                 "gq_a", "gq_b", "gk_a", "gk_b", "g_q_swa", "g_k_swa")
    vec_args = [wts[k] for k in vec_names]
    in_specs = ([pl.BlockSpec((tm, d), row)] + [_resident(a.shape) for a in vec_args]
                + [pl.BlockSpec((tm, LANES), tab)] * 4)
    n_qs = SWA_HEADS * SWA_HEAD_DIM
    n_ks = SWA_KV_HEADS * SWA_HEAD_DIM
    out_shape = (
        jax.ShapeDtypeStruct((MLA_HEADS, MLA_PAD, n), BF16),
        jax.ShapeDtypeStruct((MLA_HEADS, n, MLA_PAD), BF16),
        jax.ShapeDtypeStruct((MLA_HEADS, MLA_V, n), BF16),
        jax.ShapeDtypeStruct((n, n_qs), BF16),
        jax.ShapeDtypeStruct((n, n_ks), BF16),
        jax.ShapeDtypeStruct((n, n_ks), BF16),
    )
    out_specs = (
        pl.BlockSpec((MLA_HEADS, MLA_PAD, tm), lambda i: (0, 0, i)),
        pl.BlockSpec((MLA_HEADS, tm, MLA_PAD), head_rows),
        pl.BlockSpec((MLA_HEADS, MLA_V, tm), lambda i: (0, 0, i)),
        pl.BlockSpec((tm, n_qs), row),
        pl.BlockSpec((tm, n_ks), row),
        pl.BlockSpec((tm, n_ks), row),
    )
    return pl.pallas_call(
        _proj_kernel, grid=(n // tm,), in_specs=in_specs, out_specs=out_specs,
        out_shape=out_shape, compiler_params=_params("parallel"), name="proj",
    )(x, *vec_args, *tabs)


def _mla_kernel(qt_ref, k_ref, vt_ref, o_ref, *scratch, ck, track_max):
    j = pl.program_id(3)
    tk, tq = k_ref.shape[1], qt_ref.shape[2]
    if track_max:
        m_ref, l_ref, acc_ref = scratch
    else:
        l_ref, acc_ref, p_ref = scratch

    @pl.when(j == 0)
    def _():
        if track_max:
            m_ref[...] = jnp.full_like(m_ref, -jnp.inf)
        l_ref[...] = jnp.zeros_like(l_ref)
        acc_ref[...] = jnp.zeros_like(acc_ref)

    qt = qt_ref[0]
    if track_max:
        s = jnp.dot(k_ref[0], qt, preferred_element_type=F32)
        m_prev = m_ref[...]
        m_new = jnp.maximum(m_prev, jnp.max(s, axis=0, keepdims=True))
        alpha = jnp.exp2(m_prev - m_new)
        p = jnp.exp2(s - m_new)
        l_ref[...] = alpha * l_ref[...] + jnp.sum(p, axis=0, keepdims=True)
        acc_ref[...] = alpha * acc_ref[...] + jnp.dot(vt_ref[0], p.astype(BF16), preferred_element_type=F32)
        m_ref[...] = m_new
    else:
        l_part = l_ref[...]
        for c in range(tk // ck):
            s = jnp.dot(k_ref[0, c * ck:(c + 1) * ck, :], qt, preferred_element_type=F32)
            p = jnp.exp2(s)
            l_part = l_part + jnp.sum(p.reshape(ck // SUBLANES, SUBLANES, tq), axis=0)
            p_ref[c * ck:(c + 1) * ck, :] = p.astype(BF16)
        l_ref[...] = l_part
        acc_ref[...] += jnp.dot(vt_ref[0], p_ref[...], preferred_element_type=F32)

    @pl.when(j == pl.num_programs(3) - 1)
    def _():
        l = jnp.sum(l_ref[...], axis=0, keepdims=True)
        o_ref[...] = (acc_ref[...] / l).T.astype(o_ref.dtype)


def _mla_attention(qt, k, vt, batch, seq, tq, tk, ck, track_max):
    n = batch * seq
    tq, tk = min(tq, seq), min(tk, seq)
    nq, nk = seq // tq, seq // tk
    if track_max:
        scratch = [pltpu.VMEM((1, tq), F32), pltpu.VMEM((1, tq), F32), pltpu.VMEM((MLA_V, tq), F32)]
    else:
        scratch = [pltpu.VMEM((SUBLANES, tq), F32), pltpu.VMEM((MLA_V, tq), F32), pltpu.VMEM((tk, tq), BF16)]
    return pl.pallas_call(
        functools.partial(_mla_kernel, ck=min(ck, tk), track_max=track_max),
        grid=(batch, MLA_HEADS, nq, nk),
        in_specs=[
            pl.BlockSpec((1, MLA_PAD, tq), lambda b, h, i, j: (h, 0, b * nq + i)),
            pl.BlockSpec((1, tk, MLA_PAD), lambda b, h, i, j: (h, b * nk + j, 0)),
            pl.BlockSpec((1, MLA_V, tk), lambda b, h, i, j: (h, 0, b * nk + j)),
        ],
        out_specs=pl.BlockSpec((tq, MLA_V), lambda b, h, i, j: (b * nq + i, h)),
        out_shape=jax.ShapeDtypeStruct((n, MLA_HEADS * MLA_V), BF16),
        scratch_shapes=scratch,
        compiler_params=_params("parallel", "parallel", "parallel", "arbitrary"),
        name="mla_attn_online" if track_max else "mla_attn",
    )(qt, k, vt)


def _swa_kernel(sink_ref, q_ref, kp_ref, km_ref, kn_ref, vp_ref, vm_ref, vn_ref, o_ref, *, tq, nq):
    g = pl.program_id(1)
    i = pl.program_id(2)
    neg = -jnp.inf
    rows = lax.broadcasted_iota(jnp.int32, (tq, tq), 0)
    cols = lax.broadcasted_iota(jnp.int32, (tq, tq), 1)
    mask_m = jnp.abs(rows - cols) <= WINDOW
    rows_e = lax.broadcasted_iota(jnp.int32, (tq, WINDOW), 0)
    cols_e = lax.broadcasted_iota(jnp.int32, (tq, WINDOW), 1)
    mask_p = (cols_e >= rows_e) & (i > 0)
    mask_n = (rows_e - cols_e >= tq - WINDOW) & (i < nq - 1)
    nt = (((1,), (1,)), ((), ()))
    for hd in range(SWA_GROUP):
        q = q_ref[:, hd * SWA_HEAD_DIM:(hd + 1) * SWA_HEAD_DIM]
        s_p = jnp.where(mask_p, lax.dot_general(q, kp_ref[...], nt, preferred_element_type=F32), neg)
        s_m = jnp.where(mask_m, lax.dot_general(q, km_ref[...], nt, preferred_element_type=F32), neg)
        s_n = jnp.where(mask_n, lax.dot_general(q, kn_ref[...], nt, preferred_element_type=F32), neg)
        sink = sink_ref[g * SWA_GROUP + hd] * LOG2E
        m = jnp.maximum(jnp.maximum(jnp.max(s_p, axis=-1, keepdims=True), jnp.max(s_m, axis=-1, keepdims=True)),
                        jnp.maximum(jnp.max(s_n, axis=-1, keepdims=True), sink))
        p_p, p_m, p_n = jnp.exp2(s_p - m), jnp.exp2(s_m - m), jnp.exp2(s_n - m)
        denom = (jnp.sum(p_p, axis=-1, keepdims=True) + jnp.sum(p_m, axis=-1, keepdims=True)
                 + jnp.sum(p_n, axis=-1, keepdims=True) + jnp.exp2(sink - m))
        o = (jnp.dot(p_p.astype(BF16), vp_ref[...], preferred_element_type=F32)
             + jnp.dot(p_m.astype(BF16), vm_ref[...], preferred_element_type=F32)
             + jnp.dot(p_n.astype(BF16), vn_ref[...], preferred_element_type=F32))
        o_ref[:, hd * SWA_HEAD_DIM:(hd + 1) * SWA_HEAD_DIM] = (o / denom).astype(o_ref.dtype)


def _swa_attention(sink, q, k, v, batch, seq, tq):
    n = batch * seq
    tq = min(tq, seq)
    nq = seq // tq
    r = tq // WINDOW
    nb = seq // WINDOW
    gw = SWA_GROUP * SWA_HEAD_DIM
    main = lambda b, g, i: (b * nq + i, g)
    prev = lambda b, g, i: (b * nb + jnp.maximum(i * r - 1, 0), g)
    nxt = lambda b, g, i: (b * nb + jnp.minimum((i + 1) * r, nb - 1), g)
    edge = (WINDOW, SWA_HEAD_DIM)
    return pl.pallas_call(
        functools.partial(_swa_kernel, tq=tq, nq=nq),
        grid=(batch, SWA_KV_HEADS, nq),
        in_specs=[
            pl.BlockSpec(memory_space=pltpu.SMEM),
            pl.BlockSpec((tq, gw), main),
            pl.BlockSpec(edge, prev), pl.BlockSpec((tq, SWA_HEAD_DIM), main), pl.BlockSpec(edge, nxt),
            pl.BlockSpec(edge, prev), pl.BlockSpec((tq, SWA_HEAD_DIM), main), pl.BlockSpec(edge, nxt),
        ],
        out_specs=pl.BlockSpec((tq, gw), main),
        out_shape=jax.ShapeDtypeStruct((n, SWA_HEADS * SWA_HEAD_DIM), BF16),
        compiler_params=_params("parallel", "parallel", "parallel"),
        name="swa_attn",
    )(sink, q, k, k, k, v, v, v)


def _out_proj_kernel(x_ref, oa_ref, ob_ref, ga_ref, gb_ref, wa_ref, wb_ref, g_mlp_ref, x1_ref, hm_ref):
    oa = _rms(oa_ref[...].astype(F32), ga_ref[...]).astype(BF16)
    ob = _rms(ob_ref[...].astype(F32), gb_ref[...]).astype(BF16)
    x1 = (x_ref[...] + jnp.dot(oa, wa_ref[...], preferred_element_type=F32)
          + jnp.dot(ob, wb_ref[...], preferred_element_type=F32))
    x1_ref[...] = x1
    hm_ref[...] = _rms(x1, g_mlp_ref[...]).astype(BF16)


def _out_proj(x, oa, ob, wts, tm):
    n, d = x.shape
    tm = min(tm, n)
    row = lambda i: (i, 0)
    consts = [wts[k] for k in ("g_out_mla", "g_out_swa", "w_out_a", "w_out_b", "g_mlp")]
    return pl.pallas_call(
        _out_proj_kernel, grid=(n // tm,),
        in_specs=[pl.BlockSpec((tm, d), row), pl.BlockSpec((tm, oa.shape[1]), row),
                  pl.BlockSpec((tm, ob.shape[1]), row)] + [_resident(a.shape) for a in consts],
        out_specs=(pl.BlockSpec((tm, d), row), pl.BlockSpec((tm, d), row)),
        out_shape=(jax.ShapeDtypeStruct((n, d), F32), jax.ShapeDtypeStruct((n, d), BF16)),
        compiler_params=_params("parallel"), name="out_proj",
    )(x, oa, ob, *consts)


def _mlp_kernel(x1_ref, hm_ref, wu_ref, wd_ref, o_ref):
    @pl.when(pl.program_id(1) == 0)
    def _():
        o_ref[...] = x1_ref[...]

    up = jnp.dot(hm_ref[...], wu_ref[...], preferred_element_type=F32)
    act = jnp.square(jnp.maximum(up, 0.0)).astype(BF16)
    o_ref[...] += jnp.dot(act, wd_ref[...], preferred_element_type=F32)


def _mlp(x1, hm, w_up, w_down, tm, tf):
    n, d = x1.shape
    f = w_up.shape[1]
    tm, tf = min(tm, n), min(tf, f)
    row = lambda i, j: (i, 0)
    return pl.pallas_call(
        _mlp_kernel, grid=(n // tm, f // tf),
        in_specs=[pl.BlockSpec((tm, d), row, pipeline_mode=pl.Buffered(1)), pl.BlockSpec((tm, d), row),
                  pl.BlockSpec((d, tf), lambda i, j: (0, j)), pl.BlockSpec((tf, d), lambda i, j: (j, 0))],
        out_specs=pl.BlockSpec((tm, d), row),
        out_shape=jax.ShapeDtypeStruct((n, d), F32),
        compiler_params=_params("parallel", "arbitrary"), name="mlp",
    )(x1, hm, w_up, w_down)


def _ple_kernel(x_ref, p_ref, g_ref, wg_ref, wp_ref, o_ref):
    x = x_ref[...]
    z = jnp.dot(_rms(x, g_ref[...]).astype(BF16), wg_ref[...], preferred_element_type=F32)
    gate = 1.0 / (1.0 + jnp.exp(-z))
    e = jnp.dot(p_ref[...].astype(BF16), wp_ref[...], preferred_element_type=F32)
    o_ref[...] = x + e * gate


def _ple(x2, p, wts, tm):
    n, d = x2.shape
    tm = min(tm, n)
    row = lambda i: (i, 0)
    consts = [wts[k] for k in ("g_ple", "w_gate", "w_ple")]
    return pl.pallas_call(
        _ple_kernel, grid=(n // tm,),
        in_specs=[pl.BlockSpec((tm, d), row), pl.BlockSpec((tm, p.shape[1]), row)]
                 + [_resident(a.shape) for a in consts],
        out_specs=pl.BlockSpec((tm, d), row),
        out_shape=jax.ShapeDtypeStruct((n, d), F32),
        compiler_params=_params("parallel"), name="ple",
    )(x2, p, *consts)


def _swap_halves(a):
    half = a.shape[-1] // 2
    return jnp.concatenate([a[..., half:], a[..., :half]], axis=-1)


def _rope_tables(seq, dim):
    inv = 1.0 / (ROPE_THETA ** (jnp.arange(0, dim, 2, dtype=F32) / dim))
    ang = jnp.arange(seq, dtype=F32)[:, None] * inv[None, :]
    cos, sin = jnp.cos(ang), jnp.sin(ang)
    pad = jnp.zeros((seq, LANES - dim), F32)
    return (jnp.concatenate([cos, cos, pad], axis=-1), jnp.concatenate([-sin, sin, pad], axis=-1))


def _prepare_weights(g_attn, w_in, g_qa, w_qb, g_kva, w_kvb, g_qn_mla, g_kn_mla, g_q_swa, g_k_swa,
                     g_out_mla, g_out_swa, w_out, g_mlp, w_up, w_down, g_ple, w_ple, w_gate):
    row = lambda g: g.reshape(1, -1)
    o_kpe = MLA_Q_LORA + MLA_KV_LORA
    kpe = w_in[:, o_kpe:o_kpe + MLA_ROPE]
    w_in_r = jnp.concatenate([w_in[:, :o_kpe], kpe, _swap_halves(kpe), w_in[:, o_kpe + MLA_ROPE:]], axis=1)
    wq = w_qb.reshape(MLA_Q_LORA, MLA_HEADS, MLA_QK)
    rope = wq[..., MLA_NOPE:]
    wq_r = jnp.concatenate([wq[..., :MLA_NOPE], rope, _swap_halves(rope)], axis=-1)
    rope_gain = lambda g: jnp.concatenate([g[MLA_NOPE:], _swap_halves(g[MLA_NOPE:])]).reshape(1, -1)
    n_a = MLA_HEADS * MLA_V
    logit_span = 2.0 * MLA_QK ** 0.5 * LOG2E * jnp.max(jnp.abs(g_qn_mla)) * jnp.max(jnp.abs(g_kn_mla))
    return {
        "mla_logit_span": logit_span,
        "g_attn": row(g_attn), "w_in": w_in_r.astype(BF16),
        "g_qa": row(g_qa), "w_qb": wq_r.reshape(MLA_Q_LORA, MLA_HEADS * MLA_PAD).astype(BF16),
        "g_kva": row(g_kva), "w_kvb": w_kvb.astype(BF16),
        "gq_a": row(g_qn_mla[:MLA_NOPE]), "gq_b": rope_gain(g_qn_mla),
        "gk_a": row(g_kn_mla[:MLA_NOPE]), "gk_b": rope_gain(g_kn_mla),
        "g_q_swa": row(g_q_swa), "g_k_swa": row(g_k_swa),
        "g_out_mla": row(g_out_mla), "g_out_swa": row(g_out_swa),
        "w_out_a": w_out[:n_a].astype(BF16), "w_out_b": w_out[n_a:].astype(BF16),
        "g_mlp": row(g_mlp), "w_up": w_up.astype(BF16), "w_down": w_down.astype(BF16),
        "g_ple": row(g_ple), "w_gate": w_gate.astype(BF16), "w_ple": w_ple.astype(BF16),
    }


def _layer(x, p, sink, wts):
    batch, seq, d = x.shape
    n = batch * seq
    xf = x.reshape(n, d)
    tabs = _rope_tables(seq, MLA_ROPE) + _rope_tables(seq, SWA_HEAD_DIM)
    qt, k, vt, qs, ks, vs = _proj(xf, seq, wts, tabs, tm=256)
    o_a = lax.cond(
        wts["mla_logit_span"] <= SAFE_LOG2_RANGE,
        functools.partial(_mla_attention, batch=batch, seq=seq, tq=1024, tk=4096, ck=256, track_max=False),
        functools.partial(_mla_attention, batch=batch, seq=seq, tq=512, tk=512, ck=512, track_max=True),
        qt, k, vt)
    o_b = _swa_attention(sink, qs, ks, vs, batch, seq, tq=512)
    x1, hm = _out_proj(xf, o_a, o_b, wts, tm=512)
    x2 = _mlp(x1, hm, wts["w_up"], wts["w_down"], tm=1024, tf=512)
    y = _ple(x2, p.reshape(n, -1), wts, tm=512)
    return y.reshape(batch, seq, d)


def kernel(x_prompt, x_sample, p_prompt, p_sample, g_attn, w_in, g_qa, w_qb, g_kva, w_kvb, g_qn_mla, g_kn_mla,
           g_q_swa, g_k_swa, sink, g_out_mla, g_out_swa, w_out, g_mlp, w_up, w_down, g_ple, w_ple, w_gate):
    y_prompt, y_sample = x_prompt, x_sample
    for l in range(g_attn.shape[0]):
        wts = _prepare_weights(g_attn[l], w_in[l], g_qa[l], w_qb[l], g_kva[l], w_kvb[l], g_qn_mla[l],
                               g_kn_mla[l], g_q_swa[l], g_k_swa[l], g_out_mla[l], g_out_swa[l], w_out[l],
                               g_mlp[l], w_up[l], w_down[l], g_ple[l], w_ple[l], w_gate[l])
        y_prompt = _layer(y_prompt, p_prompt[l], sink[l], wts)
        y_sample = _layer(y_sample, p_sample[l], sink[l], wts)
    return (y_prompt, y_sample)
```

```python
import functools
import math

import jax
import jax.numpy as jnp
from jax import lax
from jax.experimental import pallas as pl
from jax.experimental.pallas import tpu as pltpu

EPS = 1e-6
ROPE_THETA = 10000.0
WINDOW = 128
MLA_HEADS = 8
MLA_Q_LORA = 512
MLA_KV_LORA = 256
MLA_NOPE = 128
MLA_ROPE = 64
MLA_V = 128
MLA_QK = MLA_NOPE + MLA_ROPE
SWA_HEADS = 8
SWA_KV_HEADS = 2
SWA_GROUP = SWA_HEADS // SWA_KV_HEADS
SWA_HEAD_DIM = 128
LANES = 128
SUBLANES = 8
MLA_PAD = 2 * LANES
SAFE_LOG2_RANGE = 100.0
LOG2E = math.log2(math.e)
VMEM_LIMIT = 56 * 1024 * 1024

BF16 = jnp.bfloat16
F32 = jnp.float32
NT = (((1,), (1,)), ((), ()))


def _params(*sem):
    return pltpu.CompilerParams(dimension_semantics=sem, vmem_limit_bytes=VMEM_LIMIT)


def _resident(shape):
    nd = len(shape)
    return pl.BlockSpec(shape, lambda *_: (0,) * nd, pipeline_mode=pl.Buffered(1))


def _rms(x, g):
    return x * lax.rsqrt(jnp.mean(x * x, axis=-1, keepdims=True) + EPS) * g


def _rope_lanes(x, t_cos, t_sin):
    return x * t_cos + pltpu.roll(x, LANES // 2, axis=1) * t_sin


def _rope_rows(x1, x2, cos, sin):
    return x1 * cos - x2 * sin, x2 * cos + x1 * sin


def _proj_kernel(x_ref, g_attn_ref, w_tm_ref, w_fm_ref, g_qa_ref, w_qbt_ref, g_kva_ref, w_kn_ref, w_vt_ref,
                 gq_ref, gk_a_ref, gk_b_ref, gqs_ref, gks_ref,
                 mc_ref, ms_ref, sc_ref, ss_ref, tkc_ref, tks_ref, tsc_ref, tss_ref,
                 qt_ref, k_ref, vt_ref, qst_ref, ks_ref, vst_ref):
    tm = x_ref.shape[0]
    h = _rms(x_ref[...], g_attn_ref[...]).astype(BF16)
    p_tm = jnp.dot(h, w_tm_ref[...], preferred_element_type=F32)
    o_ckv = MLA_Q_LORA
    o_kpe = o_ckv + MLA_KV_LORA
    o_ks = o_kpe + LANES

    k_bound = MLA_QK ** 0.5 * jnp.maximum(jnp.max(jnp.abs(gk_a_ref[...]), axis=-1, keepdims=True),
                                          jnp.max(jnp.abs(gk_b_ref[...]), axis=-1, keepdims=True))
    shift_lane = (lax.broadcasted_iota(jnp.int32, (1, LANES), 1) == MLA_QK - LANES).astype(F32)
    first_row = lax.broadcasted_iota(jnp.int32, (MLA_PAD - MLA_QK, tm), 0) == 0

    cqn = _rms(p_tm[:, :o_ckv], g_qa_ref[...]).astype(BF16)
    q_t = lax.dot_general(w_qbt_ref[...], cqn, NT, preferred_element_type=F32)
    q_scale = MLA_QK ** -0.5 * LOG2E
    half = MLA_ROPE // 2
    for hd in range(MLA_HEADS):
        blk = q_t[hd * MLA_QK:(hd + 1) * MLA_QK]
        r = lax.rsqrt(jnp.sum(blk * blk, axis=0, keepdims=True) * (1.0 / MLA_QK) + EPS) * q_scale
        y = blk * gq_ref[...] * r
        shift = -(jnp.sqrt(jnp.sum(y * y, axis=0, keepdims=True)) * k_bound)
        o1, o2 = _rope_rows(y[MLA_NOPE:MLA_NOPE + half], y[MLA_NOPE + half:], mc_ref[...], ms_ref[...])
        qt_ref[hd, :MLA_NOPE, :] = y[:MLA_NOPE].astype(BF16)
        qt_ref[hd, MLA_NOPE:MLA_NOPE + half, :] = o1.astype(BF16)
        qt_ref[hd, MLA_NOPE + half:MLA_QK, :] = o2.astype(BF16)
        qt_ref[hd, MLA_QK:, :] = jnp.where(first_row, shift, 0.0).astype(BF16)

    ckvn = _rms(p_tm[:, o_ckv:o_kpe], g_kva_ref[...]).astype(BF16)
    kn_all = jnp.dot(ckvn, w_kn_ref[...], preferred_element_type=F32)
    kb = p_tm[:, o_kpe:o_ks]
    s_pe = 0.5 * jnp.sum(kb * kb, axis=-1, keepdims=True)
    kr = _rope_lanes(kb * gk_b_ref[...], tkc_ref[...], tks_ref[...])
    for hd in range(MLA_HEADS):
        kn = kn_all[:, hd * MLA_NOPE:(hd + 1) * MLA_NOPE]
        r = lax.rsqrt((jnp.sum(kn * kn, axis=-1, keepdims=True) + s_pe) * (1.0 / MLA_QK) + EPS)
        k_ref[hd, :, :LANES] = (kn * gk_a_ref[...] * r).astype(BF16)
        k_ref[hd, :, LANES:] = (kr * r + shift_lane).astype(BF16)
    v_t = lax.dot_general(w_vt_ref[...], ckvn, NT, preferred_element_type=F32)
    for hd in range(MLA_HEADS):
        vt_ref[hd] = v_t[hd * MLA_V:(hd + 1) * MLA_V].astype(BF16)

    p_fm = lax.dot_general(w_fm_ref[...], h, NT, preferred_element_type=F32)
    s_scale = SWA_HEAD_DIM ** -0.5 * LOG2E
    sh = SWA_HEAD_DIM // 2
    for hd in range(SWA_HEADS):
        blk = p_fm[hd * SWA_HEAD_DIM:(hd + 1) * SWA_HEAD_DIM]
        r = lax.rsqrt(jnp.mean(blk * blk, axis=0, keepdims=True) + EPS) * s_scale
        y = blk * gqs_ref[...] * r
        o1, o2 = _rope_rows(y[:sh], y[sh:], sc_ref[...], ss_ref[...])
        qst_ref[hd * SWA_HEAD_DIM:hd * SWA_HEAD_DIM + sh, :] = o1.astype(BF16)
        qst_ref[hd * SWA_HEAD_DIM + sh:(hd + 1) * SWA_HEAD_DIM, :] = o2.astype(BF16)
    vst_ref[...] = p_fm[SWA_HEADS * SWA_HEAD_DIM:].astype(BF16)
    for hd in range(SWA_KV_HEADS):
        xs = p_tm[:, o_ks + hd * SWA_HEAD_DIM: o_ks + (hd + 1) * SWA_HEAD_DIM]
        kn = _rope_lanes(_rms(xs, gks_ref[...]), tsc_ref[...], tss_ref[...])
        ks_ref[:, hd * SWA_HEAD_DIM:(hd + 1) * SWA_HEAD_DIM] = kn.astype(BF16)


def _proj(x, seq, wts, tm):
    n, d = x.shape
    tm = min(tm, seq)
    nt_seq = seq // tm
    row = lambda i: (i, 0)
    col = lambda i: (0, i)
    bcast = lambda g: jnp.broadcast_to(g.reshape(-1, 1), (g.size, tm))
    consts = [wts["g_attn"], wts["w_tm"], wts["w_fm"], wts["g_qa"], wts["w_qbt"], wts["g_kva"], wts["w_kn"],
              wts["w_vt"], bcast(wts["g_qn_mla"]), wts["gk_a"], wts["gk_b"], bcast(wts["g_q_swa"]), wts["g_k_swa"]]
    m_cos, m_sin = _rope_tables(seq, MLA_ROPE)
    s_cos, s_sin = _rope_tables(seq, SWA_HEAD_DIM)
    fm_tabs = [m_cos.T, m_sin.T, s_cos.T, s_sin.T]
    tm_tabs = list(_lane_tables(m_cos, m_sin) + _lane_tables(s_cos, s_sin))
    in_specs = ([pl.BlockSpec((tm, d), row)] + [_resident(a.shape) for a in consts]
                + [pl.BlockSpec((t.shape[0], tm), lambda i: (0, i % nt_seq)) for t in fm_tabs]
                + [pl.BlockSpec((tm, LANES), lambda i: (i % nt_seq, 0))] * 4)
    n_qs = SWA_HEADS * SWA_HEAD_DIM
    n_ks = SWA_KV_HEADS * SWA_HEAD_DIM
    out_shape = (
        jax.ShapeDtypeStruct((MLA_HEADS, MLA_PAD, n), BF16),
        jax.ShapeDtypeStruct((MLA_HEADS, n, MLA_PAD), BF16),
        jax.ShapeDtypeStruct((MLA_HEADS, MLA_V, n), BF16),
        jax.ShapeDtypeStruct((n_qs, n), BF16),
        jax.ShapeDtypeStruct((n, n_ks), BF16),
        jax.ShapeDtypeStruct((n_ks, n), BF16),
    )
    out_specs = (
        pl.BlockSpec((MLA_HEADS, MLA_PAD, tm), lambda i: (0, 0, i)),
        pl.BlockSpec((MLA_HEADS, tm, MLA_PAD), lambda i: (0, i, 0)),
        pl.BlockSpec((MLA_HEADS, MLA_V, tm), lambda i: (0, 0, i)),
        pl.BlockSpec((n_qs, tm), col),
        pl.BlockSpec((tm, n_ks), row),
        pl.BlockSpec((n_ks, tm), col),
    )
    return pl.pallas_call(
        _proj_kernel, grid=(n // tm,), in_specs=in_specs, out_specs=out_specs,
        out_shape=out_shape, compiler_params=_params("parallel"), name="proj",
    )(x, *consts, *fm_tabs, *tm_tabs)


def _mla_kernel(qt_ref, k_ref, vt_ref, o_ref, *scratch, ck, track_max):
    j = pl.program_id(3)
    tk, tq = k_ref.shape[1], qt_ref.shape[2]
    if track_max:
        m_ref, l_ref, acc_ref = scratch
    else:
        l_ref, acc_ref, p_ref = scratch

    @pl.when(j == 0)
    def _():
        if track_max:
            m_ref[...] = jnp.full_like(m_ref, -jnp.inf)
        l_ref[...] = jnp.zeros_like(l_ref)
        acc_ref[...] = jnp.zeros_like(acc_ref)

    qt = qt_ref[0]
    if track_max:
        s = jnp.dot(k_ref[0], qt, preferred_element_type=F32)
        m_prev = m_ref[...]
        m_new = jnp.maximum(m_prev, jnp.max(s, axis=0, keepdims=True))
        alpha = jnp.exp2(m_prev - m_new)
        p = jnp.exp2(s - m_new)
        l_ref[...] = alpha * l_ref[...] + jnp.sum(p, axis=0, keepdims=True)
        acc_ref[...] = alpha * acc_ref[...] + jnp.dot(vt_ref[0], p.astype(BF16), preferred_element_type=F32)
        m_ref[...] = m_new
    else:
        l_part = l_ref[...]
        for c in range(tk // ck):
            s = jnp.dot(k_ref[0, c * ck:(c + 1) * ck, :], qt, preferred_element_type=F32)
            p = jnp.exp2(s)
            l_part = l_part + jnp.sum(p.reshape(ck // SUBLANES, SUBLANES, tq), axis=0)
            p_ref[c * ck:(c + 1) * ck, :] = p.astype(BF16)
        l_ref[...] = l_part
        acc_ref[...] += jnp.dot(vt_ref[0], p_ref[...], preferred_element_type=F32)

    @pl.when(j == pl.num_programs(3) - 1)
    def _():
        l = jnp.sum(l_ref[...], axis=0, keepdims=True)
        o_ref[...] = (acc_ref[...] / l).T.astype(o_ref.dtype)


def _mla_attention(qt, k, vt, batch, seq, tq, tk, ck, track_max):
    n = batch * seq
    tq, tk = min(tq, seq), min(tk, seq)
    nq, nk = seq // tq, seq // tk
    if track_max:
        scratch = [pltpu.VMEM((1, tq), F32), pltpu.VMEM((1, tq), F32), pltpu.VMEM((MLA_V, tq), F32)]
    else:
        scratch = [pltpu.VMEM((SUBLANES, tq), F32), pltpu.VMEM((MLA_V, tq), F32), pltpu.VMEM((tk, tq), BF16)]
    return pl.pallas_call(
        functools.partial(_mla_kernel, ck=min(ck, tk), track_max=track_max),
        grid=(batch, MLA_HEADS, nq, nk),
        in_specs=[
            pl.BlockSpec((1, MLA_PAD, tq), lambda b, h, i, j: (h, 0, b * nq + i)),
            pl.BlockSpec((1, tk, MLA_PAD), lambda b, h, i, j: (h, b * nk + j, 0)),
            pl.BlockSpec((1, MLA_V, tk), lambda b, h, i, j: (h, 0, b * nk + j)),
        ],
        out_specs=pl.BlockSpec((tq, MLA_V), lambda b, h, i, j: (b * nq + i, h)),
        out_shape=jax.ShapeDtypeStruct((n, MLA_HEADS * MLA_V), BF16),
        scratch_shapes=scratch,
        compiler_params=_params("parallel", "parallel", "parallel", "arbitrary"),
        name="mla_attn_online" if track_max else "mla_attn",
    )(qt, k, vt)


def _swa_kernel(sink_ref, qt_ref, kp_ref, km_ref, kn_ref, vp_ref, vm_ref, vn_ref, o_ref, kwin_ref, vwin_ref,
                *, nq, sub):
    g = pl.program_id(1)
    i = pl.program_id(2)
    tq = qt_ref.shape[1]
    kwin_ref[:WINDOW] = kp_ref[...]
    kwin_ref[WINDOW:WINDOW + tq] = km_ref[...]
    kwin_ref[WINDOW + tq:] = kn_ref[...]
    vwin_ref[:, :WINDOW] = vp_ref[...]
    vwin_ref[:, WINDOW:WINDOW + tq] = vm_ref[...]
    vwin_ref[:, WINDOW + tq:] = vn_ref[...]
    heads = range(SWA_GROUP)
    ws = sub + 2 * WINDOW
    sink = jnp.concatenate(
        [jnp.full((1, sub), sink_ref[g * SWA_GROUP + hd] * LOG2E, F32) for hd in heads], axis=1)
    key = lax.broadcasted_iota(jnp.int32, (ws, SWA_GROUP * sub), 0)
    qry = lax.broadcasted_iota(jnp.int32, (ws, SWA_GROUP * sub), 1) & (sub - 1)
    band = (key >= qry) & (key <= qry + 2 * WINDOW)
    no_prev = jnp.where(i > 0, 0.0, -jnp.inf)
    no_next = jnp.where(i < nq - 1, 0.0, -jnp.inf)
    for t in range(tq // sub):
        q_all = jnp.concatenate(
            [qt_ref[hd * SWA_HEAD_DIM:(hd + 1) * SWA_HEAD_DIM, t * sub:(t + 1) * sub] for hd in heads], axis=1)
        s = jnp.dot(kwin_ref[t * sub:t * sub + ws, :], q_all, preferred_element_type=F32)
        s = jnp.where(band, s, -jnp.inf)
        if t == 0:
            s = jnp.concatenate([s[:WINDOW] + no_prev, s[WINDOW:]], axis=0)
        if t == tq // sub - 1:
            s = jnp.concatenate([s[:ws - WINDOW], s[ws - WINDOW:] + no_next], axis=0)
        m = jnp.maximum(jnp.max(s, axis=0, keepdims=True), sink)
        p = jnp.exp2(s - m)
        denom = jnp.sum(p, axis=0, keepdims=True) + jnp.exp2(sink - m)
        o_t = jnp.dot(vwin_ref[:, t * sub:t * sub + ws], p.astype(BF16), preferred_element_type=F32) / denom
        for hd in heads:
            o_ref[t * sub:(t + 1) * sub, hd * SWA_HEAD_DIM:(hd + 1) * SWA_HEAD_DIM] = (
                o_t[:, hd * sub:(hd + 1) * sub].T.astype(o_ref.dtype))


def _swa_attention(sink, qt, k, vt, batch, seq, tq, sub):
    n = batch * seq
    tq = min(tq, seq)
    sub = min(sub, tq)
    assert sub & (sub - 1) == 0 and tq % sub == 0, "query index within a sub-tile is taken with a bit mask"
    nq = seq // tq
    r = tq // WINDOW
    nb = seq // WINDOW
    gw = SWA_GROUP * SWA_HEAD_DIM
    prev = lambda b, i: b * nb + jnp.maximum(i * r - 1, 0)
    nxt = lambda b, i: b * nb + jnp.minimum((i + 1) * r, nb - 1)
    edge = (WINDOW, SWA_HEAD_DIM)
    return pl.pallas_call(
        functools.partial(_swa_kernel, nq=nq, sub=sub),
        grid=(batch, SWA_KV_HEADS, nq),
        in_specs=[
            pl.BlockSpec(memory_space=pltpu.SMEM),
            pl.BlockSpec((gw, tq), lambda b, g, i: (g, b * nq + i)),
            pl.BlockSpec(edge, lambda b, g, i: (prev(b, i), g)),
            pl.BlockSpec((tq, SWA_HEAD_DIM), lambda b, g, i: (b * nq + i, g)),
            pl.BlockSpec(edge, lambda b, g, i: (nxt(b, i), g)),
            pl.BlockSpec(edge, lambda b, g, i: (g, prev(b, i))),
            pl.BlockSpec((SWA_HEAD_DIM, tq), lambda b, g, i: (g, b * nq + i)),
            pl.BlockSpec(edge, lambda b, g, i: (g, nxt(b, i))),
        ],
        out_specs=pl.BlockSpec((tq, gw), lambda b, g, i: (b * nq + i, g)),
        out_shape=jax.ShapeDtypeStruct((n, SWA_HEADS * SWA_HEAD_DIM), BF16),
        scratch_shapes=[pltpu.VMEM((tq + 2 * WINDOW, SWA_HEAD_DIM), BF16),
                        pltpu.VMEM((SWA_HEAD_DIM, tq + 2 * WINDOW), BF16)],
        compiler_params=_params("parallel", "parallel", "parallel"),
        name="swa_attn",
    )(sink, qt, k, k, k, vt, vt, vt)


def _out_proj_kernel(x_ref, oa_ref, ob_ref, ga_ref, gb_ref, wa_ref, wb_ref, g_mlp_ref, x1_ref, hm_ref):
    oa = _rms(oa_ref[...].astype(F32), ga_ref[...]).astype(BF16)
    ob = _rms(ob_ref[...].astype(F32), gb_ref[...]).astype(BF16)
    x1 = (x_ref[...] + jnp.dot(oa, wa_ref[...], preferred_element_type=F32)
          + jnp.dot(ob, wb_ref[...], preferred_element_type=F32))
    x1_ref[...] = x1
    hm_ref[...] = _rms(x1, g_mlp_ref[...]).astype(BF16)


def _out_proj(x, oa, ob, wts, tm):
    n, d = x.shape
    tm = min(tm, n)
    row = lambda i: (i, 0)
    consts = [wts[k] for k in ("g_out_mla", "g_out_swa", "w_out_a", "w_out_b", "g_mlp")]
    return pl.pallas_call(
        _out_proj_kernel, grid=(n // tm,),
        in_specs=[pl.BlockSpec((tm, d), row), pl.BlockSpec((tm, oa.shape[1]), row),
                  pl.BlockSpec((tm, ob.shape[1]), row)] + [_resident(a.shape) for a in consts],
        out_specs=(pl.BlockSpec((tm, d), row), pl.BlockSpec((tm, d), row)),
        out_shape=(jax.ShapeDtypeStruct((n, d), F32), jax.ShapeDtypeStruct((n, d), BF16)),
        compiler_params=_params("parallel"), name="out_proj",
    )(x, oa, ob, *consts)


def _mlp_kernel(x1_ref, hm_ref, wu_ref, wd_ref, o_ref):
    @pl.when(pl.program_id(1) == 0)
    def _():
        o_ref[...] = x1_ref[...]

    up = jnp.dot(hm_ref[...], wu_ref[...], preferred_element_type=F32)
    act = jnp.square(jnp.maximum(up, 0.0)).astype(BF16)
    o_ref[...] += jnp.dot(act, wd_ref[...], preferred_element_type=F32)


def _mlp(x1, hm, w_up, w_down, tm, tf):
    n, d = x1.shape
    f = w_up.shape[1]
    tm, tf = min(tm, n), min(tf, f)
    row = lambda i, j: (i, 0)
    return pl.pallas_call(
        _mlp_kernel, grid=(n // tm, f // tf),
        in_specs=[pl.BlockSpec((tm, d), row, pipeline_mode=pl.Buffered(1)), pl.BlockSpec((tm, d), row),
                  pl.BlockSpec((d, tf), lambda i, j: (0, j)), pl.BlockSpec((tf, d), lambda i, j: (j, 0))],
        out_specs=pl.BlockSpec((tm, d), row),
        out_shape=jax.ShapeDtypeStruct((n, d), F32),
        compiler_params=_params("parallel", "arbitrary"), name="mlp",
    )(x1, hm, w_up, w_down)


def _ple_kernel(x_ref, p_ref, g_ref, wg_ref, wp_ref, o_ref):
    x = x_ref[...]
    z = jnp.dot(_rms(x, g_ref[...]).astype(BF16), wg_ref[...], preferred_element_type=F32)
    gate = 1.0 / (1.0 + jnp.exp(-z))
    e = jnp.dot(p_ref[...].astype(BF16), wp_ref[...], preferred_element_type=F32)
    o_ref[...] = x + e * gate


def _ple(x2, p, wts, tm):
    n, d = x2.shape
    tm = min(tm, n)
    row = lambda i: (i, 0)
    consts = [wts[k] for k in ("g_ple", "w_gate", "w_ple")]
    return pl.pallas_call(
        _ple_kernel, grid=(n // tm,),
        in_specs=[pl.BlockSpec((tm, d), row), pl.BlockSpec((tm, p.shape[1]), row)]
                 + [_resident(a.shape) for a in consts],
        out_specs=pl.BlockSpec((tm, d), row),
        out_shape=jax.ShapeDtypeStruct((n, d), F32),
        compiler_params=_params("parallel"), name="ple",
    )(x2, p, *consts)


def _swap_halves(a):
    half = a.shape[-1] // 2
    return jnp.concatenate([a[..., half:], a[..., :half]], axis=-1)


def _rope_tables(seq, dim):
    inv = 1.0 / (ROPE_THETA ** (jnp.arange(0, dim, 2, dtype=F32) / dim))
    ang = jnp.arange(seq, dtype=F32)[:, None] * inv[None, :]
    return jnp.cos(ang), jnp.sin(ang)


def _lane_tables(cos, sin):
    pad = jnp.zeros((cos.shape[0], LANES - 2 * cos.shape[1]), F32)
    return (jnp.concatenate([cos, cos, pad], axis=-1), jnp.concatenate([-sin, sin, pad], axis=-1))


def _prepare_weights(g_attn, w_in, g_qa, w_qb, g_kva, w_kvb, g_qn_mla, g_kn_mla, g_q_swa, g_k_swa,
                     g_out_mla, g_out_swa, w_out, g_mlp, w_up, w_down, g_ple, w_ple, w_gate):
    row = lambda g: g.reshape(1, -1)
    o_kpe = MLA_Q_LORA + MLA_KV_LORA
    o_qs = o_kpe + MLA_ROPE
    o_ks = o_qs + SWA_HEADS * SWA_HEAD_DIM
    o_vs = o_ks + SWA_KV_HEADS * SWA_HEAD_DIM
    kpe = w_in[:, o_kpe:o_qs]
    w_tm = jnp.concatenate([w_in[:, :o_kpe], kpe, _swap_halves(kpe), w_in[:, o_ks:o_vs]], axis=1)
    w_fm = jnp.concatenate([w_in[:, o_qs:o_ks], w_in[:, o_vs:]], axis=1).T
    w_kv = w_kvb.reshape(MLA_KV_LORA, MLA_HEADS, MLA_NOPE + MLA_V)
    w_kn = w_kv[..., :MLA_NOPE].reshape(MLA_KV_LORA, MLA_HEADS * MLA_NOPE)
    w_vt = w_kv[..., MLA_NOPE:].reshape(MLA_KV_LORA, MLA_HEADS * MLA_V).T
    g_rope = g_kn_mla[MLA_NOPE:]
    n_a = MLA_HEADS * MLA_V
    logit_span = 2.0 * MLA_QK ** 0.5 * LOG2E * jnp.max(jnp.abs(g_qn_mla)) * jnp.max(jnp.abs(g_kn_mla))
    return {
        "mla_logit_span": logit_span,
        "g_attn": row(g_attn), "w_tm": w_tm.astype(BF16), "w_fm": w_fm.astype(BF16),
        "g_qa": row(g_qa), "w_qbt": w_qb.T.astype(BF16),
        "g_kva": row(g_kva), "w_kn": w_kn.astype(BF16), "w_vt": w_vt.astype(BF16),
        "g_qn_mla": g_qn_mla, "gk_a": row(g_kn_mla[:MLA_NOPE]),
        "gk_b": row(jnp.concatenate([g_rope, _swap_halves(g_rope)])),
        "g_q_swa": g_q_swa, "g_k_swa": row(g_k_swa),
        "g_out_mla": row(g_out_mla), "g_out_swa": row(g_out_swa),
        "w_out_a": w_out[:n_a].astype(BF16), "w_out_b": w_out[n_a:].astype(BF16),
        "g_mlp": row(g_mlp), "w_up": w_up.astype(BF16), "w_down": w_down.astype(BF16),
        "g_ple": row(g_ple), "w_gate": w_gate.astype(BF16), "w_ple": w_ple.astype(BF16),
    }


def _layer(x, p, sink, wts):
    batch, seq, d = x.shape
    n = batch * seq
    xf = x.reshape(n, d)
    qt, k, vt, qst, ks, vst = _proj(xf, seq, wts, tm=256)
    o_a = lax.cond(
        wts["mla_logit_span"] <= SAFE_LOG2_RANGE,
        functools.partial(_mla_attention, batch=batch, seq=seq, tq=1024, tk=4096, ck=256, track_max=False),
        functools.partial(_mla_attention, batch=batch, seq=seq, tq=512, tk=512, ck=512, track_max=True),
        qt, k, vt)
    o_b = _swa_attention(sink, qst, ks, vst, batch, seq, tq=512, sub=256)
    x1, hm = _out_proj(xf, o_a, o_b, wts, tm=512)
    x2 = _mlp(x1, hm, wts["w_up"], wts["w_down"], tm=1024, tf=512)
    y = _ple(x2, p.reshape(n, -1), wts, tm=512)
    return y.reshape(batch, seq, d)


def kernel(x_prompt, x_sample, p_prompt, p_sample, g_attn, w_in, g_qa, w_qb, g_kva, w_kvb, g_qn_mla, g_kn_mla,
           g_q_swa, g_k_swa, sink, g_out_mla, g_out_swa, w_out, g_mlp, w_up, w_down, g_ple, w_ple, w_gate):
    y_prompt, y_sample = x_prompt, x_sample
    for l in range(g_attn.shape[0]):
        wts = _prepare_weights(g_attn[l], w_in[l], g_qa[l], w_qb[l], g_kva[l], w_kvb[l], g_qn_mla[l],
                               g_kn_mla[l], g_q_swa[l], g_k_swa[l], g_out_mla[l], g_out_swa[l], w_out[l],
                               g_mlp[l], w_up[l], w_down[l], g_ple[l], w_ple[l], w_gate[l])
        y_prompt = _layer(y_prompt, p_prompt[l], sink[l], wts)
        y_sample = _layer(y_sample, p_sample[l], sink[l], wts)
    return (y_prompt, y_sample)
```

```python
import functools
import math

import jax
import jax.numpy as jnp
from jax import lax
from jax.experimental import pallas as pl
from jax.experimental.pallas import tpu as pltpu

EPS = 1e-6
ROPE_THETA = 10000.0
WINDOW = 128
MLA_HEADS = 8
MLA_Q_LORA = 512
MLA_KV_LORA = 256
MLA_NOPE = 128
MLA_ROPE = 64
MLA_V = 128
MLA_QK = MLA_NOPE + MLA_ROPE
SWA_HEADS = 8
SWA_KV_HEADS = 2
SWA_GROUP = SWA_HEADS // SWA_KV_HEADS
SWA_HEAD_DIM = 128
LANES = 128
SUBLANES = 8
MLA_PAD = 2 * LANES
SAFE_LOG2_RANGE = 100.0
LOG2E = math.log2(math.e)
VMEM_LIMIT = 56 * 1024 * 1024

BF16 = jnp.bfloat16
F32 = jnp.float32
NT = (((1,), (1,)), ((), ()))


def _params(*sem):
    return pltpu.CompilerParams(dimension_semantics=sem, vmem_limit_bytes=VMEM_LIMIT)


def _resident(shape):
    nd = len(shape)
    return pl.BlockSpec(shape, lambda *_: (0,) * nd, pipeline_mode=pl.Buffered(1))


def _rms(x, g):
    return x * lax.rsqrt(jnp.mean(x * x, axis=-1, keepdims=True) + EPS) * g


def _rope_lanes(x, t_cos, t_sin):
    return x * t_cos + pltpu.roll(x, LANES // 2, axis=1) * t_sin


def _rope_rows(x1, x2, cos, sin):
    return x1 * cos - x2 * sin, x2 * cos + x1 * sin


def _proj_kernel(x_ref, g_attn_ref, w_tm_ref, w_fm_ref, g_qa_ref, w_qbt_ref, g_kva_ref, w_kn_ref, w_vt_ref,
                 gq_ref, gk_a_ref, gk_b_ref, gqs_ref, gks_ref,
                 mc_ref, ms_ref, sc_ref, ss_ref, tkc_ref, tks_ref, tsc_ref, tss_ref,
                 qt_ref, k_ref, vt_ref, qst_ref, ks_ref, vst_ref):
    tm = x_ref.shape[0]
    h = _rms(x_ref[...], g_attn_ref[...]).astype(BF16)
    p_tm = jnp.dot(h, w_tm_ref[...], preferred_element_type=F32)
    o_ckv = MLA_Q_LORA
    o_kpe = o_ckv + MLA_KV_LORA
    o_ks = o_kpe + LANES

    k_bound = MLA_QK ** 0.5 * jnp.maximum(jnp.max(jnp.abs(gk_a_ref[...]), axis=-1, keepdims=True),
                                          jnp.max(jnp.abs(gk_b_ref[...]), axis=-1, keepdims=True))
    shift_lane = (lax.broadcasted_iota(jnp.int32, (1, LANES), 1) == MLA_QK - LANES).astype(F32)
    first_row = lax.broadcasted_iota(jnp.int32, (MLA_PAD - MLA_QK, tm), 0) == 0

    cqn = _rms(p_tm[:, :o_ckv], g_qa_ref[...]).astype(BF16)
    q_t = lax.dot_general(w_qbt_ref[...], cqn, NT, preferred_element_type=F32)
    q_scale = MLA_QK ** -0.5 * LOG2E
    half = MLA_ROPE // 2
    for hd in range(MLA_HEADS):
        blk = q_t[hd * MLA_QK:(hd + 1) * MLA_QK]
        r = lax.rsqrt(jnp.sum(blk * blk, axis=0, keepdims=True) * (1.0 / MLA_QK) + EPS) * q_scale
        y = blk * gq_ref[...] * r
        shift = -(jnp.sqrt(jnp.sum(y * y, axis=0, keepdims=True)) * k_bound)
        o1, o2 = _rope_rows(y[MLA_NOPE:MLA_NOPE + half], y[MLA_NOPE + half:], mc_ref[...], ms_ref[...])
        qt_ref[hd, :MLA_NOPE, :] = y[:MLA_NOPE].astype(BF16)
        qt_ref[hd, MLA_NOPE:MLA_NOPE + half, :] = o1.astype(BF16)
        qt_ref[hd, MLA_NOPE + half:MLA_QK, :] = o2.astype(BF16)
        qt_ref[hd, MLA_QK:, :] = jnp.where(first_row, shift, 0.0).astype(BF16)

    ckvn = _rms(p_tm[:, o_ckv:o_kpe], g_kva_ref[...]).astype(BF16)
    kn_all = jnp.dot(ckvn, w_kn_ref[...], preferred_element_type=F32)
    kb = p_tm[:, o_kpe:o_ks]
    s_pe = 0.5 * jnp.sum(kb * kb, axis=-1, keepdims=True)
    kr = _rope_lanes(kb * gk_b_ref[...], tkc_ref[...], tks_ref[...])
    for hd in range(MLA_HEADS):
        kn = kn_all[:, hd * MLA_NOPE:(hd + 1) * MLA_NOPE]
        r = lax.rsqrt((jnp.sum(kn * kn, axis=-1, keepdims=True) + s_pe) * (1.0 / MLA_QK) + EPS)
        k_ref[hd, :, :LANES] = (kn * gk_a_ref[...] * r).astype(BF16)
        k_ref[hd, :, LANES:] = (kr * r + shift_lane).astype(BF16)
    v_t = lax.dot_general(w_vt_ref[...], ckvn, NT, preferred_element_type=F32)
    for hd in range(MLA_HEADS):
        vt_ref[hd] = v_t[hd * MLA_V:(hd + 1) * MLA_V].astype(BF16)

    p_fm = lax.dot_general(w_fm_ref[...], h, NT, preferred_element_type=F32)
    s_scale = SWA_HEAD_DIM ** -0.5 * LOG2E
    sh = SWA_HEAD_DIM // 2
    for hd in range(SWA_HEADS):
        blk = p_fm[hd * SWA_HEAD_DIM:(hd + 1) * SWA_HEAD_DIM]
        r = lax.rsqrt(jnp.mean(blk * blk, axis=0, keepdims=True) + EPS) * s_scale
        y = blk * gqs_ref[...] * r
        o1, o2 = _rope_rows(y[:sh], y[sh:], sc_ref[...], ss_ref[...])
        qst_ref[hd * SWA_HEAD_DIM:hd * SWA_HEAD_DIM + sh, :] = o1.astype(BF16)
        qst_ref[hd * SWA_HEAD_DIM + sh:(hd + 1) * SWA_HEAD_DIM, :] = o2.astype(BF16)
    vst_ref[...] = p_fm[SWA_HEADS * SWA_HEAD_DIM:].astype(BF16)
    for hd in range(SWA_KV_HEADS):
        xs = p_tm[:, o_ks + hd * SWA_HEAD_DIM: o_ks + (hd + 1) * SWA_HEAD_DIM]
        kn = _rope_lanes(_rms(xs, gks_ref[...]), tsc_ref[...], tss_ref[...])
        ks_ref[:, hd * SWA_HEAD_DIM:(hd + 1) * SWA_HEAD_DIM] = kn.astype(BF16)


def _rope_inputs(seq):
    m_cos, m_sin = _rope_tables(seq, MLA_ROPE)
    s_cos, s_sin = _rope_tables(seq, SWA_HEAD_DIM)
    return ([m_cos.T, m_sin.T, s_cos.T, s_sin.T],
            list(_lane_tables(m_cos, m_sin) + _lane_tables(s_cos, s_sin)))


def _proj(x, seq, wts, tabs, tm):
    n, d = x.shape
    tm = min(tm, seq)
    nt_seq = seq // tm
    row = lambda i: (i, 0)
    col = lambda i: (0, i)
    bcast = lambda g: jnp.broadcast_to(g.reshape(-1, 1), (g.size, tm))
    consts = [wts["g_attn"], wts["w_tm"], wts["w_fm"], wts["g_qa"], wts["w_qbt"], wts["g_kva"], wts["w_kn"],
              wts["w_vt"], bcast(wts["g_qn_mla"]), wts["gk_a"], wts["gk_b"], bcast(wts["g_q_swa"]), wts["g_k_swa"]]
    fm_tabs, tm_tabs = tabs
    in_specs = ([pl.BlockSpec((tm, d), row)] + [_resident(a.shape) for a in consts]
                + [pl.BlockSpec((t.shape[0], tm), lambda i: (0, i % nt_seq)) for t in fm_tabs]
                + [pl.BlockSpec((tm, LANES), lambda i: (i % nt_seq, 0))] * 4)
    n_qs = SWA_HEADS * SWA_HEAD_DIM
    n_ks = SWA_KV_HEADS * SWA_HEAD_DIM
    out_shape = (
        jax.ShapeDtypeStruct((MLA_HEADS, MLA_PAD, n), BF16),
        jax.ShapeDtypeStruct((MLA_HEADS, n, MLA_PAD), BF16),
        jax.ShapeDtypeStruct((MLA_HEADS, MLA_V, n), BF16),
        jax.ShapeDtypeStruct((n_qs, n), BF16),
        jax.ShapeDtypeStruct((n, n_ks), BF16),
        jax.ShapeDtypeStruct((n_ks, n), BF16),
    )
    out_specs = (
        pl.BlockSpec((MLA_HEADS, MLA_PAD, tm), lambda i: (0, 0, i)),
        pl.BlockSpec((MLA_HEADS, tm, MLA_PAD), lambda i: (0, i, 0)),
        pl.BlockSpec((MLA_HEADS, MLA_V, tm), lambda i: (0, 0, i)),
        pl.BlockSpec((n_qs, tm), col),
        pl.BlockSpec((tm, n_ks), row),
        pl.BlockSpec((n_ks, tm), col),
    )
    return pl.pallas_call(
        _proj_kernel, grid=(n // tm,), in_specs=in_specs, out_specs=out_specs,
        out_shape=out_shape, compiler_params=_params("parallel"), name="proj",
    )(x, *consts, *fm_tabs, *tm_tabs)


def _mla_kernel(qt_ref, k_ref, vt_ref, o_ref, *scratch, ck, track_max):
    j = pl.program_id(3)
    tk, tq = k_ref.shape[1], qt_ref.shape[2]
    if track_max:
        m_ref, l_ref, acc_ref = scratch
    else:
        l_ref, acc_ref, p_ref = scratch

    @pl.when(j == 0)
    def _():
        if track_max:
            m_ref[...] = jnp.full_like(m_ref, -jnp.inf)
        l_ref[...] = jnp.zeros_like(l_ref)
        acc_ref[...] = jnp.zeros_like(acc_ref)

    qt = qt_ref[0]
    if track_max:
        s = jnp.dot(k_ref[0], qt, preferred_element_type=F32)
        m_prev = m_ref[...]
        m_new = jnp.maximum(m_prev, jnp.max(s, axis=0, keepdims=True))
        alpha = jnp.exp2(m_prev - m_new)
        p = jnp.exp2(s - m_new)
        l_ref[...] = alpha * l_ref[...] + jnp.sum(p, axis=0, keepdims=True)
        acc_ref[...] = alpha * acc_ref[...] + jnp.dot(vt_ref[0], p.astype(BF16), preferred_element_type=F32)
        m_ref[...] = m_new
    else:
        l_part = l_ref[...]
        for c in range(tk // ck):
            s = jnp.dot(k_ref[0, c * ck:(c + 1) * ck, :], qt, preferred_element_type=F32)
            p = jnp.exp2(s)
            l_part = l_part + jnp.sum(p.reshape(ck // SUBLANES, SUBLANES, tq), axis=0)
            p_ref[c * ck:(c + 1) * ck, :] = p.astype(BF16)
        l_ref[...] = l_part
        acc_ref[...] += jnp.dot(vt_ref[0], p_ref[...], preferred_element_type=F32)

    @pl.when(j == pl.num_programs(3) - 1)
    def _():
        l = jnp.sum(l_ref[...], axis=0, keepdims=True)
        o_ref[...] = (acc_ref[...] / l).T.astype(o_ref.dtype)


def _mla_attention(qt, k, vt, batch, seq, tq, tk, ck, track_max):
    n = batch * seq
    tq, tk = min(tq, seq), min(tk, seq)
    nq, nk = seq // tq, seq // tk
    if track_max:
        scratch = [pltpu.VMEM((1, tq), F32), pltpu.VMEM((1, tq), F32), pltpu.VMEM((MLA_V, tq), F32)]
    else:
        scratch = [pltpu.VMEM((SUBLANES, tq), F32), pltpu.VMEM((MLA_V, tq), F32), pltpu.VMEM((tk, tq), BF16)]
    return pl.pallas_call(
        functools.partial(_mla_kernel, ck=min(ck, tk), track_max=track_max),
        grid=(batch, MLA_HEADS, nq, nk),
        in_specs=[
            pl.BlockSpec((1, MLA_PAD, tq), lambda b, h, i, j: (h, 0, b * nq + i)),
            pl.BlockSpec((1, tk, MLA_PAD), lambda b, h, i, j: (h, b * nk + j, 0)),
            pl.BlockSpec((1, MLA_V, tk), lambda b, h, i, j: (h, 0, b * nk + j)),
        ],
        out_specs=pl.BlockSpec((tq, MLA_V), lambda b, h, i, j: (b * nq + i, h)),
        out_shape=jax.ShapeDtypeStruct((n, MLA_HEADS * MLA_V), BF16),
        scratch_shapes=scratch,
        compiler_params=_params("parallel", "parallel", "parallel", "arbitrary"),
        name="mla_attn_online" if track_max else "mla_attn",
    )(qt, k, vt)


def _swa_kernel(sink_ref, qt_ref, kp_ref, km_ref, kn_ref, vp_ref, vm_ref, vn_ref, o_ref, kwin_ref, vwin_ref,
                *, nq, sub):
    g = pl.program_id(1)
    i = pl.program_id(2)
    tq = qt_ref.shape[1]
    kwin_ref[:WINDOW] = kp_ref[...]
    kwin_ref[WINDOW:WINDOW + tq] = km_ref[...]
    kwin_ref[WINDOW + tq:] = kn_ref[...]
    vwin_ref[:, :WINDOW] = vp_ref[...]
    vwin_ref[:, WINDOW:WINDOW + tq] = vm_ref[...]
    vwin_ref[:, WINDOW + tq:] = vn_ref[...]
    heads = range(SWA_GROUP)
    ws = sub + 2 * WINDOW
    sink = jnp.concatenate(
        [jnp.full((1, sub), sink_ref[g * SWA_GROUP + hd] * LOG2E, F32) for hd in heads], axis=1)
    key = lax.broadcasted_iota(jnp.int32, (ws, SWA_GROUP * sub), 0)
    qry = lax.broadcasted_iota(jnp.int32, (ws, SWA_GROUP * sub), 1) & (sub - 1)
    band = (key >= qry) & (key <= qry + 2 * WINDOW)
    no_prev = jnp.where(i > 0, 0.0, -jnp.inf)
    no_next = jnp.where(i < nq - 1, 0.0, -jnp.inf)
    for t in range(tq // sub):
        q_all = jnp.concatenate(
            [qt_ref[hd * SWA_HEAD_DIM:(hd + 1) * SWA_HEAD_DIM, t * sub:(t + 1) * sub] for hd in heads], axis=1)
        s = jnp.dot(kwin_ref[t * sub:t * sub + ws, :], q_all, preferred_element_type=F32)
        s = jnp.where(band, s, -jnp.inf)
        if t == 0:
            s = jnp.concatenate([s[:WINDOW] + no_prev, s[WINDOW:]], axis=0)
        if t == tq // sub - 1:
            s = jnp.concatenate([s[:ws - WINDOW], s[ws - WINDOW:] + no_next], axis=0)
        m = jnp.maximum(jnp.max(s, axis=0, keepdims=True), sink)
        p = jnp.exp2(s - m)
        denom = jnp.sum(p, axis=0, keepdims=True) + jnp.exp2(sink - m)
        o_t = jnp.dot(vwin_ref[:, t * sub:t * sub + ws], p.astype(BF16), preferred_element_type=F32) / denom
        for hd in heads:
            o_ref[t * sub:(t + 1) * sub, hd * SWA_HEAD_DIM:(hd + 1) * SWA_HEAD_DIM] = (
                o_t[:, hd * sub:(hd + 1) * sub].T.astype(o_ref.dtype))


def _swa_attention(sink, qt, k, vt, batch, seq, tq, sub):
    n = batch * seq
    tq = min(tq, seq)
    sub = min(sub, tq)
    assert sub & (sub - 1) == 0 and tq % sub == 0, "query index within a sub-tile is taken with a bit mask"
    nq = seq // tq
    r = tq // WINDOW
    nb = seq // WINDOW
    gw = SWA_GROUP * SWA_HEAD_DIM
    prev = lambda b, i: b * nb + jnp.maximum(i * r - 1, 0)
    nxt = lambda b, i: b * nb + jnp.minimum((i + 1) * r, nb - 1)
    edge = (WINDOW, SWA_HEAD_DIM)
    return pl.pallas_call(
        functools.partial(_swa_kernel, nq=nq, sub=sub),
        grid=(batch, SWA_KV_HEADS, nq),
        in_specs=[
            pl.BlockSpec(memory_space=pltpu.SMEM),
            pl.BlockSpec((gw, tq), lambda b, g, i: (g, b * nq + i)),
            pl.BlockSpec(edge, lambda b, g, i: (prev(b, i), g)),
            pl.BlockSpec((tq, SWA_HEAD_DIM), lambda b, g, i: (b * nq + i, g)),
            pl.BlockSpec(edge, lambda b, g, i: (nxt(b, i), g)),
            pl.BlockSpec(edge, lambda b, g, i: (g, prev(b, i))),
            pl.BlockSpec((SWA_HEAD_DIM, tq), lambda b, g, i: (g, b * nq + i)),
            pl.BlockSpec(edge, lambda b, g, i: (g, nxt(b, i))),
        ],
        out_specs=pl.BlockSpec((tq, gw), lambda b, g, i: (b * nq + i, g)),
        out_shape=jax.ShapeDtypeStruct((n, SWA_HEADS * SWA_HEAD_DIM), BF16),
        scratch_shapes=[pltpu.VMEM((tq + 2 * WINDOW, SWA_HEAD_DIM), BF16),
                        pltpu.VMEM((SWA_HEAD_DIM, tq + 2 * WINDOW), BF16)],
        compiler_params=_params("parallel", "parallel", "parallel"),
        name="swa_attn",
    )(sink, qt, k, k, k, vt, vt, vt)


def _out_proj_kernel(x_ref, oa_ref, ob_ref, ga_ref, gb_ref, wa_ref, wb_ref, g_mlp_ref, x1_ref, hm_ref):
    oa = _rms(oa_ref[...].astype(F32), ga_ref[...]).astype(BF16)
    ob = _rms(ob_ref[...].astype(F32), gb_ref[...]).astype(BF16)
    x1 = (x_ref[...] + jnp.dot(oa, wa_ref[...], preferred_element_type=F32)
          + jnp.dot(ob, wb_ref[...], preferred_element_type=F32))
    x1_ref[...] = x1
    hm_ref[...] = _rms(x1, g_mlp_ref[...]).astype(BF16)


def _out_proj(x, oa, ob, wts, tm):
    n, d = x.shape
    tm = min(tm, n)
    row = lambda i: (i, 0)
    consts = [wts[k] for k in ("g_out_mla", "g_out_swa", "w_out_a", "w_out_b", "g_mlp")]
    return pl.pallas_call(
        _out_proj_kernel, grid=(n // tm,),
        in_specs=[pl.BlockSpec((tm, d), row), pl.BlockSpec((tm, oa.shape[1]), row),
                  pl.BlockSpec((tm, ob.shape[1]), row)] + [_resident(a.shape) for a in consts],
        out_specs=(pl.BlockSpec((tm, d), row), pl.BlockSpec((tm, d), row)),
        out_shape=(jax.ShapeDtypeStruct((n, d), F32), jax.ShapeDtypeStruct((n, d), BF16)),
        compiler_params=_params("parallel"), name="out_proj",
    )(x, oa, ob, *consts)


def _mlp_kernel(x1_ref, hm_ref, wu_ref, wd_ref, o_ref):
    @pl.when(pl.program_id(1) == 0)
    def _():
        o_ref[...] = x1_ref[...]

    up = jnp.dot(hm_ref[...], wu_ref[...], preferred_element_type=F32)
    act = jnp.square(jnp.maximum(up, 0.0)).astype(BF16)
    o_ref[...] += jnp.dot(act, wd_ref[...], preferred_element_type=F32)


def _mlp(x1, hm, w_up, w_down, tm, tf):
    n, d = x1.shape
    f = w_up.shape[1]
    tm, tf = min(tm, n), min(tf, f)
    row = lambda i, j: (i, 0)
    return pl.pallas_call(
        _mlp_kernel, grid=(n // tm, f // tf),
        in_specs=[pl.BlockSpec((tm, d), row), pl.BlockSpec((tm, d), row),
                  pl.BlockSpec((d, tf), lambda i, j: (0, j)), pl.BlockSpec((tf, d), lambda i, j: (j, 0))],
        out_specs=pl.BlockSpec((tm, d), row),
        out_shape=jax.ShapeDtypeStruct((n, d), F32),
        compiler_params=_params("parallel", "arbitrary"), name="mlp",
    )(x1, hm, w_up, w_down)


def _ple_kernel(x_ref, p_ref, g_ref, wg_ref, wp_ref, o_ref):
    x = x_ref[...]
    z = jnp.dot(_rms(x, g_ref[...]).astype(BF16), wg_ref[...], preferred_element_type=F32)
    gate = 1.0 / (1.0 + jnp.exp(-z))
    e = jnp.dot(p_ref[...].astype(BF16), wp_ref[...], preferred_element_type=F32)
    o_ref[...] = x + e * gate


def _ple(x2, p, wts, tm):
    n, d = x2.shape
    tm = min(tm, n)
    row = lambda i: (i, 0)
    consts = [wts[k] for k in ("g_ple", "w_gate", "w_ple")]
    return pl.pallas_call(
        _ple_kernel, grid=(n // tm,),
        in_specs=[pl.BlockSpec((tm, d), row), pl.BlockSpec((tm, p.shape[1]), row)]
                 + [_resident(a.shape) for a in consts],
        out_specs=pl.BlockSpec((tm, d), row),
        out_shape=jax.ShapeDtypeStruct((n, d), F32),
        compiler_params=_params("parallel"), name="ple",
    )(x2, p, *consts)


def _swap_halves(a):
    half = a.shape[-1] // 2
    return jnp.concatenate([a[..., half:], a[..., :half]], axis=-1)


def _rope_tables(seq, dim):
    inv = 1.0 / (ROPE_THETA ** (jnp.arange(0, dim, 2, dtype=F32) / dim))
    ang = jnp.arange(seq, dtype=F32)[:, None] * inv[None, :]
    return jnp.cos(ang), jnp.sin(ang)


def _lane_tables(cos, sin):
    pad = jnp.zeros((cos.shape[0], LANES - 2 * cos.shape[1]), F32)
    return (jnp.concatenate([cos, cos, pad], axis=-1), jnp.concatenate([-sin, sin, pad], axis=-1))


def _prepare_weights(g_attn, w_in, g_qa, w_qb, g_kva, w_kvb, g_qn_mla, g_kn_mla, g_q_swa, g_k_swa,
                     g_out_mla, g_out_swa, w_out, g_mlp, w_up, w_down, g_ple, w_ple, w_gate):
    row = lambda g: g.reshape(1, -1)
    o_kpe = MLA_Q_LORA + MLA_KV_LORA
    o_qs = o_kpe + MLA_ROPE
    o_ks = o_qs + SWA_HEADS * SWA_HEAD_DIM
    o_vs = o_ks + SWA_KV_HEADS * SWA_HEAD_DIM
    kpe = w_in[:, o_kpe:o_qs]
    w_tm = jnp.concatenate([w_in[:, :o_kpe], kpe, _swap_halves(kpe), w_in[:, o_ks:o_vs]], axis=1)
    w_fm = jnp.concatenate([w_in[:, o_qs:o_ks], w_in[:, o_vs:]], axis=1).T
    w_kv = w_kvb.reshape(MLA_KV_LORA, MLA_HEADS, MLA_NOPE + MLA_V)
    w_kn = w_kv[..., :MLA_NOPE].reshape(MLA_KV_LORA, MLA_HEADS * MLA_NOPE)
    w_vt = w_kv[..., MLA_NOPE:].reshape(MLA_KV_LORA, MLA_HEADS * MLA_V).T
    g_rope = g_kn_mla[MLA_NOPE:]
    n_a = MLA_HEADS * MLA_V
    logit_span = 2.0 * MLA_QK ** 0.5 * LOG2E * jnp.max(jnp.abs(g_qn_mla)) * jnp.max(jnp.abs(g_kn_mla))
    return {
        "mla_logit_span": logit_span,
        "g_attn": row(g_attn), "w_tm": w_tm.astype(BF16), "w_fm": w_fm.astype(BF16),
        "g_qa": row(g_qa), "w_qbt": w_qb.T.astype(BF16),
        "g_kva": row(g_kva), "w_kn": w_kn.astype(BF16), "w_vt": w_vt.astype(BF16),
        "g_qn_mla": g_qn_mla, "gk_a": row(g_kn_mla[:MLA_NOPE]),
        "gk_b": row(jnp.concatenate([g_rope, _swap_halves(g_rope)])),
        "g_q_swa": g_q_swa, "g_k_swa": row(g_k_swa),
        "g_out_mla": row(g_out_mla), "g_out_swa": row(g_out_swa),
        "w_out_a": w_out[:n_a].astype(BF16), "w_out_b": w_out[n_a:].astype(BF16),
        "g_mlp": row(g_mlp), "w_up": w_up.astype(BF16), "w_down": w_down.astype(BF16),
        "g_ple": row(g_ple), "w_gate": w_gate.astype(BF16), "w_ple": w_ple.astype(BF16),
    }


def _layer(x, p, sink, wts, tabs):
    batch, seq, d = x.shape
    n = batch * seq
    xf = x.reshape(n, d)
    qt, k, vt, qst, ks, vst = _proj(xf, seq, wts, tabs, tm=256)
    o_a = lax.cond(
        wts["mla_logit_span"] <= SAFE_LOG2_RANGE,
        functools.partial(_mla_attention, batch=batch, seq=seq, tq=1024, tk=4096, ck=256, track_max=False),
        functools.partial(_mla_attention, batch=batch, seq=seq, tq=512, tk=512, ck=512, track_max=True),
        qt, k, vt)
    o_b = _swa_attention(sink, qst, ks, vst, batch, seq, tq=512, sub=256)
    x1, hm = _out_proj(xf, o_a, o_b, wts, tm=512)
    x2 = _mlp(x1, hm, wts["w_up"], wts["w_down"], tm=1024, tf=512)
    y = _ple(x2, p.reshape(n, -1), wts, tm=512)
    return y.reshape(batch, seq, d)


def kernel(x_prompt, x_sample, p_prompt, p_sample, g_attn, w_in, g_qa, w_qb, g_kva, w_kvb, g_qn_mla, g_kn_mla,
           g_q_swa, g_k_swa, sink, g_out_mla, g_out_swa, w_out, g_mlp, w_up, w_down, g_ple, w_ple, w_gate):
    y_prompt, y_sample = x_prompt, x_sample
    tabs = _rope_inputs(max(x_prompt.shape[1], x_sample.shape[1]))
    for l in range(g_attn.shape[0]):
        wts = _prepare_weights(g_attn[l], w_in[l], g_qa[l], w_qb[l], g_kva[l], w_kvb[l], g_qn_mla[l],
                               g_kn_mla[l], g_q_swa[l], g_k_swa[l], g_out_mla[l], g_out_swa[l], w_out[l],
                               g_mlp[l], w_up[l], w_down[l], g_ple[l], w_ple[l], w_gate[l])
        y_prompt = _layer(y_prompt, p_prompt[l], sink[l], wts, tabs)
        y_sample = _layer(y_sample, p_sample[l], sink[l], wts, tabs)
    return (y_prompt, y_sample)
```

```python
import functools
import math

import jax
import jax.numpy as jnp
from jax import lax
from jax.experimental import pallas as pl
from jax.experimental.pallas import tpu as pltpu

EPS = 1e-6
ROPE_THETA = 10000.0
WINDOW = 128
MLA_HEADS = 8
MLA_Q_LORA = 512
MLA_KV_LORA = 256
MLA_NOPE = 128
MLA_ROPE = 64
MLA_V = 128
MLA_QK = MLA_NOPE + MLA_ROPE
SWA_HEADS = 8
SWA_KV_HEADS = 2
SWA_GROUP = SWA_HEADS // SWA_KV_HEADS
SWA_HEAD_DIM = 128
LANES = 128
SUBLANES = 8
MLA_PAD = 2 * LANES
SAFE_LOG2_RANGE = 100.0
LOG2E = math.log2(math.e)
MLP_TF = 512
VMEM_LIMIT = 56 * 1024 * 1024

BF16 = jnp.bfloat16
F32 = jnp.float32
NT = (((1,), (1,)), ((), ()))


def _params(*sem):
    return pltpu.CompilerParams(dimension_semantics=sem, vmem_limit_bytes=VMEM_LIMIT)


def _resident(shape):
    nd = len(shape)
    return pl.BlockSpec(shape, lambda *_: (0,) * nd, pipeline_mode=pl.Buffered(1))


def _rms(x, g):
    return x * lax.rsqrt(jnp.mean(x * x, axis=-1, keepdims=True) + EPS) * g


def _rope_lanes(x, t_cos, t_sin):
    return x * t_cos + pltpu.roll(x, LANES // 2, axis=1) * t_sin


def _rope_rows(x1, x2, cos, sin):
    return x1 * cos - x2 * sin, x2 * cos + x1 * sin


def _proj_kernel(x_ref, g_attn_ref, w_tm_ref, w_fm_ref, g_qa_ref, w_qbt_ref, g_kva_ref, w_kn_ref, w_vt_ref,
                 gq_ref, gk_a_ref, gk_b_ref, gqs_ref, gks_ref,
                 mc_ref, ms_ref, sc_ref, ss_ref, tkc_ref, tks_ref, tsc_ref, tss_ref,
                 qt_ref, k_ref, vt_ref, qst_ref, ks_ref, vst_ref):
    tm = x_ref.shape[0]
    h = _rms(x_ref[...], g_attn_ref[...]).astype(BF16)
    p_tm = jnp.dot(h, w_tm_ref[...], preferred_element_type=F32)
    o_ckv = MLA_Q_LORA
    o_kpe = o_ckv + MLA_KV_LORA
    o_ks = o_kpe + LANES

    k_bound = MLA_QK ** 0.5 * jnp.maximum(jnp.max(jnp.abs(gk_a_ref[...]), axis=-1, keepdims=True),
                                          jnp.max(jnp.abs(gk_b_ref[...]), axis=-1, keepdims=True))
    shift_lane = (lax.broadcasted_iota(jnp.int32, (1, LANES), 1) == MLA_QK - LANES).astype(F32)
    first_row = lax.broadcasted_iota(jnp.int32, (MLA_PAD - MLA_QK, tm), 0) == 0

    cqn = _rms(p_tm[:, :o_ckv], g_qa_ref[...]).astype(BF16)
    q_t = lax.dot_general(w_qbt_ref[...], cqn, NT, preferred_element_type=F32)
    q_scale = MLA_QK ** -0.5 * LOG2E
    half = MLA_ROPE // 2
    for hd in range(MLA_HEADS):
        blk = q_t[hd * MLA_QK:(hd + 1) * MLA_QK]
        r = lax.rsqrt(jnp.sum(blk * blk, axis=0, keepdims=True) * (1.0 / MLA_QK) + EPS) * q_scale
        y = blk * gq_ref[...] * r
        shift = -(jnp.sqrt(jnp.sum(y * y, axis=0, keepdims=True)) * k_bound)
        o1, o2 = _rope_rows(y[MLA_NOPE:MLA_NOPE + half], y[MLA_NOPE + half:], mc_ref[...], ms_ref[...])
        qt_ref[hd, :MLA_NOPE, :] = y[:MLA_NOPE].astype(BF16)
        qt_ref[hd, MLA_NOPE:MLA_NOPE + half, :] = o1.astype(BF16)
        qt_ref[hd, MLA_NOPE + half:MLA_QK, :] = o2.astype(BF16)
        qt_ref[hd, MLA_QK:, :] = jnp.where(first_row, shift, 0.0).astype(BF16)

    ckvn = _rms(p_tm[:, o_ckv:o_kpe], g_kva_ref[...]).astype(BF16)
    kn_all = jnp.dot(ckvn, w_kn_ref[...], preferred_element_type=F32)
    kb = p_tm[:, o_kpe:o_ks]
    s_pe = 0.5 * jnp.sum(kb * kb, axis=-1, keepdims=True)
    kr = _rope_lanes(kb * gk_b_ref[...], tkc_ref[...], tks_ref[...])
    for hd in range(MLA_HEADS):
        kn = kn_all[:, hd * MLA_NOPE:(hd + 1) * MLA_NOPE]
        r = lax.rsqrt((jnp.sum(kn * kn, axis=-1, keepdims=True) + s_pe) * (1.0 / MLA_QK) + EPS)
        k_ref[hd, :, :LANES] = (kn * gk_a_ref[...] * r).astype(BF16)
        k_ref[hd, :, LANES:] = (kr * r + shift_lane).astype(BF16)
    v_t = lax.dot_general(w_vt_ref[...], ckvn, NT, preferred_element_type=F32)
    for hd in range(MLA_HEADS):
        vt_ref[hd] = v_t[hd * MLA_V:(hd + 1) * MLA_V].astype(BF16)

    p_fm = lax.dot_general(w_fm_ref[...], h, NT, preferred_element_type=F32)
    s_scale = SWA_HEAD_DIM ** -0.5 * LOG2E
    sh = SWA_HEAD_DIM // 2
    for hd in range(SWA_HEADS):
        blk = p_fm[hd * SWA_HEAD_DIM:(hd + 1) * SWA_HEAD_DIM]
        r = lax.rsqrt(jnp.mean(blk * blk, axis=0, keepdims=True) + EPS) * s_scale
        y = blk * gqs_ref[...] * r
        o1, o2 = _rope_rows(y[:sh], y[sh:], sc_ref[...], ss_ref[...])
        qst_ref[hd * SWA_HEAD_DIM:hd * SWA_HEAD_DIM + sh, :] = o1.astype(BF16)
        qst_ref[hd * SWA_HEAD_DIM + sh:(hd + 1) * SWA_HEAD_DIM, :] = o2.astype(BF16)
    vst_ref[...] = p_fm[SWA_HEADS * SWA_HEAD_DIM:].astype(BF16)
    for hd in range(SWA_KV_HEADS):
        xs = p_tm[:, o_ks + hd * SWA_HEAD_DIM: o_ks + (hd + 1) * SWA_HEAD_DIM]
        kn = _rope_lanes(_rms(xs, gks_ref[...]), tsc_ref[...], tss_ref[...])
        ks_ref[:, hd * SWA_HEAD_DIM:(hd + 1) * SWA_HEAD_DIM] = kn.astype(BF16)


def _rope_inputs(seq):
    m_cos, m_sin = _rope_tables(seq, MLA_ROPE)
    s_cos, s_sin = _rope_tables(seq, SWA_HEAD_DIM)
    return ([m_cos.T, m_sin.T, s_cos.T, s_sin.T],
            list(_lane_tables(m_cos, m_sin) + _lane_tables(s_cos, s_sin)))


def _proj(x, seq, wts, tabs, tm):
    n, d = x.shape
    tm = min(tm, seq)
    nt_seq = seq // tm
    row = lambda i: (i, 0)
    col = lambda i: (0, i)
    bcast = lambda g: jnp.broadcast_to(g.reshape(-1, 1), (g.size, tm))
    consts = [wts["g_attn"], wts["w_tm"], wts["w_fm"], wts["g_qa"], wts["w_qbt"], wts["g_kva"], wts["w_kn"],
              wts["w_vt"], bcast(wts["g_qn_mla"]), wts["gk_a"], wts["gk_b"], bcast(wts["g_q_swa"]), wts["g_k_swa"]]
    fm_tabs, tm_tabs = tabs
    in_specs = ([pl.BlockSpec((tm, d), row)] + [_resident(a.shape) for a in consts]
                + [pl.BlockSpec((t.shape[0], tm), lambda i: (0, i % nt_seq)) for t in fm_tabs]
                + [pl.BlockSpec((tm, LANES), lambda i: (i % nt_seq, 0))] * 4)
    n_qs = SWA_HEADS * SWA_HEAD_DIM
    n_ks = SWA_KV_HEADS * SWA_HEAD_DIM
    out_shape = (
        jax.ShapeDtypeStruct((MLA_HEADS, MLA_PAD, n), BF16),
        jax.ShapeDtypeStruct((MLA_HEADS, n, MLA_PAD), BF16),
        jax.ShapeDtypeStruct((MLA_HEADS, MLA_V, n), BF16),
        jax.ShapeDtypeStruct((n_qs, n), BF16),
        jax.ShapeDtypeStruct((n, n_ks), BF16),
        jax.ShapeDtypeStruct((n_ks, n), BF16),
    )
    out_specs = (
        pl.BlockSpec((MLA_HEADS, MLA_PAD, tm), lambda i: (0, 0, i)),
        pl.BlockSpec((MLA_HEADS, tm, MLA_PAD), lambda i: (0, i, 0)),
        pl.BlockSpec((MLA_HEADS, MLA_V, tm), lambda i: (0, 0, i)),
        pl.BlockSpec((n_qs, tm), col),
        pl.BlockSpec((tm, n_ks), row),
        pl.BlockSpec((n_ks, tm), col),
    )
    return pl.pallas_call(
        _proj_kernel, grid=(n // tm,), in_specs=in_specs, out_specs=out_specs,
        out_shape=out_shape, compiler_params=_params("parallel"), name="proj",
    )(x, *consts, *fm_tabs, *tm_tabs)


def _mla_kernel(qt_ref, k_ref, vt_ref, o_ref, *scratch, ck, track_max):
    j = pl.program_id(3)
    tk, tq = k_ref.shape[1], qt_ref.shape[2]
    if track_max:
        m_ref, l_ref, acc_ref = scratch
    else:
        l_ref, acc_ref, p_ref = scratch

    @pl.when(j == 0)
    def _():
        if track_max:
            m_ref[...] = jnp.full_like(m_ref, -jnp.inf)
        l_ref[...] = jnp.zeros_like(l_ref)
        acc_ref[...] = jnp.zeros_like(acc_ref)

    qt = qt_ref[0]
    if track_max:
        s = jnp.dot(k_ref[0], qt, preferred_element_type=F32)
        m_prev = m_ref[...]
        m_new = jnp.maximum(m_prev, jnp.max(s, axis=0, keepdims=True))
        alpha = jnp.exp2(m_prev - m_new)
        p = jnp.exp2(s - m_new)
        l_ref[...] = alpha * l_ref[...] + jnp.sum(p, axis=0, keepdims=True)
        acc_ref[...] = alpha * acc_ref[...] + jnp.dot(vt_ref[0], p.astype(BF16), preferred_element_type=F32)
        m_ref[...] = m_new
    else:
        l_part = l_ref[...]
        for c in range(tk // ck):
            s = jnp.dot(k_ref[0, c * ck:(c + 1) * ck, :], qt, preferred_element_type=F32)
            p = jnp.exp2(s)
            l_part = l_part + jnp.sum(p.reshape(ck // SUBLANES, SUBLANES, tq), axis=0)
            p_ref[c * ck:(c + 1) * ck, :] = p.astype(BF16)
        l_ref[...] = l_part
        acc_ref[...] += jnp.dot(vt_ref[0], p_ref[...], preferred_element_type=F32)

    @pl.when(j == pl.num_programs(3) - 1)
    def _():
        l = jnp.sum(l_ref[...], axis=0, keepdims=True)
        o_ref[...] = (acc_ref[...] / l).T.astype(o_ref.dtype)


def _mla_attention(qt, k, vt, batch, seq, tq, tk, ck, track_max):
    n = batch * seq
    tq, tk = min(tq, seq), min(tk, seq)
    nq, nk = seq // tq, seq // tk
    if track_max:
        scratch = [pltpu.VMEM((1, tq), F32), pltpu.VMEM((1, tq), F32), pltpu.VMEM((MLA_V, tq), F32)]
    else:
        scratch = [pltpu.VMEM((SUBLANES, tq), F32), pltpu.VMEM((MLA_V, tq), F32), pltpu.VMEM((tk, tq), BF16)]
    return pl.pallas_call(
        functools.partial(_mla_kernel, ck=min(ck, tk), track_max=track_max),
        grid=(batch, MLA_HEADS, nq, nk),
        in_specs=[
            pl.BlockSpec((1, MLA_PAD, tq), lambda b, h, i, j: (h, 0, b * nq + i)),
            pl.BlockSpec((1, tk, MLA_PAD), lambda b, h, i, j: (h, b * nk + j, 0)),
            pl.BlockSpec((1, MLA_V, tk), lambda b, h, i, j: (h, 0, b * nk + j)),
        ],
        out_specs=pl.BlockSpec((tq, MLA_V), lambda b, h, i, j: (b * nq + i, h)),
        out_shape=jax.ShapeDtypeStruct((n, MLA_HEADS * MLA_V), BF16),
        scratch_shapes=scratch,
        compiler_params=_params("parallel", "parallel", "parallel", "arbitrary"),
        name="mla_attn_online" if track_max else "mla_attn",
    )(qt, k, vt)


def _swa_kernel(sink_ref, qt_ref, kp_ref, km_ref, kn_ref, vp_ref, vm_ref, vn_ref, o_ref, kwin_ref, vwin_ref,
                *, nq, sub):
    g = pl.program_id(1)
    i = pl.program_id(2)
    tq = qt_ref.shape[1]
    kwin_ref[:WINDOW] = kp_ref[...]
    kwin_ref[WINDOW:WINDOW + tq] = km_ref[...]
    kwin_ref[WINDOW + tq:] = kn_ref[...]
    vwin_ref[:, :WINDOW] = vp_ref[...]
    vwin_ref[:, WINDOW:WINDOW + tq] = vm_ref[...]
    vwin_ref[:, WINDOW + tq:] = vn_ref[...]
    heads = range(SWA_GROUP)
    ws = sub + 2 * WINDOW
    sink = jnp.concatenate(
        [jnp.full((1, sub), sink_ref[g * SWA_GROUP + hd] * LOG2E, F32) for hd in heads], axis=1)
    key = lax.broadcasted_iota(jnp.int32, (ws, SWA_GROUP * sub), 0)
    qry = lax.broadcasted_iota(jnp.int32, (ws, SWA_GROUP * sub), 1) & (sub - 1)
    band = (key >= qry) & (key <= qry + 2 * WINDOW)
    no_prev = jnp.where(i > 0, 0.0, -jnp.inf)
    no_next = jnp.where(i < nq - 1, 0.0, -jnp.inf)
    for t in range(tq // sub):
        q_all = jnp.concatenate(
            [qt_ref[hd * SWA_HEAD_DIM:(hd + 1) * SWA_HEAD_DIM, t * sub:(t + 1) * sub] for hd in heads], axis=1)
        s = jnp.dot(kwin_ref[t * sub:t * sub + ws, :], q_all, preferred_element_type=F32)
        s = jnp.where(band, s, -jnp.inf)
        if t == 0:
            s = jnp.concatenate([s[:WINDOW] + no_prev, s[WINDOW:]], axis=0)
        if t == tq // sub - 1:
            s = jnp.concatenate([s[:ws - WINDOW], s[ws - WINDOW:] + no_next], axis=0)
        m = jnp.maximum(jnp.max(s, axis=0, keepdims=True), sink)
        p = jnp.exp2(s - m)
        denom = jnp.sum(p, axis=0, keepdims=True) + jnp.exp2(sink - m)
        o_t = jnp.dot(vwin_ref[:, t * sub:t * sub + ws], p.astype(BF16), preferred_element_type=F32) / denom
        for hd in heads:
            o_ref[t * sub:(t + 1) * sub, hd * SWA_HEAD_DIM:(hd + 1) * SWA_HEAD_DIM] = (
                o_t[:, hd * sub:(hd + 1) * sub].T.astype(o_ref.dtype))


def _swa_attention(sink, qt, k, vt, batch, seq, tq, sub):
    n = batch * seq
    tq = min(tq, seq)
    sub = min(sub, tq)
    assert sub & (sub - 1) == 0 and tq % sub == 0, "query index within a sub-tile is taken with a bit mask"
    nq = seq // tq
    r = tq // WINDOW
    nb = seq // WINDOW
    gw = SWA_GROUP * SWA_HEAD_DIM
    prev = lambda b, i: b * nb + jnp.maximum(i * r - 1, 0)
    nxt = lambda b, i: b * nb + jnp.minimum((i + 1) * r, nb - 1)
    edge = (WINDOW, SWA_HEAD_DIM)
    return pl.pallas_call(
        functools.partial(_swa_kernel, nq=nq, sub=sub),
        grid=(batch, SWA_KV_HEADS, nq),
        in_specs=[
            pl.BlockSpec(memory_space=pltpu.SMEM),
            pl.BlockSpec((gw, tq), lambda b, g, i: (g, b * nq + i)),
            pl.BlockSpec(edge, lambda b, g, i: (prev(b, i), g)),
            pl.BlockSpec((tq, SWA_HEAD_DIM), lambda b, g, i: (b * nq + i, g)),
            pl.BlockSpec(edge, lambda b, g, i: (nxt(b, i), g)),
            pl.BlockSpec(edge, lambda b, g, i: (g, prev(b, i))),
            pl.BlockSpec((SWA_HEAD_DIM, tq), lambda b, g, i: (g, b * nq + i)),
            pl.BlockSpec(edge, lambda b, g, i: (g, nxt(b, i))),
        ],
        out_specs=pl.BlockSpec((tq, gw), lambda b, g, i: (b * nq + i, g)),
        out_shape=jax.ShapeDtypeStruct((n, SWA_HEADS * SWA_HEAD_DIM), BF16),
        scratch_shapes=[pltpu.VMEM((tq + 2 * WINDOW, SWA_HEAD_DIM), BF16),
                        pltpu.VMEM((SWA_HEAD_DIM, tq + 2 * WINDOW), BF16)],
        compiler_params=_params("parallel", "parallel", "parallel"),
        name="swa_attn",
    )(sink, qt, k, k, k, vt, vt, vt)


def _out_proj_kernel(x_ref, oa_ref, ob_ref, ga_ref, gb_ref, wa_ref, wb_ref, g_mlp_ref, x1_ref, hm_ref):
    oa = _rms(oa_ref[...].astype(F32), ga_ref[...]).astype(BF16)
    ob = _rms(ob_ref[...].astype(F32), gb_ref[...]).astype(BF16)
    x1 = (x_ref[...] + jnp.dot(oa, wa_ref[...], preferred_element_type=F32)
          + jnp.dot(ob, wb_ref[...], preferred_element_type=F32))
    x1_ref[...] = x1
    hm_ref[...] = _rms(x1, g_mlp_ref[...]).astype(BF16)


def _out_proj(x, oa, ob, wts, tm):
    n, d = x.shape
    tm = min(tm, n)
    row = lambda i: (i, 0)
    consts = [wts[k] for k in ("g_out_mla", "g_out_swa", "w_out_a", "w_out_b", "g_mlp")]
    return pl.pallas_call(
        _out_proj_kernel, grid=(n // tm,),
        in_specs=[pl.BlockSpec((tm, d), row), pl.BlockSpec((tm, oa.shape[1]), row),
                  pl.BlockSpec((tm, ob.shape[1]), row)] + [_resident(a.shape) for a in consts],
        out_specs=(pl.BlockSpec((tm, d), row), pl.BlockSpec((tm, d), row)),
        out_shape=(jax.ShapeDtypeStruct((n, d), F32), jax.ShapeDtypeStruct((n, d), BF16)),
        compiler_params=_params("parallel"), name="out_proj",
    )(x, oa, ob, *consts)


def _mlp_kernel(x1_ref, hm_ref, wu_ref, wd_ref, o_ref):
    @pl.when(pl.program_id(1) == 0)
    def _():
        o_ref[...] = x1_ref[...]

    up = jnp.dot(hm_ref[...], wu_ref[0], preferred_element_type=F32)
    act = jnp.square(jnp.maximum(up, 0.0)).astype(BF16)
    o_ref[...] += jnp.dot(act, wd_ref[...], preferred_element_type=F32)


def _mlp(x1, hm, w_up, w_down, tm):
    n, d = x1.shape
    nf, _, tf = w_up.shape
    tm = min(tm, n)
    row = lambda i, j: (i, 0)
    return pl.pallas_call(
        _mlp_kernel, grid=(n // tm, nf),
        in_specs=[pl.BlockSpec((tm, d), row), pl.BlockSpec((tm, d), row),
                  pl.BlockSpec((1, d, tf), lambda i, j: (j, 0, 0)), pl.BlockSpec((tf, d), lambda i, j: (j, 0))],
        out_specs=pl.BlockSpec((tm, d), row),
        out_shape=jax.ShapeDtypeStruct((n, d), F32),
        compiler_params=_params("parallel", "arbitrary"), name="mlp",
    )(x1, hm, w_up, w_down)


def _ple_kernel(x_ref, p_ref, g_ref, wg_ref, wp_ref, o_ref):
    x = x_ref[...]
    z = jnp.dot(_rms(x, g_ref[...]).astype(BF16), wg_ref[...], preferred_element_type=F32)
    gate = 1.0 / (1.0 + jnp.exp(-z))
    e = jnp.dot(p_ref[...].astype(BF16), wp_ref[...], preferred_element_type=F32)
    o_ref[...] = x + e * gate


def _ple(x2, p, wts, tm):
    n, d = x2.shape
    tm = min(tm, n)
    row = lambda i: (i, 0)
    consts = [wts[k] for k in ("g_ple", "w_gate", "w_ple")]
    return pl.pallas_call(
        _ple_kernel, grid=(n // tm,),
        in_specs=[pl.BlockSpec((tm, d), row), pl.BlockSpec((tm, p.shape[1]), row)]
                 + [_resident(a.shape) for a in consts],
        out_specs=pl.BlockSpec((tm, d), row),
        out_shape=jax.ShapeDtypeStruct((n, d), F32),
        compiler_params=_params("parallel"), name="ple",
    )(x2, p, *consts)


def _swap_halves(a):
    half = a.shape[-1] // 2
    return jnp.concatenate([a[..., half:], a[..., :half]], axis=-1)


def _rope_tables(seq, dim):
    inv = 1.0 / (ROPE_THETA ** (jnp.arange(0, dim, 2, dtype=F32) / dim))
    ang = jnp.arange(seq, dtype=F32)[:, None] * inv[None, :]
    return jnp.cos(ang), jnp.sin(ang)


def _lane_tables(cos, sin):
    pad = jnp.zeros((cos.shape[0], LANES - 2 * cos.shape[1]), F32)
    return (jnp.concatenate([cos, cos, pad], axis=-1), jnp.concatenate([-sin, sin, pad], axis=-1))


def _prepare_weights(g_attn, w_in, g_qa, w_qb, g_kva, w_kvb, g_qn_mla, g_kn_mla, g_q_swa, g_k_swa,
                     g_out_mla, g_out_swa, w_out, g_mlp, w_up, w_down, g_ple, w_ple, w_gate):
    row = lambda g: g.reshape(1, -1)
    o_kpe = MLA_Q_LORA + MLA_KV_LORA
    o_qs = o_kpe + MLA_ROPE
    o_ks = o_qs + SWA_HEADS * SWA_HEAD_DIM
    o_vs = o_ks + SWA_KV_HEADS * SWA_HEAD_DIM
    kpe = w_in[:, o_kpe:o_qs]
    w_tm = jnp.concatenate([w_in[:, :o_kpe], kpe, _swap_halves(kpe), w_in[:, o_ks:o_vs]], axis=1)
    w_fm = jnp.concatenate([w_in[:, o_qs:o_ks], w_in[:, o_vs:]], axis=1).T
    w_kv = w_kvb.reshape(MLA_KV_LORA, MLA_HEADS, MLA_NOPE + MLA_V)
    w_kn = w_kv[..., :MLA_NOPE].reshape(MLA_KV_LORA, MLA_HEADS * MLA_NOPE)
    w_vt = w_kv[..., MLA_NOPE:].reshape(MLA_KV_LORA, MLA_HEADS * MLA_V).T
    g_rope = g_kn_mla[MLA_NOPE:]
    n_a = MLA_HEADS * MLA_V
    d_model, d_ff = w_up.shape
    logit_span = 2.0 * MLA_QK ** 0.5 * LOG2E * jnp.max(jnp.abs(g_qn_mla)) * jnp.max(jnp.abs(g_kn_mla))
    return {
        "mla_logit_span": logit_span,
        "g_attn": row(g_attn), "w_tm": w_tm.astype(BF16), "w_fm": w_fm.astype(BF16),
        "g_qa": row(g_qa), "w_qbt": w_qb.T.astype(BF16),
        "g_kva": row(g_kva), "w_kn": w_kn.astype(BF16), "w_vt": w_vt.astype(BF16),
        "g_qn_mla": g_qn_mla, "gk_a": row(g_kn_mla[:MLA_NOPE]),
        "gk_b": row(jnp.concatenate([g_rope, _swap_halves(g_rope)])),
        "g_q_swa": g_q_swa, "g_k_swa": row(g_k_swa),
        "g_out_mla": row(g_out_mla), "g_out_swa": row(g_out_swa),
        "w_out_a": w_out[:n_a].astype(BF16), "w_out_b": w_out[n_a:].astype(BF16),
        "g_mlp": row(g_mlp), "w_down": w_down.astype(BF16),
        "w_up": w_up.astype(BF16).reshape(d_model, -1, min(MLP_TF, d_ff)).transpose(1, 0, 2),
        "g_ple": row(g_ple), "w_gate": w_gate.astype(BF16), "w_ple": w_ple.astype(BF16),
    }


def _layer(x, p, sink, wts, tabs):
    batch, seq, d = x.shape
    n = batch * seq
    xf = x.reshape(n, d)
    qt, k, vt, qst, ks, vst = _proj(xf, seq, wts, tabs, tm=256)
    o_a = lax.cond(
        wts["mla_logit_span"] <= SAFE_LOG2_RANGE,
        functools.partial(_mla_attention, batch=batch, seq=seq, tq=1024, tk=4096, ck=256, track_max=False),
        functools.partial(_mla_attention, batch=batch, seq=seq, tq=512, tk=512, ck=512, track_max=True),
        qt, k, vt)
    o_b = _swa_attention(sink, qst, ks, vst, batch, seq, tq=512, sub=256)
    x1, hm = _out_proj(xf, o_a, o_b, wts, tm=512)
    x2 = _mlp(x1, hm, wts["w_up"], wts["w_down"], tm=1024)
    y = _ple(x2, p.reshape(n, -1), wts, tm=512)
    return y.reshape(batch, seq, d)


def kernel(x_prompt, x_sample, p_prompt, p_sample, g_attn, w_in, g_qa, w_qb, g_kva, w_kvb, g_qn_mla, g_kn_mla,
           g_q_swa, g_k_swa, sink, g_out_mla, g_out_swa, w_out, g_mlp, w_up, w_down, g_ple, w_ple, w_gate):
    y_prompt, y_sample = x_prompt, x_sample
    tabs = _rope_inputs(max(x_prompt.shape[1], x_sample.shape[1]))
    for l in range(g_attn.shape[0]):
        wts = _prepare_weights(g_attn[l], w_in[l], g_qa[l], w_qb[l], g_kva[l], w_kvb[l], g_qn_mla[l],
                               g_kn_mla[l], g_q_swa[l], g_k_swa[l], g_out_mla[l], g_out_swa[l], w_out[l],
                               g_mlp[l], w_up[l], w_down[l], g_ple[l], w_ple[l], w_gate[l])
        y_prompt = _layer(y_prompt, p_prompt[l], sink[l], wts, tabs)
        y_sample = _layer(y_sample, p_sample[l], sink[l], wts, tabs)
    return (y_prompt, y_sample)
```

```python
import functools
import math

import jax
import jax.numpy as jnp
from jax import lax
from jax.experimental import pallas as pl
from jax.experimental.pallas import tpu as pltpu

EPS = 1e-6
ROPE_THETA = 10000.0
WINDOW = 128
MLA_HEADS = 8
MLA_Q_LORA = 512
MLA_KV_LORA = 256
MLA_NOPE = 128
MLA_ROPE = 64
MLA_V = 128
MLA_QK = MLA_NOPE + MLA_ROPE
SWA_HEADS = 8
SWA_KV_HEADS = 2
SWA_GROUP = SWA_HEADS // SWA_KV_HEADS
SWA_HEAD_DIM = 128
LANES = 128
SUBLANES = 8
MLA_PAD = 2 * LANES
SAFE_LOG2_RANGE = 100.0
LOG2E = math.log2(math.e)
VMEM_LIMIT = 56 * 1024 * 1024

BF16 = jnp.bfloat16
F32 = jnp.float32
NT = (((1,), (1,)), ((), ()))


def _params(*sem):
    return pltpu.CompilerParams(dimension_semantics=sem, vmem_limit_bytes=VMEM_LIMIT)


def _resident(shape):
    nd = len(shape)
    return pl.BlockSpec(shape, lambda *_: (0,) * nd, pipeline_mode=pl.Buffered(1))


def _rms(x, g):
    return x * lax.rsqrt(jnp.mean(x * x, axis=-1, keepdims=True) + EPS) * g


def _rope_lanes(x, t_cos, t_sin):
    return x * t_cos + pltpu.roll(x, LANES // 2, axis=1) * t_sin


def _rope_rows(x1, x2, cos, sin):
    return x1 * cos - x2 * sin, x2 * cos + x1 * sin


def _proj_kernel(x_ref, g_attn_ref, w_tm_ref, w_fm_ref, g_qa_ref, w_qbt_ref, g_kva_ref, w_kn_ref, w_vt_ref,
                 gq_ref, gk_a_ref, gk_b_ref, gqs_ref, gks_ref,
                 mc_ref, ms_ref, sc_ref, ss_ref, tkc_ref, tks_ref, tsc_ref, tss_ref,
                 qt_ref, k_ref, vt_ref, qst_ref, ks_ref, vst_ref):
    tm = x_ref.shape[0]
    h = _rms(x_ref[...], g_attn_ref[...]).astype(BF16)
    p_tm = jnp.dot(h, w_tm_ref[...], preferred_element_type=F32)
    o_ckv = MLA_Q_LORA
    o_kpe = o_ckv + MLA_KV_LORA
    o_ks = o_kpe + LANES

    k_bound = MLA_QK ** 0.5 * jnp.maximum(jnp.max(jnp.abs(gk_a_ref[...]), axis=-1, keepdims=True),
                                          jnp.max(jnp.abs(gk_b_ref[...]), axis=-1, keepdims=True))
    shift_lane = (lax.broadcasted_iota(jnp.int32, (1, LANES), 1) == MLA_QK - LANES).astype(F32)
    first_row = lax.broadcasted_iota(jnp.int32, (MLA_PAD - MLA_QK, tm), 0) == 0

    cqn = _rms(p_tm[:, :o_ckv], g_qa_ref[...]).astype(BF16)
    q_t = lax.dot_general(w_qbt_ref[...], cqn, NT, preferred_element_type=F32)
    q_scale = MLA_QK ** -0.5 * LOG2E
    half = MLA_ROPE // 2
    for hd in range(MLA_HEADS):
        blk = q_t[hd * MLA_QK:(hd + 1) * MLA_QK]
        r = lax.rsqrt(jnp.sum(blk * blk, axis=0, keepdims=True) * (1.0 / MLA_QK) + EPS) * q_scale
        y = blk * gq_ref[...] * r
        shift = -(jnp.sqrt(jnp.sum(y * y, axis=0, keepdims=True)) * k_bound)
        o1, o2 = _rope_rows(y[MLA_NOPE:MLA_NOPE + half], y[MLA_NOPE + half:], mc_ref[...], ms_ref[...])
        qt_ref[hd, :MLA_NOPE, :] = y[:MLA_NOPE].astype(BF16)
        qt_ref[hd, MLA_NOPE:MLA_NOPE + half, :] = o1.astype(BF16)
        qt_ref[hd, MLA_NOPE + half:MLA_QK, :] = o2.astype(BF16)
        qt_ref[hd, MLA_QK:, :] = jnp.where(first_row, shift, 0.0).astype(BF16)

    ckvn = _rms(p_tm[:, o_ckv:o_kpe], g_kva_ref[...]).astype(BF16)
    kn_all = jnp.dot(ckvn, w_kn_ref[...], preferred_element_type=F32)
    kb = p_tm[:, o_kpe:o_ks]
    s_pe = 0.5 * jnp.sum(kb * kb, axis=-1, keepdims=True)
    kr = _rope_lanes(kb * gk_b_ref[...], tkc_ref[...], tks_ref[...])
    for hd in range(MLA_HEADS):
        kn = kn_all[:, hd * MLA_NOPE:(hd + 1) * MLA_NOPE]
        r = lax.rsqrt((jnp.sum(kn * kn, axis=-1, keepdims=True) + s_pe) * (1.0 / MLA_QK) + EPS)
        k_ref[hd, :, :LANES] = (kn * gk_a_ref[...] * r).astype(BF16)
        k_ref[hd, :, LANES:] = (kr * r + shift_lane).astype(BF16)
    v_t = lax.dot_general(w_vt_ref[...], ckvn, NT, preferred_element_type=F32)
    for hd in range(MLA_HEADS):
        vt_ref[hd] = v_t[hd * MLA_V:(hd + 1) * MLA_V].astype(BF16)

    p_fm = lax.dot_general(w_fm_ref[...], h, NT, preferred_element_type=F32)
    s_scale = SWA_HEAD_DIM ** -0.5 * LOG2E
    sh = SWA_HEAD_DIM // 2
    for hd in range(SWA_HEADS):
        blk = p_fm[hd * SWA_HEAD_DIM:(hd + 1) * SWA_HEAD_DIM]
        r = lax.rsqrt(jnp.mean(blk * blk, axis=0, keepdims=True) + EPS) * s_scale
        y = blk * gqs_ref[...] * r
        o1, o2 = _rope_rows(y[:sh], y[sh:], sc_ref[...], ss_ref[...])
        qst_ref[hd * SWA_HEAD_DIM:hd * SWA_HEAD_DIM + sh, :] = o1.astype(BF16)
        qst_ref[hd * SWA_HEAD_DIM + sh:(hd + 1) * SWA_HEAD_DIM, :] = o2.astype(BF16)
    vst_ref[...] = p_fm[SWA_HEADS * SWA_HEAD_DIM:].astype(BF16)
    for hd in range(SWA_KV_HEADS):
        xs = p_tm[:, o_ks + hd * SWA_HEAD_DIM: o_ks + (hd + 1) * SWA_HEAD_DIM]
        kn = _rope_lanes(_rms(xs, gks_ref[...]), tsc_ref[...], tss_ref[...])
        ks_ref[:, hd * SWA_HEAD_DIM:(hd + 1) * SWA_HEAD_DIM] = kn.astype(BF16)


def _rope_inputs(seq):
    assert 2 * MLA_ROPE == SWA_HEAD_DIM
    s_cos, s_sin = _rope_tables(seq, SWA_HEAD_DIM)
    m_cos, m_sin = s_cos[0::2], s_sin[0::2]
    return ([m_cos, m_sin, s_cos, s_sin],
            list(_lane_tables(m_cos.T, m_sin.T) + _lane_tables(s_cos.T, s_sin.T)))


def _proj(x, seq, wts, tabs, tm):
    n, d = x.shape
    tm = min(tm, seq)
    nt_seq = seq // tm
    row = lambda i: (i, 0)
    col = lambda i: (0, i)
    bcast = lambda g: jnp.broadcast_to(g.reshape(-1, 1), (g.size, tm))
    consts = [wts["g_attn"], wts["w_tm"], wts["w_fm"], wts["g_qa"], wts["w_qbt"], wts["g_kva"], wts["w_kn"],
              wts["w_vt"], bcast(wts["g_qn_mla"]), wts["gk_a"], wts["gk_b"], bcast(wts["g_q_swa"]), wts["g_k_swa"]]
    fm_tabs, tm_tabs = tabs
    in_specs = ([pl.BlockSpec((tm, d), row)] + [_resident(a.shape) for a in consts]
                + [pl.BlockSpec((t.shape[0], tm), lambda i: (0, i % nt_seq)) for t in fm_tabs]
                + [pl.BlockSpec((tm, LANES), lambda i: (i % nt_seq, 0))] * 4)
    n_qs = SWA_HEADS * SWA_HEAD_DIM
    n_ks = SWA_KV_HEADS * SWA_HEAD_DIM
    out_shape = (
        jax.ShapeDtypeStruct((MLA_HEADS, MLA_PAD, n), BF16),
        jax.ShapeDtypeStruct((MLA_HEADS, n, MLA_PAD), BF16),
        jax.ShapeDtypeStruct((MLA_HEADS, MLA_V, n), BF16),
        jax.ShapeDtypeStruct((n_qs, n), BF16),
        jax.ShapeDtypeStruct((n, n_ks), BF16),
        jax.ShapeDtypeStruct((n_ks, n), BF16),
    )
    out_specs = (
        pl.BlockSpec((MLA_HEADS, MLA_PAD, tm), lambda i: (0, 0, i)),
        pl.BlockSpec((MLA_HEADS, tm, MLA_PAD), lambda i: (0, i, 0)),
        pl.BlockSpec((MLA_HEADS, MLA_V, tm), lambda i: (0, 0, i)),
        pl.BlockSpec((n_qs, tm), col),
        pl.BlockSpec((tm, n_ks), row),
        pl.BlockSpec((n_ks, tm), col),
    )
    return pl.pallas_call(
        _proj_kernel, grid=(n // tm,), in_specs=in_specs, out_specs=out_specs,
        out_shape=out_shape, compiler_params=_params("parallel"), name="proj",
    )(x, *consts, *fm_tabs, *tm_tabs)


def _mla_kernel(qt_ref, k_ref, vt_ref, o_ref, *scratch, ck, track_max):
    j = pl.program_id(3)
    tk, tq = k_ref.shape[1], qt_ref.shape[2]
    if track_max:
        m_ref, l_ref, acc_ref = scratch
    else:
        l_ref, acc_ref, p_ref = scratch

    @pl.when(j == 0)
    def _():
        if track_max:
            m_ref[...] = jnp.full_like(m_ref, -jnp.inf)
        l_ref[...] = jnp.zeros_like(l_ref)
        acc_ref[...] = jnp.zeros_like(acc_ref)

    qt = qt_ref[0]
    if track_max:
        s = jnp.dot(k_ref[0], qt, preferred_element_type=F32)
        m_prev = m_ref[...]
        m_new = jnp.maximum(m_prev, jnp.max(s, axis=0, keepdims=True))
        alpha = jnp.exp2(m_prev - m_new)
        p = jnp.exp2(s - m_new)
        l_ref[...] = alpha * l_ref[...] + jnp.sum(p, axis=0, keepdims=True)
        acc_ref[...] = alpha * acc_ref[...] + jnp.dot(vt_ref[0], p.astype(BF16), preferred_element_type=F32)
        m_ref[...] = m_new
    else:
        l_part = l_ref[...]
        for c in range(tk // ck):
            s = jnp.dot(k_ref[0, c * ck:(c + 1) * ck, :], qt, preferred_element_type=F32)
            p = jnp.exp2(s)
            l_part = l_part + jnp.sum(p.reshape(ck // SUBLANES, SUBLANES, tq), axis=0)
            p_ref[c * ck:(c + 1) * ck, :] = p.astype(BF16)
        l_ref[...] = l_part
        acc_ref[...] += jnp.dot(vt_ref[0], p_ref[...], preferred_element_type=F32)

    @pl.when(j == pl.num_programs(3) - 1)
    def _():
        l = jnp.sum(l_ref[...], axis=0, keepdims=True)
        o_ref[...] = (acc_ref[...] / l).T.astype(o_ref.dtype)


def _mla_attention(qt, k, vt, batch, seq, tq, tk, ck, track_max):
    n = batch * seq
    tq, tk = min(tq, seq), min(tk, seq)
    nq, nk = seq // tq, seq // tk
    if track_max:
        scratch = [pltpu.VMEM((1, tq), F32), pltpu.VMEM((1, tq), F32), pltpu.VMEM((MLA_V, tq), F32)]
    else:
        scratch = [pltpu.VMEM((SUBLANES, tq), F32), pltpu.VMEM((MLA_V, tq), F32), pltpu.VMEM((tk, tq), BF16)]
    return pl.pallas_call(
        functools.partial(_mla_kernel, ck=min(ck, tk), track_max=track_max),
        grid=(batch, MLA_HEADS, nq, nk),
        in_specs=[
            pl.BlockSpec((1, MLA_PAD, tq), lambda b, h, i, j: (h, 0, b * nq + i)),
            pl.BlockSpec((1, tk, MLA_PAD), lambda b, h, i, j: (h, b * nk + j, 0)),
            pl.BlockSpec((1, MLA_V, tk), lambda b, h, i, j: (h, 0, b * nk + j)),
        ],
        out_specs=pl.BlockSpec((tq, MLA_V), lambda b, h, i, j: (b * nq + i, h)),
        out_shape=jax.ShapeDtypeStruct((n, MLA_HEADS * MLA_V), BF16),
        scratch_shapes=scratch,
        compiler_params=_params("parallel", "parallel", "parallel", "arbitrary"),
        name="mla_attn_online" if track_max else "mla_attn",
    )(qt, k, vt)


def _swa_kernel(sink_ref, qt_ref, kp_ref, km_ref, kn_ref, vp_ref, vm_ref, vn_ref, o_ref, kwin_ref, vwin_ref,
                *, nq, sub):
    g = pl.program_id(1)
    i = pl.program_id(2)
    tq = qt_ref.shape[1]
    kwin_ref[:WINDOW] = kp_ref[...]
    kwin_ref[WINDOW:WINDOW + tq] = km_ref[...]
    kwin_ref[WINDOW + tq:] = kn_ref[...]
    vwin_ref[:, :WINDOW] = vp_ref[...]
    vwin_ref[:, WINDOW:WINDOW + tq] = vm_ref[...]
    vwin_ref[:, WINDOW + tq:] = vn_ref[...]
    heads = range(SWA_GROUP)
    ws = sub + 2 * WINDOW
    sink = jnp.concatenate(
        [jnp.full((1, sub), sink_ref[g * SWA_GROUP + hd] * LOG2E, F32) for hd in heads], axis=1)
    key = lax.broadcasted_iota(jnp.int32, (ws, SWA_GROUP * sub), 0)
    qry = lax.broadcasted_iota(jnp.int32, (ws, SWA_GROUP * sub), 1) & (sub - 1)
    band = (key >= qry) & (key <= qry + 2 * WINDOW)
    no_prev = jnp.where(i > 0, 0.0, -jnp.inf)
    no_next = jnp.where(i < nq - 1, 0.0, -jnp.inf)
    for t in range(tq // sub):
        q_all = jnp.concatenate(
            [qt_ref[hd * SWA_HEAD_DIM:(hd + 1) * SWA_HEAD_DIM, t * sub:(t + 1) * sub] for hd in heads], axis=1)
        s = jnp.dot(kwin_ref[t * sub:t * sub + ws, :], q_all, preferred_element_type=F32)
        s = jnp.where(band, s, -jnp.inf)
        if t == 0:
            s = jnp.concatenate([s[:WINDOW] + no_prev, s[WINDOW:]], axis=0)
        if t == tq // sub - 1:
            s = jnp.concatenate([s[:ws - WINDOW], s[ws - WINDOW:] + no_next], axis=0)
        m = jnp.maximum(jnp.max(s, axis=0, keepdims=True), sink)
        p = jnp.exp2(s - m)
        denom = jnp.sum(p, axis=0, keepdims=True) + jnp.exp2(sink - m)
        o_t = jnp.dot(vwin_ref[:, t * sub:t * sub + ws], p.astype(BF16), preferred_element_type=F32) / denom
        for hd in heads:
            o_ref[t * sub:(t + 1) * sub, hd * SWA_HEAD_DIM:(hd + 1) * SWA_HEAD_DIM] = (
                o_t[:, hd * sub:(hd + 1) * sub].T.astype(o_ref.dtype))


def _swa_attention(sink, qt, k, vt, batch, seq, tq, sub):
    n = batch * seq
    tq = min(tq, seq)
    sub = min(sub, tq)
    assert sub & (sub - 1) == 0 and tq % sub == 0, "query index within a sub-tile is taken with a bit mask"
    nq = seq // tq
    r = tq // WINDOW
    nb = seq // WINDOW
    gw = SWA_GROUP * SWA_HEAD_DIM
    prev = lambda b, i: b * nb + jnp.maximum(i * r - 1, 0)
    nxt = lambda b, i: b * nb + jnp.minimum((i + 1) * r, nb - 1)
    edge = (WINDOW, SWA_HEAD_DIM)
    return pl.pallas_call(
        functools.partial(_swa_kernel, nq=nq, sub=sub),
        grid=(batch, SWA_KV_HEADS, nq),
        in_specs=[
            pl.BlockSpec(memory_space=pltpu.SMEM),
            pl.BlockSpec((gw, tq), lambda b, g, i: (g, b * nq + i)),
            pl.BlockSpec(edge, lambda b, g, i: (prev(b, i), g)),
            pl.BlockSpec((tq, SWA_HEAD_DIM), lambda b, g, i: (b * nq + i, g)),
            pl.BlockSpec(edge, lambda b, g, i: (nxt(b, i), g)),
            pl.BlockSpec(edge, lambda b, g, i: (g, prev(b, i))),
            pl.BlockSpec((SWA_HEAD_DIM, tq), lambda b, g, i: (g, b * nq + i)),
            pl.BlockSpec(edge, lambda b, g, i: (g, nxt(b, i))),
        ],
        out_specs=pl.BlockSpec((tq, gw), lambda b, g, i: (b * nq + i, g)),
        out_shape=jax.ShapeDtypeStruct((n, SWA_HEADS * SWA_HEAD_DIM), BF16),
        scratch_shapes=[pltpu.VMEM((tq + 2 * WINDOW, SWA_HEAD_DIM), BF16),
                        pltpu.VMEM((SWA_HEAD_DIM, tq + 2 * WINDOW), BF16)],
        compiler_params=_params("parallel", "parallel", "parallel"),
        name="swa_attn",
    )(sink, qt, k, k, k, vt, vt, vt)


def _out_proj_kernel(x_ref, oa_ref, ob_ref, ga_ref, gb_ref, wa_ref, wb_ref, g_mlp_ref, x1_ref, hm_ref):
    oa = _rms(oa_ref[...].astype(F32), ga_ref[...]).astype(BF16)
    ob = _rms(ob_ref[...].astype(F32), gb_ref[...]).astype(BF16)
    x1 = (x_ref[...] + jnp.dot(oa, wa_ref[...], preferred_element_type=F32)
          + jnp.dot(ob, wb_ref[...], preferred_element_type=F32))
    x1_ref[...] = x1
    hm_ref[...] = _rms(x1, g_mlp_ref[...]).astype(BF16)


def _out_proj(x, oa, ob, wts, tm):
    n, d = x.shape
    tm = min(tm, n)
    row = lambda i: (i, 0)
    consts = [wts[k] for k in ("g_out_mla", "g_out_swa", "w_out_a", "w_out_b", "g_mlp")]
    return pl.pallas_call(
        _out_proj_kernel, grid=(n // tm,),
        in_specs=[pl.BlockSpec((tm, d), row), pl.BlockSpec((tm, oa.shape[1]), row),
                  pl.BlockSpec((tm, ob.shape[1]), row)] + [_resident(a.shape) for a in consts],
        out_specs=(pl.BlockSpec((tm, d), row), pl.BlockSpec((tm, d), row)),
        out_shape=(jax.ShapeDtypeStruct((n, d), F32), jax.ShapeDtypeStruct((n, d), BF16)),
        compiler_params=_params("parallel"), name="out_proj",
    )(x, oa, ob, *consts)


def _mlp_kernel(x1_ref, hm_ref, wu_ref, wd_ref, o_ref):
    @pl.when(pl.program_id(1) == 0)
    def _():
        o_ref[...] = x1_ref[...]

    up = jnp.dot(hm_ref[...], wu_ref[...], preferred_element_type=F32)
    act = jnp.square(jnp.maximum(up, 0.0)).astype(BF16)
    o_ref[...] += jnp.dot(act, wd_ref[...], preferred_element_type=F32)


def _mlp(x1, hm, w_up, w_down, tm, tf):
    n, d = x1.shape
    f = w_up.shape[1]
    tm, tf = min(tm, n), min(tf, f)
    row = lambda i, j: (i, 0)
    return pl.pallas_call(
        _mlp_kernel, grid=(n // tm, f // tf),
        in_specs=[pl.BlockSpec((tm, d), row), pl.BlockSpec((tm, d), row),
                  pl.BlockSpec((d, tf), lambda i, j: (0, j)), pl.BlockSpec((tf, d), lambda i, j: (j, 0))],
        out_specs=pl.BlockSpec((tm, d), row),
        out_shape=jax.ShapeDtypeStruct((n, d), F32),
        compiler_params=_params("parallel", "arbitrary"), name="mlp",
    )(x1, hm, w_up, w_down)


def _ple_kernel(x_ref, p_ref, g_ref, wg_ref, wp_ref, o_ref):
    x = x_ref[...]
    z = jnp.dot(_rms(x, g_ref[...]).astype(BF16), wg_ref[...], preferred_element_type=F32)
    gate = 1.0 / (1.0 + jnp.exp(-z))
    e = jnp.dot(p_ref[...].astype(BF16), wp_ref[...], preferred_element_type=F32)
    o_ref[...] = x + e * gate


def _ple(x2, p, wts, tm):
    n, d = x2.shape
    tm = min(tm, n)
    row = lambda i: (i, 0)
    consts = [wts[k] for k in ("g_ple", "w_gate", "w_ple")]
    return pl.pallas_call(
        _ple_kernel, grid=(n // tm,),
        in_specs=[pl.BlockSpec((tm, d), row), pl.BlockSpec((tm, p.shape[1]), row)]
                 + [_resident(a.shape) for a in consts],
        out_specs=pl.BlockSpec((tm, d), row),
        out_shape=jax.ShapeDtypeStruct((n, d), F32),
        compiler_params=_params("parallel"), name="ple",
    )(x2, p, *consts)


def _swap_halves(a):
    half = a.shape[-1] // 2
    return jnp.concatenate([a[..., half:], a[..., :half]], axis=-1)


def _rope_tables(seq, dim):
    inv = 1.0 / (ROPE_THETA ** (jnp.arange(0, dim, 2, dtype=F32) / dim))
    ang = inv[:, None] * jnp.arange(seq, dtype=F32)[None, :]
    return jnp.cos(ang), jnp.sin(ang)


def _lane_tables(cos, sin):
    pad = jnp.zeros((cos.shape[0], LANES - 2 * cos.shape[1]), F32)
    return (jnp.concatenate([cos, cos, pad], axis=-1), jnp.concatenate([-sin, sin, pad], axis=-1))


def _prepare_weights(g_attn, w_in, g_qa, w_qb, g_kva, w_kvb, g_qn_mla, g_kn_mla, g_q_swa, g_k_swa,
                     g_out_mla, g_out_swa, w_out, g_mlp, w_up, w_down, g_ple, w_ple, w_gate):
    row = lambda g: g.reshape(1, -1)
    o_kpe = MLA_Q_LORA + MLA_KV_LORA
    o_qs = o_kpe + MLA_ROPE
    o_ks = o_qs + SWA_HEADS * SWA_HEAD_DIM
    o_vs = o_ks + SWA_KV_HEADS * SWA_HEAD_DIM
    kpe = w_in[:, o_kpe:o_qs]
    w_tm = jnp.concatenate([w_in[:, :o_kpe], kpe, _swap_halves(kpe), w_in[:, o_ks:o_vs]], axis=1)
    w_fm = jnp.concatenate([w_in[:, o_qs:o_ks], w_in[:, o_vs:]], axis=1).T
    w_kv = w_kvb.reshape(MLA_KV_LORA, MLA_HEADS, MLA_NOPE + MLA_V)
    w_kn = w_kv[..., :MLA_NOPE].reshape(MLA_KV_LORA, MLA_HEADS * MLA_NOPE)
    w_vt = w_kv[..., MLA_NOPE:].reshape(MLA_KV_LORA, MLA_HEADS * MLA_V).T
    g_rope = g_kn_mla[MLA_NOPE:]
    n_a = MLA_HEADS * MLA_V
    logit_span = 2.0 * MLA_QK ** 0.5 * LOG2E * jnp.max(jnp.abs(g_qn_mla)) * jnp.max(jnp.abs(g_kn_mla))
    return {
        "mla_logit_span": logit_span,
        "g_attn": row(g_attn), "w_tm": w_tm.astype(BF16), "w_fm": w_fm.astype(BF16),
        "g_qa": row(g_qa), "w_qbt": w_qb.T.astype(BF16),
        "g_kva": row(g_kva), "w_kn": w_kn.astype(BF16), "w_vt": w_vt.astype(BF16),
        "g_qn_mla": g_qn_mla, "gk_a": row(g_kn_mla[:MLA_NOPE]),
        "gk_b": row(jnp.concatenate([g_rope, _swap_halves(g_rope)])),
        "g_q_swa": g_q_swa, "g_k_swa": row(g_k_swa),
        "g_out_mla": row(g_out_mla), "g_out_swa": row(g_out_swa),
        "w_out_a": w_out[:n_a].astype(BF16), "w_out_b": w_out[n_a:].astype(BF16),
        "g_mlp": row(g_mlp), "w_up": w_up.astype(BF16), "w_down": w_down.astype(BF16),
        "g_ple": row(g_ple), "w_gate": w_gate.astype(BF16), "w_ple": w_ple.astype(BF16),
    }


def _layer(x, p, sink, wts, tabs):
    batch, seq, d = x.shape
    n = batch * seq
    xf = x.reshape(n, d)
    qt, k, vt, qst, ks, vst = _proj(xf, seq, wts, tabs, tm=512)
    o_a = lax.cond(
        wts["mla_logit_span"] <= SAFE_LOG2_RANGE,
        functools.partial(_mla_attention, batch=batch, seq=seq, tq=1024, tk=4096, ck=256, track_max=False),
        functools.partial(_mla_attention, batch=batch, seq=seq, tq=512, tk=512, ck=512, track_max=True),
        qt, k, vt)
    o_b = _swa_attention(sink, qst, ks, vst, batch, seq, tq=512, sub=256)
    x1, hm = _out_proj(xf, o_a, o_b, wts, tm=512)
    x2 = _mlp(x1, hm, wts["w_up"], wts["w_down"], tm=1024, tf=512)
    y = _ple(x2, p.reshape(n, -1), wts, tm=512)
    return y.reshape(batch, seq, d)


def kernel(x_prompt, x_sample, p_prompt, p_sample, g_attn, w_in, g_qa, w_qb, g_kva, w_kvb, g_qn_mla, g_kn_mla,
           g_q_swa, g_k_swa, sink, g_out_mla, g_out_swa, w_out, g_mlp, w_up, w_down, g_ple, w_ple, w_gate):
    y_prompt, y_sample = x_prompt, x_sample
    tabs = _rope_inputs(max(x_prompt.shape[1], x_sample.shape[1]))
    for l in range(g_attn.shape[0]):
        wts = _prepare_weights(g_attn[l], w_in[l], g_qa[l], w_qb[l], g_kva[l], w_kvb[l], g_qn_mla[l],
                               g_kn_mla[l], g_q_swa[l], g_k_swa[l], g_out_mla[l], g_out_swa[l], w_out[l],
                               g_mlp[l], w_up[l], w_down[l], g_ple[l], w_ple[l], w_gate[l])
        y_prompt = _layer(y_prompt, p_prompt[l], sink[l], wts, tabs)
        y_sample = _layer(y_sample, p_sample[l], sink[l], wts, tabs)
    return (y_prompt, y_sample)
```

```python
import functools
import math

import jax
import jax.numpy as jnp
from jax import lax
from jax.experimental import pallas as pl
from jax.experimental.pallas import tpu as pltpu

EPS = 1e-6
ROPE_THETA = 10000.0
WINDOW = 128
MLA_HEADS = 8
MLA_Q_LORA = 512
MLA_KV_LORA = 256
MLA_NOPE = 128
MLA_ROPE = 64
MLA_V = 128
MLA_QK = MLA_NOPE + MLA_ROPE
SWA_HEADS = 8
SWA_KV_HEADS = 2
SWA_GROUP = SWA_HEADS // SWA_KV_HEADS
SWA_HEAD_DIM = 128
LANES = 128
SUBLANES = 8
MLA_PAD = 2 * LANES
SAFE_LOG2_RANGE = 100.0
LOG2E = math.log2(math.e)
VMEM_LIMIT = 56 * 1024 * 1024

BF16 = jnp.bfloat16
F32 = jnp.float32
NT = (((1,), (1,)), ((), ()))


def _params(*sem):
    return pltpu.CompilerParams(dimension_semantics=sem, vmem_limit_bytes=VMEM_LIMIT)


def _resident(shape):
    nd = len(shape)
    return pl.BlockSpec(shape, lambda *_: (0,) * nd, pipeline_mode=pl.Buffered(1))


def _rms(x, g):
    return x * lax.rsqrt(jnp.mean(x * x, axis=-1, keepdims=True) + EPS) * g


def _rope_lanes(x, t_cos, t_sin):
    return x * t_cos + pltpu.roll(x, LANES // 2, axis=1) * t_sin


def _rope_rows(x1, x2, cos, sin):
    return x1 * cos - x2 * sin, x2 * cos + x1 * sin


def _proj_kernel(x_ref, g_attn_ref, w_tm_ref, w_fm_ref, g_qa_ref, w_qbt_ref, g_kva_ref, w_kn_ref, w_vt_ref,
                 gq_ref, gk_a_ref, gk_b_ref, gqs_ref, gks_ref,
                 mc_ref, ms_ref, sc_ref, ss_ref, tkc_ref, tks_ref, tsc_ref, tss_ref,
                 qt_ref, k_ref, vt_ref, qst_ref, ks_ref, vst_ref):
    tm = x_ref.shape[0]
    h = _rms(x_ref[...], g_attn_ref[...]).astype(BF16)
    p_tm = jnp.dot(h, w_tm_ref[...], preferred_element_type=F32)
    o_ckv = MLA_Q_LORA
    o_kpe = o_ckv + MLA_KV_LORA
    o_ks = o_kpe + LANES

    k_bound = MLA_QK ** 0.5 * jnp.maximum(jnp.max(jnp.abs(gk_a_ref[...]), axis=-1, keepdims=True),
                                          jnp.max(jnp.abs(gk_b_ref[...]), axis=-1, keepdims=True))
    shift_lane = (lax.broadcasted_iota(jnp.int32, (1, LANES), 1) == MLA_QK - LANES).astype(F32)
    first_row = lax.broadcasted_iota(jnp.int32, (MLA_PAD - MLA_QK, tm), 0) == 0

    cqn = _rms(p_tm[:, :o_ckv], g_qa_ref[...]).astype(BF16)
    q_t = lax.dot_general(w_qbt_ref[...], cqn, NT, preferred_element_type=F32)
    q_scale = MLA_QK ** -0.5 * LOG2E
    half = MLA_ROPE // 2
    for hd in range(MLA_HEADS):
        blk = q_t[hd * MLA_QK:(hd + 1) * MLA_QK]
        r = lax.rsqrt(jnp.sum(blk * blk, axis=0, keepdims=True) * (1.0 / MLA_QK) + EPS) * q_scale
        y = blk * gq_ref[...] * r
        shift = -(jnp.sqrt(jnp.sum(y * y, axis=0, keepdims=True)) * k_bound)
        o1, o2 = _rope_rows(y[MLA_NOPE:MLA_NOPE + half], y[MLA_NOPE + half:], mc_ref[...], ms_ref[...])
        qt_ref[hd, :MLA_NOPE, :] = y[:MLA_NOPE].astype(BF16)
        qt_ref[hd, MLA_NOPE:MLA_NOPE + half, :] = o1.astype(BF16)
        qt_ref[hd, MLA_NOPE + half:MLA_QK, :] = o2.astype(BF16)
        qt_ref[hd, MLA_QK:, :] = jnp.where(first_row, shift, 0.0).astype(BF16)

    ckvn = _rms(p_tm[:, o_ckv:o_kpe], g_kva_ref[...]).astype(BF16)
    kn_all = jnp.dot(ckvn, w_kn_ref[...], preferred_element_type=F32)
    kb = p_tm[:, o_kpe:o_ks]
    s_pe = 0.5 * jnp.sum(kb * kb, axis=-1, keepdims=True)
    kr = _rope_lanes(kb * gk_b_ref[...], tkc_ref[...], tks_ref[...])
    for hd in range(MLA_HEADS):
        kn = kn_all[:, hd * MLA_NOPE:(hd + 1) * MLA_NOPE]
        r = lax.rsqrt((jnp.sum(kn * kn, axis=-1, keepdims=True) + s_pe) * (1.0 / MLA_QK) + EPS)
        k_ref[hd, :, :LANES] = (kn * gk_a_ref[...] * r).astype(BF16)
        k_ref[hd, :, LANES:] = (kr * r + shift_lane).astype(BF16)
    v_t = lax.dot_general(w_vt_ref[...], ckvn, NT, preferred_element_type=F32)
    for hd in range(MLA_HEADS):
        vt_ref[hd] = v_t[hd * MLA_V:(hd + 1) * MLA_V].astype(BF16)

    p_fm = lax.dot_general(w_fm_ref[...], h, NT, preferred_element_type=F32)
    s_scale = SWA_HEAD_DIM ** -0.5 * LOG2E
    sh = SWA_HEAD_DIM // 2
    for hd in range(SWA_HEADS):
        blk = p_fm[hd * SWA_HEAD_DIM:(hd + 1) * SWA_HEAD_DIM]
        r = lax.rsqrt(jnp.mean(blk * blk, axis=0, keepdims=True) + EPS) * s_scale
        y = blk * gqs_ref[...] * r
        o1, o2 = _rope_rows(y[:sh], y[sh:], sc_ref[...], ss_ref[...])
        qst_ref[hd * SWA_HEAD_DIM:hd * SWA_HEAD_DIM + sh, :] = o1.astype(BF16)
        qst_ref[hd * SWA_HEAD_DIM + sh:(hd + 1) * SWA_HEAD_DIM, :] = o2.astype(BF16)
    vst_ref[...] = p_fm[SWA_HEADS * SWA_HEAD_DIM:].astype(BF16)
    for hd in range(SWA_KV_HEADS):
        xs = p_tm[:, o_ks + hd * SWA_HEAD_DIM: o_ks + (hd + 1) * SWA_HEAD_DIM]
        kn = _rope_lanes(_rms(xs, gks_ref[...]), tsc_ref[...], tss_ref[...])
        ks_ref[:, hd * SWA_HEAD_DIM:(hd + 1) * SWA_HEAD_DIM] = kn.astype(BF16)


def _rope_inputs(seq):
    fm_tabs, tm_tabs = [], []
    for dim in (MLA_ROPE, SWA_HEAD_DIM):
        half = dim // 2
        inv = 1.0 / (ROPE_THETA ** (jnp.arange(0, dim, 2, dtype=F32) / dim))
        fm_tabs += list(_cos_sin(inv[:, None], seq, axis=1))
        pad = jnp.zeros((LANES - dim,), F32)
        cos, sin = _cos_sin(jnp.concatenate([inv, inv, pad])[None, :], seq, axis=0)
        ones = jnp.ones((half,), F32)
        tm_tabs += [cos * jnp.concatenate([ones, ones, pad]), sin * jnp.concatenate([-ones, ones, pad])]
    return fm_tabs, tm_tabs


def _cos_sin(inv, seq, axis):
    blk = LANES
    assert seq % blk == 0
    shape = [1, 1]
    shape[axis] = -1
    pos_a = (jnp.arange(seq // blk, dtype=F32) * blk).reshape(shape)
    pos_b = jnp.arange(blk, dtype=F32).reshape(shape)
    ca, sa = jnp.cos(pos_a * inv), jnp.sin(pos_a * inv)
    cb, sb = jnp.cos(pos_b * inv), jnp.sin(pos_b * inv)
    ca, sa = jnp.expand_dims(ca, axis + 1), jnp.expand_dims(sa, axis + 1)
    cb, sb = jnp.expand_dims(cb, axis), jnp.expand_dims(sb, axis)
    out_shape = list(inv.shape)
    out_shape[axis] = seq
    return (ca * cb - sa * sb).reshape(out_shape), (sa * cb + ca * sb).reshape(out_shape)


def _proj(x, seq, wts, tabs, tm):
    n, d = x.shape
    tm = min(tm, seq)
    nt_seq = seq // tm
    row = lambda i: (i, 0)
    col = lambda i: (0, i)
    bcast = lambda g: jnp.broadcast_to(g.reshape(-1, 1), (g.size, tm))
    consts = [wts["g_attn"], wts["w_tm"], wts["w_fm"], wts["g_qa"], wts["w_qbt"], wts["g_kva"], wts["w_kn"],
              wts["w_vt"], bcast(wts["g_qn_mla"]), wts["gk_a"], wts["gk_b"], bcast(wts["g_q_swa"]), wts["g_k_swa"]]
    fm_tabs, tm_tabs = tabs
    in_specs = ([pl.BlockSpec((tm, d), row)] + [_resident(a.shape) for a in consts]
                + [pl.BlockSpec((t.shape[0], tm), lambda i: (0, i % nt_seq)) for t in fm_tabs]
                + [pl.BlockSpec((tm, LANES), lambda i: (i % nt_seq, 0))] * 4)
    n_qs = SWA_HEADS * SWA_HEAD_DIM
    n_ks = SWA_KV_HEADS * SWA_HEAD_DIM
    out_shape = (
        jax.ShapeDtypeStruct((MLA_HEADS, MLA_PAD, n), BF16),
        jax.ShapeDtypeStruct((MLA_HEADS, n, MLA_PAD), BF16),
        jax.ShapeDtypeStruct((MLA_HEADS, MLA_V, n), BF16),
        jax.ShapeDtypeStruct((n_qs, n), BF16),
        jax.ShapeDtypeStruct((n, n_ks), BF16),
        jax.ShapeDtypeStruct((n_ks, n), BF16),
    )
    out_specs = (
        pl.BlockSpec((MLA_HEADS, MLA_PAD, tm), lambda i: (0, 0, i)),
        pl.BlockSpec((MLA_HEADS, tm, MLA_PAD), lambda i: (0, i, 0)),
        pl.BlockSpec((MLA_HEADS, MLA_V, tm), lambda i: (0, 0, i)),
        pl.BlockSpec((n_qs, tm), col),
        pl.BlockSpec((tm, n_ks), row),
        pl.BlockSpec((n_ks, tm), col),
    )
    return pl.pallas_call(
        _proj_kernel, grid=(n // tm,), in_specs=in_specs, out_specs=out_specs,
        out_shape=out_shape, compiler_params=_params("parallel"), name="proj",
    )(x, *consts, *fm_tabs, *tm_tabs)


def _mla_kernel(qt_ref, k_ref, vt_ref, o_ref, *scratch, ck, track_max):
    j = pl.program_id(3)
    tk, tq = k_ref.shape[1], qt_ref.shape[2]
    if track_max:
        m_ref, l_ref, acc_ref = scratch
    else:
        l_ref, acc_ref, p_ref = scratch

    @pl.when(j == 0)
    def _():
        if track_max:
            m_ref[...] = jnp.full_like(m_ref, -jnp.inf)
        l_ref[...] = jnp.zeros_like(l_ref)
        acc_ref[...] = jnp.zeros_like(acc_ref)

    qt = qt_ref[0]
    if track_max:
        s = jnp.dot(k_ref[0], qt, preferred_element_type=F32)
        m_prev = m_ref[...]
        m_new = jnp.maximum(m_prev, jnp.max(s, axis=0, keepdims=True))
        alpha = jnp.exp2(m_prev - m_new)
        p = jnp.exp2(s - m_new)
        l_ref[...] = alpha * l_ref[...] + jnp.sum(p, axis=0, keepdims=True)
        acc_ref[...] = alpha * acc_ref[...] + jnp.dot(vt_ref[0], p.astype(BF16), preferred_element_type=F32)
        m_ref[...] = m_new
    else:
        l_part = l_ref[...]
        for c in range(tk // ck):
            s = jnp.dot(k_ref[0, c * ck:(c + 1) * ck, :], qt, preferred_element_type=F32)
            p = jnp.exp2(s)
            l_part = l_part + jnp.sum(p.reshape(ck // SUBLANES, SUBLANES, tq), axis=0)
            p_ref[c * ck:(c + 1) * ck, :] = p.astype(BF16)
        l_ref[...] = l_part
        acc_ref[...] += jnp.dot(vt_ref[0], p_ref[...], preferred_element_type=F32)

    @pl.when(j == pl.num_programs(3) - 1)
    def _():
        l = jnp.sum(l_ref[...], axis=0, keepdims=True)
        o_ref[...] = (acc_ref[...] / l).T.astype(o_ref.dtype)


def _mla_attention(qt, k, vt, batch, seq, tq, tk, ck, track_max):
    n = batch * seq
    tq, tk = min(tq, seq), min(tk, seq)
    nq, nk = seq // tq, seq // tk
    if track_max:
        scratch = [pltpu.VMEM((1, tq), F32), pltpu.VMEM((1, tq), F32), pltpu.VMEM((MLA_V, tq), F32)]
    else:
        scratch = [pltpu.VMEM((SUBLANES, tq), F32), pltpu.VMEM((MLA_V, tq), F32), pltpu.VMEM((tk, tq), BF16)]
    return pl.pallas_call(
        functools.partial(_mla_kernel, ck=min(ck, tk), track_max=track_max),
        grid=(batch, MLA_HEADS, nq, nk),
        in_specs=[
            pl.BlockSpec((1, MLA_PAD, tq), lambda b, h, i, j: (h, 0, b * nq + i)),
            pl.BlockSpec((1, tk, MLA_PAD), lambda b, h, i, j: (h, b * nk + j, 0)),
            pl.BlockSpec((1, MLA_V, tk), lambda b, h, i, j: (h, 0, b * nk + j)),
        ],
        out_specs=pl.BlockSpec((tq, MLA_V), lambda b, h, i, j: (b * nq + i, h)),
        out_shape=jax.ShapeDtypeStruct((n, MLA_HEADS * MLA_V), BF16),
        scratch_shapes=scratch,
        compiler_params=_params("parallel", "parallel", "parallel", "arbitrary"),
        name="mla_attn_online" if track_max else "mla_attn",
    )(qt, k, vt)


def _swa_kernel(sink_ref, qt_ref, kp_ref, km_ref, kn_ref, vp_ref, vm_ref, vn_ref, o_ref, kwin_ref, vwin_ref,
                *, nq, sub):
    g = pl.program_id(1)
    i = pl.program_id(2)
    tq = qt_ref.shape[1]
    kwin_ref[:WINDOW] = kp_ref[...]
    kwin_ref[WINDOW:WINDOW + tq] = km_ref[...]
    kwin_ref[WINDOW + tq:] = kn_ref[...]
    vwin_ref[:, :WINDOW] = vp_ref[...]
    vwin_ref[:, WINDOW:WINDOW + tq] = vm_ref[...]
    vwin_ref[:, WINDOW + tq:] = vn_ref[...]
    heads = range(SWA_GROUP)
    ws = sub + 2 * WINDOW
    sink = jnp.concatenate(
        [jnp.full((1, sub), sink_ref[g * SWA_GROUP + hd] * LOG2E, F32) for hd in heads], axis=1)
    key = lax.broadcasted_iota(jnp.int32, (ws, SWA_GROUP * sub), 0)
    qry = lax.broadcasted_iota(jnp.int32, (ws, SWA_GROUP * sub), 1) & (sub - 1)
    band = (key >= qry) & (key <= qry + 2 * WINDOW)
    no_prev = jnp.where(i > 0, 0.0, -jnp.inf)
    no_next = jnp.where(i < nq - 1, 0.0, -jnp.inf)
    for t in range(tq // sub):
        q_all = jnp.concatenate(
            [qt_ref[hd * SWA_HEAD_DIM:(hd + 1) * SWA_HEAD_DIM, t * sub:(t + 1) * sub] for hd in heads], axis=1)
        s = jnp.dot(kwin_ref[t * sub:t * sub + ws, :], q_all, preferred_element_type=F32)
        s = jnp.where(band, s, -jnp.inf)
        if t == 0:
            s = jnp.concatenate([s[:WINDOW] + no_prev, s[WINDOW:]], axis=0)
        if t == tq // sub - 1:
            s = jnp.concatenate([s[:ws - WINDOW], s[ws - WINDOW:] + no_next], axis=0)
        m = jnp.maximum(jnp.max(s, axis=0, keepdims=True), sink)
        p = jnp.exp2(s - m)
        denom = jnp.sum(p, axis=0, keepdims=True) + jnp.exp2(sink - m)
        o_t = jnp.dot(vwin_ref[:, t * sub:t * sub + ws], p.astype(BF16), preferred_element_type=F32) / denom
        for hd in heads:
            o_ref[t * sub:(t + 1) * sub, hd * SWA_HEAD_DIM:(hd + 1) * SWA_HEAD_DIM] = (
                o_t[:, hd * sub:(hd + 1) * sub].T.astype(o_ref.dtype))


def _swa_attention(sink, qt, k, vt, batch, seq, tq, sub):
    n = batch * seq
    tq = min(tq, seq)
    sub = min(sub, tq)
    assert sub & (sub - 1) == 0 and tq % sub == 0, "query index within a sub-tile is taken with a bit mask"
    nq = seq // tq
    r = tq // WINDOW
    nb = seq // WINDOW
    gw = SWA_GROUP * SWA_HEAD_DIM
    prev = lambda b, i: b * nb + jnp.maximum(i * r - 1, 0)
    nxt = lambda b, i: b * nb + jnp.minimum((i + 1) * r, nb - 1)
    edge = (WINDOW, SWA_HEAD_DIM)
    return pl.pallas_call(
        functools.partial(_swa_kernel, nq=nq, sub=sub),
        grid=(batch, SWA_KV_HEADS, nq),
        in_specs=[
            pl.BlockSpec(memory_space=pltpu.SMEM),
            pl.BlockSpec((gw, tq), lambda b, g, i: (g, b * nq + i)),
            pl.BlockSpec(edge, lambda b, g, i: (prev(b, i), g)),
            pl.BlockSpec((tq, SWA_HEAD_DIM), lambda b, g, i: (b * nq + i, g)),
            pl.BlockSpec(edge, lambda b, g, i: (nxt(b, i), g)),
            pl.BlockSpec(edge, lambda b, g, i: (g, prev(b, i))),
            pl.BlockSpec((SWA_HEAD_DIM, tq), lambda b, g, i: (g, b * nq + i)),
            pl.BlockSpec(edge, lambda b, g, i: (g, nxt(b, i))),
        ],
        out_specs=pl.BlockSpec((tq, gw), lambda b, g, i: (b * nq + i, g)),
        out_shape=jax.ShapeDtypeStruct((n, SWA_HEADS * SWA_HEAD_DIM), BF16),
        scratch_shapes=[pltpu.VMEM((tq + 2 * WINDOW, SWA_HEAD_DIM), BF16),
                        pltpu.VMEM((SWA_HEAD_DIM, tq + 2 * WINDOW), BF16)],
        compiler_params=_params("parallel", "parallel", "parallel"),
        name="swa_attn",
    )(sink, qt, k, k, k, vt, vt, vt)


def _out_proj_kernel(x_ref, oa_ref, ob_ref, ga_ref, gb_ref, wa_ref, wb_ref, g_mlp_ref, x1_ref, hm_ref):
    oa = _rms(oa_ref[...].astype(F32), ga_ref[...]).astype(BF16)
    ob = _rms(ob_ref[...].astype(F32), gb_ref[...]).astype(BF16)
    x1 = (x_ref[...] + jnp.dot(oa, wa_ref[...], preferred_element_type=F32)
          + jnp.dot(ob, wb_ref[...], preferred_element_type=F32))
    x1_ref[...] = x1
    hm_ref[...] = _rms(x1, g_mlp_ref[...]).astype(BF16)


def _out_proj(x, oa, ob, wts, tm):
    n, d = x.shape
    tm = min(tm, n)
    row = lambda i: (i, 0)
    consts = [wts[k] for k in ("g_out_mla", "g_out_swa", "w_out_a", "w_out_b", "g_mlp")]
    return pl.pallas_call(
        _out_proj_kernel, grid=(n // tm,),
        in_specs=[pl.BlockSpec((tm, d), row), pl.BlockSpec((tm, oa.shape[1]), row),
                  pl.BlockSpec((tm, ob.shape[1]), row)] + [_resident(a.shape) for a in consts],
        out_specs=(pl.BlockSpec((tm, d), row), pl.BlockSpec((tm, d), row)),
        out_shape=(jax.ShapeDtypeStruct((n, d), F32), jax.ShapeDtypeStruct((n, d), BF16)),
        compiler_params=_params("parallel"), name="out_proj",
    )(x, oa, ob, *consts)


def _mlp_kernel(x1_ref, hm_ref, wu_ref, wd_ref, o_ref):
    @pl.when(pl.program_id(1) == 0)
    def _():
        o_ref[...] = x1_ref[...]

    up = jnp.dot(hm_ref[...], wu_ref[...], preferred_element_type=F32)
    act = jnp.square(jnp.maximum(up, 0.0)).astype(BF16)
    o_ref[...] += jnp.dot(act, wd_ref[...], preferred_element_type=F32)


def _mlp(x1, hm, w_up, w_down, tm, tf):
    n, d = x1.shape
    f = w_up.shape[1]
    tm, tf = min(tm, n), min(tf, f)
    row = lambda i, j: (i, 0)
    return pl.pallas_call(
        _mlp_kernel, grid=(n // tm, f // tf),
        in_specs=[pl.BlockSpec((tm, d), row), pl.BlockSpec((tm, d), row),
                  pl.BlockSpec((d, tf), lambda i, j: (0, j)), pl.BlockSpec((tf, d), lambda i, j: (j, 0))],
        out_specs=pl.BlockSpec((tm, d), row),
        out_shape=jax.ShapeDtypeStruct((n, d), F32),
        compiler_params=_params("parallel", "arbitrary"), name="mlp",
    )(x1, hm, w_up, w_down)


def _ple_kernel(x_ref, p_ref, g_ref, wg_ref, wp_ref, o_ref):
    x = x_ref[...]
    z = jnp.dot(_rms(x, g_ref[...]).astype(BF16), wg_ref[...], preferred_element_type=F32)
    gate = 1.0 / (1.0 + jnp.exp(-z))
    e = jnp.dot(p_ref[...].astype(BF16), wp_ref[...], preferred_element_type=F32)
    o_ref[...] = x + e * gate


def _ple(x2, p, wts, tm):
    n, d = x2.shape
    tm = min(tm, n)
    row = lambda i: (i, 0)
    consts = [wts[k] for k in ("g_ple", "w_gate", "w_ple")]
    return pl.pallas_call(
        _ple_kernel, grid=(n // tm,),
        in_specs=[pl.BlockSpec((tm, d), row), pl.BlockSpec((tm, p.shape[1]), row)]
                 + [_resident(a.shape) for a in consts],
        out_specs=pl.BlockSpec((tm, d), row),
        out_shape=jax.ShapeDtypeStruct((n, d), F32),
        compiler_params=_params("parallel"), name="ple",
    )(x2, p, *consts)


def _swap_halves(a):
    half = a.shape[-1] // 2
    return jnp.concatenate([a[..., half:], a[..., :half]], axis=-1)


def _prepare_weights(g_attn, w_in, g_qa, w_qb, g_kva, w_kvb, g_qn_mla, g_kn_mla, g_q_swa, g_k_swa,
                     g_out_mla, g_out_swa, w_out, g_mlp, w_up, w_down, g_ple, w_ple, w_gate):
    row = lambda g: g.reshape(1, -1)
    o_kpe = MLA_Q_LORA + MLA_KV_LORA
    o_qs = o_kpe + MLA_ROPE
    o_ks = o_qs + SWA_HEADS * SWA_HEAD_DIM
    o_vs = o_ks + SWA_KV_HEADS * SWA_HEAD_DIM
    kpe = w_in[:, o_kpe:o_qs]
    w_tm = jnp.concatenate([w_in[:, :o_kpe], kpe, _swap_halves(kpe), w_in[:, o_ks:o_vs]], axis=1)
    w_fm = jnp.concatenate([w_in[:, o_qs:o_ks], w_in[:, o_vs:]], axis=1).T
    w_kv = w_kvb.reshape(MLA_KV_LORA, MLA_HEADS, MLA_NOPE + MLA_V)
    w_kn = w_kv[..., :MLA_NOPE].reshape(MLA_KV_LORA, MLA_HEADS * MLA_NOPE)
    w_vt = w_kv[..., MLA_NOPE:].reshape(MLA_KV_LORA, MLA_HEADS * MLA_V).T
    g_rope = g_kn_mla[MLA_NOPE:]
    n_a = MLA_HEADS * MLA_V
    logit_span = 2.0 * MLA_QK ** 0.5 * LOG2E * jnp.max(jnp.abs(g_qn_mla)) * jnp.max(jnp.abs(g_kn_mla))
    return {
        "mla_logit_span": logit_span,
        "g_attn": row(g_attn), "w_tm": w_tm.astype(BF16), "w_fm": w_fm.astype(BF16),
        "g_qa": row(g_qa), "w_qbt": w_qb.T.astype(BF16),
        "g_kva": row(g_kva), "w_kn": w_kn.astype(BF16), "w_vt": w_vt.astype(BF16),
        "g_qn_mla": g_qn_mla, "gk_a": row(g_kn_mla[:MLA_NOPE]),
        "gk_b": row(jnp.concatenate([g_rope, _swap_halves(g_rope)])),
        "g_q_swa": g_q_swa, "g_k_swa": row(g_k_swa),
        "g_out_mla": row(g_out_mla), "g_out_swa": row(g_out_swa),
        "w_out_a": w_out[:n_a].astype(BF16), "w_out_b": w_out[n_a:].astype(BF16),
        "g_mlp": row(g_mlp), "w_up": w_up.astype(BF16), "w_down": w_down.astype(BF16),
        "g_ple": row(g_ple), "w_gate": w_gate.astype(BF16), "w_ple": w_ple.astype(BF16),
    }


def _layer(x, p, sink, wts, tabs):
    batch, seq, d = x.shape
    n = batch * seq
    xf = x.reshape(n, d)
    qt, k, vt, qst, ks, vst = _proj(xf, seq, wts, tabs, tm=512)
    o_a = lax.cond(
        wts["mla_logit_span"] <= SAFE_LOG2_RANGE,
        functools.partial(_mla_attention, batch=batch, seq=seq, tq=1024, tk=4096, ck=256, track_max=False),
        functools.partial(_mla_attention, batch=batch, seq=seq, tq=512, tk=512, ck=512, track_max=True),
        qt, k, vt)
    o_b = _swa_attention(sink, qst, ks, vst, batch, seq, tq=512, sub=256)
    x1, hm = _out_proj(xf, o_a, o_b, wts, tm=512)
    x2 = _mlp(x1, hm, wts["w_up"], wts["w_down"], tm=1024, tf=512)
    y = _ple(x2, p.reshape(n, -1), wts, tm=512)
    return y.reshape(batch, seq, d)


def kernel(x_prompt, x_sample, p_prompt, p_sample, g_attn, w_in, g_qa, w_qb, g_kva, w_kvb, g_qn_mla, g_kn_mla,
           g_q_swa, g_k_swa, sink, g_out_mla, g_out_swa, w_out, g_mlp, w_up, w_down, g_ple, w_ple, w_gate):
    y_prompt, y_sample = x_prompt, x_sample
    tabs = _rope_inputs(max(x_prompt.shape[1], x_sample.shape[1]))
    for l in range(g_attn.shape[0]):
        wts = _prepare_weights(g_attn[l], w_in[l], g_qa[l], w_qb[l], g_kva[l], w_kvb[l], g_qn_mla[l],
                               g_kn_mla[l], g_q_swa[l], g_k_swa[l], g_out_mla[l], g_out_swa[l], w_out[l],
                               g_mlp[l], w_up[l], w_down[l], g_ple[l], w_ple[l], w_gate[l])
        y_prompt = _layer(y_prompt, p_prompt[l], sink[l], wts, tabs)
        y_sample = _layer(y_sample, p_sample[l], sink[l], wts, tabs)
    return (y_prompt, y_sample)
```

```python
import functools
import math

import jax
import jax.numpy as jnp
from jax import lax
from jax.experimental import pallas as pl
from jax.experimental.pallas import tpu as pltpu

EPS = 1e-6
ROPE_THETA = 10000.0
WINDOW = 128
MLA_HEADS = 8
MLA_Q_LORA = 512
MLA_KV_LORA = 256
MLA_NOPE = 128
MLA_ROPE = 64
MLA_V = 128
MLA_QK = MLA_NOPE + MLA_ROPE
SWA_HEADS = 8
SWA_KV_HEADS = 2
SWA_GROUP = SWA_HEADS // SWA_KV_HEADS
SWA_HEAD_DIM = 128
LANES = 128
SUBLANES = 8
MLA_PAD = 2 * LANES
SAFE_LOG2_RANGE = 100.0
LOG2E = math.log2(math.e)
VMEM_LIMIT = 56 * 1024 * 1024

BF16 = jnp.bfloat16
F32 = jnp.float32
NT = (((1,), (1,)), ((), ()))


def _params(*sem):
    return pltpu.CompilerParams(dimension_semantics=sem, vmem_limit_bytes=VMEM_LIMIT)


def _resident(shape):
    nd = len(shape)
    return pl.BlockSpec(shape, lambda *_: (0,) * nd, pipeline_mode=pl.Buffered(1))


def _rms(x, g):
    return x * lax.rsqrt(jnp.mean(x * x, axis=-1, keepdims=True) + EPS) * g


def _rope_lanes(x, t_cos, t_sin):
    return x * t_cos + pltpu.roll(x, LANES // 2, axis=1) * t_sin


def _rope_rows(x1, x2, cos, sin):
    return x1 * cos - x2 * sin, x2 * cos + x1 * sin


def _proj_kernel(x_ref, g_attn_ref, w_tm_ref, w_fm_ref, g_qa_ref, w_qbt_ref, g_kva_ref, w_kn_ref, w_vt_ref,
                 gq_ref, gk_a_ref, gk_b_ref, gqs_ref, gks_ref,
                 mc_ref, ms_ref, sc_ref, ss_ref, tkc_ref, tks_ref, tsc_ref, tss_ref,
                 qt_ref, k_ref, vt_ref, qst_ref, ks_ref, vst_ref):
    tm = x_ref.shape[0]
    h = _rms(x_ref[...], g_attn_ref[...]).astype(BF16)
    p_tm = jnp.dot(h, w_tm_ref[...], preferred_element_type=F32)
    o_ckv = MLA_Q_LORA
    o_kpe = o_ckv + MLA_KV_LORA
    o_ks = o_kpe + LANES

    k_bound = MLA_QK ** 0.5 * jnp.maximum(jnp.max(jnp.abs(gk_a_ref[...]), axis=-1, keepdims=True),
                                          jnp.max(jnp.abs(gk_b_ref[...]), axis=-1, keepdims=True))
    shift_lane = (lax.broadcasted_iota(jnp.int32, (1, LANES), 1) == MLA_QK - LANES).astype(F32)
    first_row = lax.broadcasted_iota(jnp.int32, (MLA_PAD - MLA_QK, tm), 0) == 0

    cqn = _rms(p_tm[:, :o_ckv], g_qa_ref[...]).astype(BF16)
    q_t = lax.dot_general(w_qbt_ref[...], cqn, NT, preferred_element_type=F32)
    q_scale = MLA_QK ** -0.5 * LOG2E
    half = MLA_ROPE // 2
    for hd in range(MLA_HEADS):
        blk = q_t[hd * MLA_QK:(hd + 1) * MLA_QK]
        r = lax.rsqrt(jnp.sum(blk * blk, axis=0, keepdims=True) * (1.0 / MLA_QK) + EPS) * q_scale
        y = blk * gq_ref[...] * r
        shift = -(jnp.sqrt(jnp.sum(y * y, axis=0, keepdims=True)) * k_bound)
        o1, o2 = _rope_rows(y[MLA_NOPE:MLA_NOPE + half], y[MLA_NOPE + half:], mc_ref[...], ms_ref[...])
        qt_ref[hd, :MLA_NOPE, :] = y[:MLA_NOPE].astype(BF16)
        qt_ref[hd, MLA_NOPE:MLA_NOPE + half, :] = o1.astype(BF16)
        qt_ref[hd, MLA_NOPE + half:MLA_QK, :] = o2.astype(BF16)
        qt_ref[hd, MLA_QK:, :] = jnp.where(first_row, shift, 0.0).astype(BF16)

    ckvn = _rms(p_tm[:, o_ckv:o_kpe], g_kva_ref[...]).astype(BF16)
    kn_all = jnp.dot(ckvn, w_kn_ref[...], preferred_element_type=F32)
    kb = p_tm[:, o_kpe:o_ks]
    s_pe = 0.5 * jnp.sum(kb * kb, axis=-1, keepdims=True)
    kr = _rope_lanes(kb * gk_b_ref[...], tkc_ref[...], tks_ref[...])
    for hd in range(MLA_HEADS):
        kn = kn_all[:, hd * MLA_NOPE:(hd + 1) * MLA_NOPE]
        r = lax.rsqrt((jnp.sum(kn * kn, axis=-1, keepdims=True) + s_pe) * (1.0 / MLA_QK) + EPS)
        k_ref[hd, :, :LANES] = (kn * gk_a_ref[...] * r).astype(BF16)
        k_ref[hd, :, LANES:] = (kr * r + shift_lane).astype(BF16)
    v_t = lax.dot_general(w_vt_ref[...], ckvn, NT, preferred_element_type=F32)
    for hd in range(MLA_HEADS):
        vt_ref[hd] = v_t[hd * MLA_V:(hd + 1) * MLA_V].astype(BF16)

    p_fm = lax.dot_general(w_fm_ref[...], h, NT, preferred_element_type=F32)
    s_scale = SWA_HEAD_DIM ** -0.5 * LOG2E
    sh = SWA_HEAD_DIM // 2
    for hd in range(SWA_HEADS):
        blk = p_fm[hd * SWA_HEAD_DIM:(hd + 1) * SWA_HEAD_DIM]
        r = lax.rsqrt(jnp.mean(blk * blk, axis=0, keepdims=True) + EPS) * s_scale
        y = blk * gqs_ref[...] * r
        o1, o2 = _rope_rows(y[:sh], y[sh:], sc_ref[...], ss_ref[...])
        qst_ref[hd * SWA_HEAD_DIM:hd * SWA_HEAD_DIM + sh, :] = o1.astype(BF16)
        qst_ref[hd * SWA_HEAD_DIM + sh:(hd + 1) * SWA_HEAD_DIM, :] = o2.astype(BF16)
    vst_ref[...] = p_fm[SWA_HEADS * SWA_HEAD_DIM:].astype(BF16)
    for hd in range(SWA_KV_HEADS):
        xs = p_tm[:, o_ks + hd * SWA_HEAD_DIM: o_ks + (hd + 1) * SWA_HEAD_DIM]
        kn = _rope_lanes(_rms(xs, gks_ref[...]), tsc_ref[...], tss_ref[...])
        ks_ref[:, hd * SWA_HEAD_DIM:(hd + 1) * SWA_HEAD_DIM] = kn.astype(BF16)


def _rope_inputs(seq):
    fm_tabs, tm_tabs = [], []
    for dim in (MLA_ROPE, SWA_HEAD_DIM):
        half = dim // 2
        inv = 1.0 / (ROPE_THETA ** (jnp.arange(0, dim, 2, dtype=F32) / dim))
        fm_tabs += list(_cos_sin(inv[:, None], seq, axis=1))
        pad = jnp.zeros((LANES - dim,), F32)
        cos, sin = _cos_sin(jnp.concatenate([inv, inv, pad])[None, :], seq, axis=0)
        ones = jnp.ones((half,), F32)
        tm_tabs += [cos * jnp.concatenate([ones, ones, pad]), sin * jnp.concatenate([-ones, ones, pad])]
    return fm_tabs, tm_tabs


def _cos_sin(inv, seq, axis):
    blk = LANES
    assert seq % blk == 0
    shape = [1, 1]
    shape[axis] = -1
    pos_a = (jnp.arange(seq // blk, dtype=F32) * blk).reshape(shape)
    pos_b = jnp.arange(blk, dtype=F32).reshape(shape)
    ca, sa = jnp.cos(pos_a * inv), jnp.sin(pos_a * inv)
    cb, sb = jnp.cos(pos_b * inv), jnp.sin(pos_b * inv)
    ca, sa = jnp.expand_dims(ca, axis + 1), jnp.expand_dims(sa, axis + 1)
    cb, sb = jnp.expand_dims(cb, axis), jnp.expand_dims(sb, axis)
    out_shape = list(inv.shape)
    out_shape[axis] = seq
    return (ca * cb - sa * sb).reshape(out_shape), (sa * cb + ca * sb).reshape(out_shape)


def _proj(x, seq, wts, tabs, tm):
    n, d = x.shape
    tm = min(tm, seq)
    nt_seq = seq // tm
    row = lambda i: (i, 0)
    col = lambda i: (0, i)
    bcast = lambda g: jnp.broadcast_to(g.reshape(-1, 1), (g.size, tm))
    consts = [wts["g_attn"], wts["w_tm"], wts["w_fm"], wts["g_qa"], wts["w_qbt"], wts["g_kva"], wts["w_kn"],
              wts["w_vt"], bcast(wts["g_qn_mla"]), wts["gk_a"], wts["gk_b"], bcast(wts["g_q_swa"]), wts["g_k_swa"]]
    fm_tabs, tm_tabs = tabs
    in_specs = ([pl.BlockSpec((tm, d), row)] + [_resident(a.shape) for a in consts]
                + [pl.BlockSpec((t.shape[0], tm), lambda i: (0, i % nt_seq)) for t in fm_tabs]
                + [pl.BlockSpec((tm, LANES), lambda i: (i % nt_seq, 0))] * 4)
    n_qs = SWA_HEADS * SWA_HEAD_DIM
    n_ks = SWA_KV_HEADS * SWA_HEAD_DIM
    out_shape = (
        jax.ShapeDtypeStruct((MLA_HEADS, MLA_PAD, n), BF16),
        jax.ShapeDtypeStruct((MLA_HEADS, n, MLA_PAD), BF16),
        jax.ShapeDtypeStruct((MLA_HEADS, MLA_V, n), BF16),
        jax.ShapeDtypeStruct((n_qs, n), BF16),
        jax.ShapeDtypeStruct((n, n_ks), BF16),
        jax.ShapeDtypeStruct((n_ks, n), BF16),
    )
    out_specs = (
        pl.BlockSpec((MLA_HEADS, MLA_PAD, tm), lambda i: (0, 0, i)),
        pl.BlockSpec((MLA_HEADS, tm, MLA_PAD), lambda i: (0, i, 0)),
        pl.BlockSpec((MLA_HEADS, MLA_V, tm), lambda i: (0, 0, i)),
        pl.BlockSpec((n_qs, tm), col),
        pl.BlockSpec((tm, n_ks), row),
        pl.BlockSpec((n_ks, tm), col),
    )
    return pl.pallas_call(
        _proj_kernel, grid=(n // tm,), in_specs=in_specs, out_specs=out_specs,
        out_shape=out_shape, compiler_params=_params("parallel"), name="proj",
    )(x, *consts, *fm_tabs, *tm_tabs)


def _mla_kernel(qt_ref, k_ref, vt_ref, o_ref, *scratch, ck, track_max):
    j = pl.program_id(3)
    tk, tq = k_ref.shape[1], qt_ref.shape[2]
    if track_max:
        m_ref, l_ref, acc_ref = scratch
    else:
        l_ref, acc_ref, p_ref = scratch

    @pl.when(j == 0)
    def _():
        if track_max:
            m_ref[...] = jnp.full_like(m_ref, -jnp.inf)
        l_ref[...] = jnp.zeros_like(l_ref)
        acc_ref[...] = jnp.zeros_like(acc_ref)

    qt = qt_ref[0]
    if track_max:
        s = jnp.dot(k_ref[0], qt, preferred_element_type=F32)
        m_prev = m_ref[...]
        m_new = jnp.maximum(m_prev, jnp.max(s, axis=0, keepdims=True))
        alpha = jnp.exp2(m_prev - m_new)
        p = jnp.exp2(s - m_new)
        l_ref[...] = alpha * l_ref[...] + jnp.sum(p, axis=0, keepdims=True)
        acc_ref[...] = alpha * acc_ref[...] + jnp.dot(vt_ref[0], p.astype(BF16), preferred_element_type=F32)
        m_ref[...] = m_new
    else:
        l_part = l_ref[...]
        for c in range(tk // ck):
            s = jnp.dot(k_ref[0, c * ck:(c + 1) * ck, :], qt, preferred_element_type=F32)
            p = jnp.exp2(s)
            l_part = l_part + jnp.sum(p.reshape(ck // SUBLANES, SUBLANES, tq), axis=0)
            p_ref[c * ck:(c + 1) * ck, :] = p.astype(BF16)
        l_ref[...] = l_part
        acc_ref[...] += jnp.dot(vt_ref[0], p_ref[...], preferred_element_type=F32)

    @pl.when(j == pl.num_programs(3) - 1)
    def _():
        l = jnp.sum(l_ref[...], axis=0, keepdims=True)
        o_ref[...] = (acc_ref[...] / l).T.astype(o_ref.dtype)


def _mla_attention(qt, k, vt, batch, seq, tq, tk, ck, track_max):
    n = batch * seq
    tq, tk = min(tq, seq), min(tk, seq)
    nq, nk = seq // tq, seq // tk
    if track_max:
        scratch = [pltpu.VMEM((1, tq), F32), pltpu.VMEM((1, tq), F32), pltpu.VMEM((MLA_V, tq), F32)]
    else:
        scratch = [pltpu.VMEM((SUBLANES, tq), F32), pltpu.VMEM((MLA_V, tq), F32), pltpu.VMEM((tk, tq), BF16)]
    return pl.pallas_call(
        functools.partial(_mla_kernel, ck=min(ck, tk), track_max=track_max),
        grid=(batch, MLA_HEADS, nq, nk),
        in_specs=[
            pl.BlockSpec((1, MLA_PAD, tq), lambda b, h, i, j: (h, 0, b * nq + i)),
            pl.BlockSpec((1, tk, MLA_PAD), lambda b, h, i, j: (h, b * nk + j, 0)),
            pl.BlockSpec((1, MLA_V, tk), lambda b, h, i, j: (h, 0, b * nk + j)),
        ],
        out_specs=pl.BlockSpec((tq, MLA_V), lambda b, h, i, j: (b * nq + i, h)),
        out_shape=jax.ShapeDtypeStruct((n, MLA_HEADS * MLA_V), BF16),
        scratch_shapes=scratch,
        compiler_params=_params("parallel", "parallel", "parallel", "arbitrary"),
        name="mla_attn_online" if track_max else "mla_attn",
    )(qt, k, vt)


def _swa_kernel(sink_ref, qt_ref, kp_ref, km_ref, kn_ref, vp_ref, vm_ref, vn_ref, o_ref, kwin_ref, vwin_ref,
                *, nq, sub):
    g = pl.program_id(1)
    i = pl.program_id(2)
    tq = qt_ref.shape[1]
    kwin_ref[:WINDOW] = kp_ref[...]
    kwin_ref[WINDOW:WINDOW + tq] = km_ref[...]
    kwin_ref[WINDOW + tq:] = kn_ref[...]
    vwin_ref[:, :WINDOW] = vp_ref[...]
    vwin_ref[:, WINDOW:WINDOW + tq] = vm_ref[...]
    vwin_ref[:, WINDOW + tq:] = vn_ref[...]
    heads = range(SWA_GROUP)
    ws = sub + 2 * WINDOW
    sink = jnp.concatenate(
        [jnp.full((1, sub), sink_ref[g * SWA_GROUP + hd] * LOG2E, F32) for hd in heads], axis=1)
    key = lax.broadcasted_iota(jnp.int32, (ws, SWA_GROUP * sub), 0)
    qry = lax.broadcasted_iota(jnp.int32, (ws, SWA_GROUP * sub), 1) & (sub - 1)
    band = (key >= qry) & (key <= qry + 2 * WINDOW)
    no_prev = jnp.where(i > 0, 0.0, -jnp.inf)
    no_next = jnp.where(i < nq - 1, 0.0, -jnp.inf)
    for t in range(tq // sub):
        q_all = jnp.concatenate(
            [qt_ref[hd * SWA_HEAD_DIM:(hd + 1) * SWA_HEAD_DIM, t * sub:(t + 1) * sub] for hd in heads], axis=1)
        s = jnp.dot(kwin_ref[t * sub:t * sub + ws, :], q_all, preferred_element_type=F32)
        s = jnp.where(band, s, -jnp.inf)
        if t == 0:
            s = jnp.concatenate([s[:WINDOW] + no_prev, s[WINDOW:]], axis=0)
        if t == tq // sub - 1:
            s = jnp.concatenate([s[:ws - WINDOW], s[ws - WINDOW:] + no_next], axis=0)
        m = jnp.maximum(jnp.max(s, axis=0, keepdims=True), sink)
        p = jnp.exp2(s - m)
        denom = jnp.sum(p, axis=0, keepdims=True) + jnp.exp2(sink - m)
        o_t = jnp.dot(vwin_ref[:, t * sub:t * sub + ws], p.astype(BF16), preferred_element_type=F32) / denom
        for hd in heads:
            o_ref[t * sub:(t + 1) * sub, hd * SWA_HEAD_DIM:(hd + 1) * SWA_HEAD_DIM] = (
                o_t[:, hd * sub:(hd + 1) * sub].T.astype(o_ref.dtype))


def _swa_attention(sink, qt, k, vt, batch, seq, tq, sub):
    n = batch * seq
    tq = min(tq, seq)
    sub = min(sub, tq)
    assert sub & (sub - 1) == 0 and tq % sub == 0, "query index within a sub-tile is taken with a bit mask"
    nq = seq // tq
    r = tq // WINDOW
    nb = seq // WINDOW
    gw = SWA_GROUP * SWA_HEAD_DIM
    prev = lambda b, i: b * nb + jnp.maximum(i * r - 1, 0)
    nxt = lambda b, i: b * nb + jnp.minimum((i + 1) * r, nb - 1)
    edge = (WINDOW, SWA_HEAD_DIM)
    return pl.pallas_call(
        functools.partial(_swa_kernel, nq=nq, sub=sub),
        grid=(batch, SWA_KV_HEADS, nq),
        in_specs=[
            pl.BlockSpec(memory_space=pltpu.SMEM),
            pl.BlockSpec((gw, tq), lambda b, g, i: (g, b * nq + i)),
            pl.BlockSpec(edge, lambda b, g, i: (prev(b, i), g)),
            pl.BlockSpec((tq, SWA_HEAD_DIM), lambda b, g, i: (b * nq + i, g)),
            pl.BlockSpec(edge, lambda b, g, i: (nxt(b, i), g)),
            pl.BlockSpec(edge, lambda b, g, i: (g, prev(b, i))),
            pl.BlockSpec((SWA_HEAD_DIM, tq), lambda b, g, i: (g, b * nq + i)),
            pl.BlockSpec(edge, lambda b, g, i: (g, nxt(b, i))),
        ],
        out_specs=pl.BlockSpec((tq, gw), lambda b, g, i: (b * nq + i, g)),
        out_shape=jax.ShapeDtypeStruct((n, SWA_HEADS * SWA_HEAD_DIM), BF16),
        scratch_shapes=[pltpu.VMEM((tq + 2 * WINDOW, SWA_HEAD_DIM), BF16),
                        pltpu.VMEM((SWA_HEAD_DIM, tq + 2 * WINDOW), BF16)],
        compiler_params=_params("parallel", "parallel", "parallel"),
        name="swa_attn",
    )(sink, qt, k, k, k, vt, vt, vt)


def _out_proj_kernel(x_ref, oa_ref, ob_ref, ga_ref, gb_ref, wa_ref, wb_ref, g_mlp_ref, x1_ref, hm_ref):
    oa = _rms(oa_ref[...].astype(F32), ga_ref[...]).astype(BF16)
    ob = _rms(ob_ref[...].astype(F32), gb_ref[...]).astype(BF16)
    x1 = (x_ref[...] + jnp.dot(oa, wa_ref[...], preferred_element_type=F32)
          + jnp.dot(ob, wb_ref[...], preferred_element_type=F32))
    x1_ref[...] = x1
    hm_ref[...] = _rms(x1, g_mlp_ref[...]).astype(BF16)


def _out_proj(x, oa, ob, wts, tm):
    n, d = x.shape
    tm = min(tm, n)
    row = lambda i: (i, 0)
    consts = [wts[k] for k in ("g_out_mla", "g_out_swa", "w_out_a", "w_out_b", "g_mlp")]
    return pl.pallas_call(
        _out_proj_kernel, grid=(n // tm,),
        in_specs=[pl.BlockSpec((tm, d), row), pl.BlockSpec((tm, oa.shape[1]), row),
                  pl.BlockSpec((tm, ob.shape[1]), row)] + [_resident(a.shape) for a in consts],
        out_specs=(pl.BlockSpec((tm, d), row), pl.BlockSpec((tm, d), row)),
        out_shape=(jax.ShapeDtypeStruct((n, d), F32), jax.ShapeDtypeStruct((n, d), BF16)),
        compiler_params=_params("parallel"), name="out_proj",
    )(x, oa, ob, *consts)


def _mlp_kernel(x1_ref, hm_ref, wu_ref, wd_ref, o_ref):
    @pl.when(pl.program_id(1) == 0)
    def _():
        o_ref[...] = x1_ref[...]

    up = jnp.dot(hm_ref[...], wu_ref[...], preferred_element_type=F32)
    act = jnp.square(jnp.maximum(up, 0.0)).astype(BF16)
    o_ref[...] += jnp.dot(act, wd_ref[...], preferred_element_type=F32)


def _mlp(x1, hm, w_up, w_down, tm, tf):
    n, d = x1.shape
    f = w_up.shape[1]
    tm, tf = min(tm, n), min(tf, f)
    row = lambda i, j: (i, 0)
    early = lambda i, j: (jnp.minimum(i + jnp.where(j >= 2, 1, 0), n // tm - 1), 0)
    return pl.pallas_call(
        _mlp_kernel, grid=(n // tm, f // tf),
        in_specs=[pl.BlockSpec((tm, d), early), pl.BlockSpec((tm, d), row),
                  pl.BlockSpec((d, tf), lambda i, j: (0, j)), pl.BlockSpec((tf, d), lambda i, j: (j, 0))],
        out_specs=pl.BlockSpec((tm, d), row),
        out_shape=jax.ShapeDtypeStruct((n, d), F32),
        compiler_params=_params("parallel", "arbitrary"), name="mlp",
    )(x1, hm, w_up, w_down)


def _ple_kernel(x_ref, p_ref, g_ref, wg_ref, wp_ref, o_ref):
    x = x_ref[...]
    z = jnp.dot(_rms(x, g_ref[...]).astype(BF16), wg_ref[...], preferred_element_type=F32)
    gate = 1.0 / (1.0 + jnp.exp(-z))
    e = jnp.dot(p_ref[...].astype(BF16), wp_ref[...], preferred_element_type=F32)
    o_ref[...] = x + e * gate


def _ple(x2, p, wts, tm):
    n, d = x2.shape
    tm = min(tm, n)
    row = lambda i: (i, 0)
    consts = [wts[k] for k in ("g_ple", "w_gate", "w_ple")]
    return pl.pallas_call(
        _ple_kernel, grid=(n // tm,),
        in_specs=[pl.BlockSpec((tm, d), row), pl.BlockSpec((tm, p.shape[1]), row)]
                 + [_resident(a.shape) for a in consts],
        out_specs=pl.BlockSpec((tm, d), row),
        out_shape=jax.ShapeDtypeStruct((n, d), F32),
        compiler_params=_params("parallel"), name="ple",
    )(x2, p, *consts)


def _swap_halves(a):
    half = a.shape[-1] // 2
    return jnp.concatenate([a[..., half:], a[..., :half]], axis=-1)


def _prepare_weights(g_attn, w_in, g_qa, w_qb, g_kva, w_kvb, g_qn_mla, g_kn_mla, g_q_swa, g_k_swa,
                     g_out_mla, g_out_swa, w_out, g_mlp, w_up, w_down, g_ple, w_ple, w_gate):
    row = lambda g: g.reshape(1, -1)
    o_kpe = MLA_Q_LORA + MLA_KV_LORA
    o_qs = o_kpe + MLA_ROPE
    o_ks = o_qs + SWA_HEADS * SWA_HEAD_DIM
    o_vs = o_ks + SWA_KV_HEADS * SWA_HEAD_DIM
    kpe = w_in[:, o_kpe:o_qs]
    w_tm = jnp.concatenate([w_in[:, :o_kpe], kpe, _swap_halves(kpe), w_in[:, o_ks:o_vs]], axis=1)
    w_fm = jnp.concatenate([w_in[:, o_qs:o_ks], w_in[:, o_vs:]], axis=1).T
    w_kv = w_kvb.reshape(MLA_KV_LORA, MLA_HEADS, MLA_NOPE + MLA_V)
    w_kn = w_kv[..., :MLA_NOPE].reshape(MLA_KV_LORA, MLA_HEADS * MLA_NOPE)
    w_vt = w_kv[..., MLA_NOPE:].reshape(MLA_KV_LORA, MLA_HEADS * MLA_V).T
    g_rope = g_kn_mla[MLA_NOPE:]
    n_a = MLA_HEADS * MLA_V
    logit_span = 2.0 * MLA_QK ** 0.5 * LOG2E * jnp.max(jnp.abs(g_qn_mla)) * jnp.max(jnp.abs(g_kn_mla))
    return {
        "mla_logit_span": logit_span,
        "g_attn": row(g_attn), "w_tm": w_tm.astype(BF16), "w_fm": w_fm.astype(BF16),
        "g_qa": row(g_qa), "w_qbt": w_qb.T.astype(BF16),
        "g_kva": row(g_kva), "w_kn": w_kn.astype(BF16), "w_vt": w_vt.astype(BF16),
        "g_qn_mla": g_qn_mla, "gk_a": row(g_kn_mla[:MLA_NOPE]),
        "gk_b": row(jnp.concatenate([g_rope, _swap_halves(g_rope)])),
        "g_q_swa": g_q_swa, "g_k_swa": row(g_k_swa),
        "g_out_mla": row(g_out_mla), "g_out_swa": row(g_out_swa),
        "w_out_a": w_out[:n_a].astype(BF16), "w_out_b": w_out[n_a:].astype(BF16),
        "g_mlp": row(g_mlp), "w_up": w_up.astype(BF16), "w_down": w_down.astype(BF16),
        "g_ple": row(g_ple), "w_gate": w_gate.astype(BF16), "w_ple": w_ple.astype(BF16),
    }


def _layer(x, p, sink, wts, tabs):
    batch, seq, d = x.shape
    n = batch * seq
    xf = x.reshape(n, d)
    qt, k, vt, qst, ks, vst = _proj(xf, seq, wts, tabs, tm=512)
    o_a = lax.cond(
        wts["mla_logit_span"] <= SAFE_LOG2_RANGE,
        functools.partial(_mla_attention, batch=batch, seq=seq, tq=1024, tk=4096, ck=256, track_max=False),
        functools.partial(_mla_attention, batch=batch, seq=seq, tq=512, tk=512, ck=512, track_max=True),
        qt, k, vt)
    o_b = _swa_attention(sink, qst, ks, vst, batch, seq, tq=512, sub=256)
    x1, hm = _out_proj(xf, o_a, o_b, wts, tm=512)
    x2 = _mlp(x1, hm, wts["w_up"], wts["w_down"], tm=1024, tf=512)
    y = _ple(x2, p.reshape(n, -1), wts, tm=512)
    return y.reshape(batch, seq, d)


def kernel(x_prompt, x_sample, p_prompt, p_sample, g_attn, w_in, g_qa, w_qb, g_kva, w_kvb, g_qn_mla, g_kn_mla,
           g_q_swa, g_k_swa, sink, g_out_mla, g_out_swa, w_out, g_mlp, w_up, w_down, g_ple, w_ple, w_gate):
    y_prompt, y_sample = x_prompt, x_sample
    tabs = _rope_inputs(max(x_prompt.shape[1], x_sample.shape[1]))
    for l in range(g_attn.shape[0]):
        wts = _prepare_weights(g_attn[l], w_in[l], g_qa[l], w_qb[l], g_kva[l], w_kvb[l], g_qn_mla[l],
                               g_kn_mla[l], g_q_swa[l], g_k_swa[l], g_out_mla[l], g_out_swa[l], w_out[l],
                               g_mlp[l], w_up[l], w_down[l], g_ple[l], w_ple[l], w_gate[l])
        y_prompt = _layer(y_prompt, p_prompt[l], sink[l], wts, tabs)
        y_sample = _layer(y_sample, p_sample[l], sink[l], wts, tabs)
    return (y_prompt, y_sample)
```

```python
import functools
import math

import jax
import jax.numpy as jnp
from jax import lax
from jax.experimental import pallas as pl
from jax.experimental.pallas import tpu as pltpu

EPS = 1e-6
ROPE_THETA = 10000.0
WINDOW = 128
MLA_HEADS = 8
MLA_Q_LORA = 512
MLA_KV_LORA = 256
MLA_NOPE = 128
MLA_ROPE = 64
MLA_V = 128
MLA_QK = MLA_NOPE + MLA_ROPE
SWA_HEADS = 8
SWA_KV_HEADS = 2
SWA_GROUP = SWA_HEADS // SWA_KV_HEADS
SWA_HEAD_DIM = 128
LANES = 128
SUBLANES = 8
MLA_PAD = 2 * LANES
SAFE_LOG2_RANGE = 100.0
LOG2E = math.log2(math.e)
VMEM_LIMIT = 56 * 1024 * 1024

BF16 = jnp.bfloat16
F32 = jnp.float32
STREAM = jnp.bfloat16
NT = (((1,), (1,)), ((), ()))


def _params(*sem):
    return pltpu.CompilerParams(dimension_semantics=sem, vmem_limit_bytes=VMEM_LIMIT)


def _resident(shape):
    nd = len(shape)
    return pl.BlockSpec(shape, lambda *_: (0,) * nd, pipeline_mode=pl.Buffered(1))


def _rms(x, g):
    return x * lax.rsqrt(jnp.mean(x * x, axis=-1, keepdims=True) + EPS) * g


def _rope_lanes(x, t_cos, t_sin):
    return x * t_cos + pltpu.roll(x, LANES // 2, axis=1) * t_sin


def _rope_rows(x1, x2, cos, sin):
    return x1 * cos - x2 * sin, x2 * cos + x1 * sin


def _proj_kernel(x_ref, g_attn_ref, w_tm_ref, w_fm_ref, g_qa_ref, w_qbt_ref, g_kva_ref, w_kn_ref, w_vt_ref,
                 gq_ref, gk_a_ref, gk_b_ref, gqs_ref, gks_ref,
                 mc_ref, ms_ref, sc_ref, ss_ref, tkc_ref, tks_ref, tsc_ref, tss_ref,
                 qt_ref, k_ref, vt_ref, qst_ref, ks_ref, vst_ref):
    tm = x_ref.shape[0]
    h = _rms(x_ref[...], g_attn_ref[...]).astype(BF16)
    p_tm = jnp.dot(h, w_tm_ref[...], preferred_element_type=F32)
    o_ckv = MLA_Q_LORA
    o_kpe = o_ckv + MLA_KV_LORA
    o_ks = o_kpe + LANES

    k_bound = MLA_QK ** 0.5 * jnp.maximum(jnp.max(jnp.abs(gk_a_ref[...]), axis=-1, keepdims=True),
                                          jnp.max(jnp.abs(gk_b_ref[...]), axis=-1, keepdims=True))
    shift_lane = (lax.broadcasted_iota(jnp.int32, (1, LANES), 1) == MLA_QK - LANES).astype(F32)
    first_row = lax.broadcasted_iota(jnp.int32, (MLA_PAD - MLA_QK, tm), 0) == 0

    cqn = _rms(p_tm[:, :o_ckv], g_qa_ref[...]).astype(BF16)
    q_t = lax.dot_general(w_qbt_ref[...], cqn, NT, preferred_element_type=F32)
    q_scale = MLA_QK ** -0.5 * LOG2E
    half = MLA_ROPE // 2
    for hd in range(MLA_HEADS):
        blk = q_t[hd * MLA_QK:(hd + 1) * MLA_QK]
        r = lax.rsqrt(jnp.sum(blk * blk, axis=0, keepdims=True) * (1.0 / MLA_QK) + EPS) * q_scale
        y = blk * gq_ref[...] * r
        shift = -(jnp.sqrt(jnp.sum(y * y, axis=0, keepdims=True)) * k_bound)
        o1, o2 = _rope_rows(y[MLA_NOPE:MLA_NOPE + half], y[MLA_NOPE + half:], mc_ref[...], ms_ref[...])
        qt_ref[hd, :MLA_NOPE, :] = y[:MLA_NOPE].astype(BF16)
        qt_ref[hd, MLA_NOPE:MLA_NOPE + half, :] = o1.astype(BF16)
        qt_ref[hd, MLA_NOPE + half:MLA_QK, :] = o2.astype(BF16)
        qt_ref[hd, MLA_QK:, :] = jnp.where(first_row, shift, 0.0).astype(BF16)

    ckvn = _rms(p_tm[:, o_ckv:o_kpe], g_kva_ref[...]).astype(BF16)
    kn_all = jnp.dot(ckvn, w_kn_ref[...], preferred_element_type=F32)
    kb = p_tm[:, o_kpe:o_ks]
    s_pe = 0.5 * jnp.sum(kb * kb, axis=-1, keepdims=True)
    kr = _rope_lanes(kb * gk_b_ref[...], tkc_ref[...], tks_ref[...])
    for hd in range(MLA_HEADS):
        kn = kn_all[:, hd * MLA_NOPE:(hd + 1) * MLA_NOPE]
        r = lax.rsqrt((jnp.sum(kn * kn, axis=-1, keepdims=True) + s_pe) * (1.0 / MLA_QK) + EPS)
        k_ref[hd, :, :LANES] = (kn * gk_a_ref[...] * r).astype(BF16)
        k_ref[hd, :, LANES:] = (kr * r + shift_lane).astype(BF16)
    v_t = lax.dot_general(w_vt_ref[...], ckvn, NT, preferred_element_type=F32)
    for hd in range(MLA_HEADS):
        vt_ref[hd] = v_t[hd * MLA_V:(hd + 1) * MLA_V].astype(BF16)

    p_fm = lax.dot_general(w_fm_ref[...], h, NT, preferred_element_type=F32)
    s_scale = SWA_HEAD_DIM ** -0.5 * LOG2E
    sh = SWA_HEAD_DIM // 2
    for hd in range(SWA_HEADS):
        blk = p_fm[hd * SWA_HEAD_DIM:(hd + 1) * SWA_HEAD_DIM]
        r = lax.rsqrt(jnp.mean(blk * blk, axis=0, keepdims=True) + EPS) * s_scale
        y = blk * gqs_ref[...] * r
        o1, o2 = _rope_rows(y[:sh], y[sh:], sc_ref[...], ss_ref[...])
        qst_ref[hd * SWA_HEAD_DIM:hd * SWA_HEAD_DIM + sh, :] = o1.astype(BF16)
        qst_ref[hd * SWA_HEAD_DIM + sh:(hd + 1) * SWA_HEAD_DIM, :] = o2.astype(BF16)
    vst_ref[...] = p_fm[SWA_HEADS * SWA_HEAD_DIM:].astype(BF16)
    for hd in range(SWA_KV_HEADS):
        xs = p_tm[:, o_ks + hd * SWA_HEAD_DIM: o_ks + (hd + 1) * SWA_HEAD_DIM]
        kn = _rope_lanes(_rms(xs, gks_ref[...]), tsc_ref[...], tss_ref[...])
        ks_ref[:, hd * SWA_HEAD_DIM:(hd + 1) * SWA_HEAD_DIM] = kn.astype(BF16)


def _rope_inputs(seq):
    fm_tabs, tm_tabs = [], []
    for dim in (MLA_ROPE, SWA_HEAD_DIM):
        half = dim // 2
        inv = 1.0 / (ROPE_THETA ** (jnp.arange(0, dim, 2, dtype=F32) / dim))
        fm_tabs += list(_cos_sin(inv[:, None], seq, axis=1))
        pad = jnp.zeros((LANES - dim,), F32)
        cos, sin = _cos_sin(jnp.concatenate([inv, inv, pad])[None, :], seq, axis=0)
        ones = jnp.ones((half,), F32)
        tm_tabs += [cos * jnp.concatenate([ones, ones, pad]), sin * jnp.concatenate([-ones, ones, pad])]
    return fm_tabs, tm_tabs


def _cos_sin(inv, seq, axis):
    blk = LANES
    assert seq % blk == 0
    shape = [1, 1]
    shape[axis] = -1
    pos_a = (jnp.arange(seq // blk, dtype=F32) * blk).reshape(shape)
    pos_b = jnp.arange(blk, dtype=F32).reshape(shape)
    ca, sa = jnp.cos(pos_a * inv), jnp.sin(pos_a * inv)
    cb, sb = jnp.cos(pos_b * inv), jnp.sin(pos_b * inv)
    ca, sa = jnp.expand_dims(ca, axis + 1), jnp.expand_dims(sa, axis + 1)
    cb, sb = jnp.expand_dims(cb, axis), jnp.expand_dims(sb, axis)
    out_shape = list(inv.shape)
    out_shape[axis] = seq
    return (ca * cb - sa * sb).reshape(out_shape), (sa * cb + ca * sb).reshape(out_shape)


def _proj(x, seq, wts, tabs, tm):
    n, d = x.shape
    tm = min(tm, seq)
    nt_seq = seq // tm
    row = lambda i: (i, 0)
    col = lambda i: (0, i)
    bcast = lambda g: jnp.broadcast_to(g.reshape(-1, 1), (g.size, tm))
    consts = [wts["g_attn"], wts["w_tm"], wts["w_fm"], wts["g_qa"], wts["w_qbt"], wts["g_kva"], wts["w_kn"],
              wts["w_vt"], bcast(wts["g_qn_mla"]), wts["gk_a"], wts["gk_b"], bcast(wts["g_q_swa"]), wts["g_k_swa"]]
    fm_tabs, tm_tabs = tabs
    in_specs = ([pl.BlockSpec((tm, d), row)] + [_resident(a.shape) for a in consts]
                + [pl.BlockSpec((t.shape[0], tm), lambda i: (0, i % nt_seq)) for t in fm_tabs]
                + [pl.BlockSpec((tm, LANES), lambda i: (i % nt_seq, 0))] * 4)
    n_qs = SWA_HEADS * SWA_HEAD_DIM
    n_ks = SWA_KV_HEADS * SWA_HEAD_DIM
    out_shape = (
        jax.ShapeDtypeStruct((MLA_HEADS, MLA_PAD, n), BF16),
        jax.ShapeDtypeStruct((MLA_HEADS, n, MLA_PAD), BF16),
        jax.ShapeDtypeStruct((MLA_HEADS, MLA_V, n), BF16),
        jax.ShapeDtypeStruct((n_qs, n), BF16),
        jax.ShapeDtypeStruct((n, n_ks), BF16),
        jax.ShapeDtypeStruct((n_ks, n), BF16),
    )
    out_specs = (
        pl.BlockSpec((MLA_HEADS, MLA_PAD, tm), lambda i: (0, 0, i)),
        pl.BlockSpec((MLA_HEADS, tm, MLA_PAD), lambda i: (0, i, 0)),
        pl.BlockSpec((MLA_HEADS, MLA_V, tm), lambda i: (0, 0, i)),
        pl.BlockSpec((n_qs, tm), col),
        pl.BlockSpec((tm, n_ks), row),
        pl.BlockSpec((n_ks, tm), col),
    )
    return pl.pallas_call(
        _proj_kernel, grid=(n // tm,), in_specs=in_specs, out_specs=out_specs,
        out_shape=out_shape, compiler_params=_params("parallel"), name="proj",
    )(x, *consts, *fm_tabs, *tm_tabs)


def _mla_kernel(qt_ref, k_ref, vt_ref, o_ref, *scratch, ck, track_max):
    j = pl.program_id(3)
    tk, tq = k_ref.shape[1], qt_ref.shape[2]
    if track_max:
        m_ref, l_ref, acc_ref = scratch
    else:
        l_ref, acc_ref, p_ref = scratch

    @pl.when(j == 0)
    def _():
        if track_max:
            m_ref[...] = jnp.full_like(m_ref, -jnp.inf)
        l_ref[...] = jnp.zeros_like(l_ref)
        acc_ref[...] = jnp.zeros_like(acc_ref)

    qt = qt_ref[0]
    if track_max:
        s = jnp.dot(k_ref[0], qt, preferred_element_type=F32)
        m_prev = m_ref[...]
        m_new = jnp.maximum(m_prev, jnp.max(s, axis=0, keepdims=True))
        alpha = jnp.exp2(m_prev - m_new)
        p = jnp.exp2(s - m_new)
        l_ref[...] = alpha * l_ref[...] + jnp.sum(p, axis=0, keepdims=True)
        acc_ref[...] = alpha * acc_ref[...] + jnp.dot(vt_ref[0], p.astype(BF16), preferred_element_type=F32)
        m_ref[...] = m_new
    else:
        l_part = l_ref[...]
        for c in range(tk // ck):
            s = jnp.dot(k_ref[0, c * ck:(c + 1) * ck, :], qt, preferred_element_type=F32)
            p = jnp.exp2(s)
            l_part = l_part + jnp.sum(p.reshape(ck // SUBLANES, SUBLANES, tq), axis=0)
            p_ref[c * ck:(c + 1) * ck, :] = p.astype(BF16)
        l_ref[...] = l_part
        acc_ref[...] += jnp.dot(vt_ref[0], p_ref[...], preferred_element_type=F32)

    @pl.when(j == pl.num_programs(3) - 1)
    def _():
        l = jnp.sum(l_ref[...], axis=0, keepdims=True)
        o_ref[...] = (acc_ref[...] / l).T.astype(o_ref.dtype)


def _mla_attention(qt, k, vt, batch, seq, tq, tk, ck, track_max):
    n = batch * seq
    tq, tk = min(tq, seq), min(tk, seq)
    nq, nk = seq // tq, seq // tk
    if track_max:
        scratch = [pltpu.VMEM((1, tq), F32), pltpu.VMEM((1, tq), F32), pltpu.VMEM((MLA_V, tq), F32)]
    else:
        scratch = [pltpu.VMEM((SUBLANES, tq), F32), pltpu.VMEM((MLA_V, tq), F32), pltpu.VMEM((tk, tq), BF16)]
    return pl.pallas_call(
        functools.partial(_mla_kernel, ck=min(ck, tk), track_max=track_max),
        grid=(batch, MLA_HEADS, nq, nk),
        in_specs=[
            pl.BlockSpec((1, MLA_PAD, tq), lambda b, h, i, j: (h, 0, b * nq + i)),
            pl.BlockSpec((1, tk, MLA_PAD), lambda b, h, i, j: (h, b * nk + j, 0)),
            pl.BlockSpec((1, MLA_V, tk), lambda b, h, i, j: (h, 0, b * nk + j)),
        ],
        out_specs=pl.BlockSpec((tq, MLA_V), lambda b, h, i, j: (b * nq + i, h)),
        out_shape=jax.ShapeDtypeStruct((n, MLA_HEADS * MLA_V), BF16),
        scratch_shapes=scratch,
        compiler_params=_params("parallel", "parallel", "parallel", "arbitrary"),
        name="mla_attn_online" if track_max else "mla_attn",
    )(qt, k, vt)


def _swa_kernel(sink_ref, qt_ref, kp_ref, km_ref, kn_ref, vp_ref, vm_ref, vn_ref, o_ref, kwin_ref, vwin_ref,
                *, nq, sub):
    g = pl.program_id(1)
    i = pl.program_id(2)
    tq = qt_ref.shape[1]
    kwin_ref[:WINDOW] = kp_ref[...]
    kwin_ref[WINDOW:WINDOW + tq] = km_ref[...]
    kwin_ref[WINDOW + tq:] = kn_ref[...]
    vwin_ref[:, :WINDOW] = vp_ref[...]
    vwin_ref[:, WINDOW:WINDOW + tq] = vm_ref[...]
    vwin_ref[:, WINDOW + tq:] = vn_ref[...]
    heads = range(SWA_GROUP)
    ws = sub + 2 * WINDOW
    sink = jnp.concatenate(
        [jnp.full((1, sub), sink_ref[g * SWA_GROUP + hd] * LOG2E, F32) for hd in heads], axis=1)
    key = lax.broadcasted_iota(jnp.int32, (ws, SWA_GROUP * sub), 0)
    qry = lax.broadcasted_iota(jnp.int32, (ws, SWA_GROUP * sub), 1) & (sub - 1)
    band = (key >= qry) & (key <= qry + 2 * WINDOW)
    no_prev = jnp.where(i > 0, 0.0, -jnp.inf)
    no_next = jnp.where(i < nq - 1, 0.0, -jnp.inf)
    for t in range(tq // sub):
        q_all = jnp.concatenate(
            [qt_ref[hd * SWA_HEAD_DIM:(hd + 1) * SWA_HEAD_DIM, t * sub:(t + 1) * sub] for hd in heads], axis=1)
        s = jnp.dot(kwin_ref[t * sub:t * sub + ws, :], q_all, preferred_element_type=F32)
        s = jnp.where(band, s, -jnp.inf)
        if t == 0:
            s = jnp.concatenate([s[:WINDOW] + no_prev, s[WINDOW:]], axis=0)
        if t == tq // sub - 1:
            s = jnp.concatenate([s[:ws - WINDOW], s[ws - WINDOW:] + no_next], axis=0)
        m = jnp.maximum(jnp.max(s, axis=0, keepdims=True), sink)
        p = jnp.exp2(s - m)
        denom = jnp.sum(p, axis=0, keepdims=True) + jnp.exp2(sink - m)
        o_t = jnp.dot(vwin_ref[:, t * sub:t * sub + ws], p.astype(BF16), preferred_element_type=F32) / denom
        for hd in heads:
            o_ref[t * sub:(t + 1) * sub, hd * SWA_HEAD_DIM:(hd + 1) * SWA_HEAD_DIM] = (
                o_t[:, hd * sub:(hd + 1) * sub].T.astype(o_ref.dtype))


def _swa_attention(sink, qt, k, vt, batch, seq, tq, sub):
    n = batch * seq
    tq = min(tq, seq)
    sub = min(sub, tq)
    assert sub & (sub - 1) == 0 and tq % sub == 0, "query index within a sub-tile is taken with a bit mask"
    nq = seq // tq
    r = tq // WINDOW
    nb = seq // WINDOW
    gw = SWA_GROUP * SWA_HEAD_DIM
    prev = lambda b, i: b * nb + jnp.maximum(i * r - 1, 0)
    nxt = lambda b, i: b * nb + jnp.minimum((i + 1) * r, nb - 1)
    edge = (WINDOW, SWA_HEAD_DIM)
    return pl.pallas_call(
        functools.partial(_swa_kernel, nq=nq, sub=sub),
        grid=(batch, SWA_KV_HEADS, nq),
        in_specs=[
            pl.BlockSpec(memory_space=pltpu.SMEM),
            pl.BlockSpec((gw, tq), lambda b, g, i: (g, b * nq + i)),
            pl.BlockSpec(edge, lambda b, g, i: (prev(b, i), g)),
            pl.BlockSpec((tq, SWA_HEAD_DIM), lambda b, g, i: (b * nq + i, g)),
            pl.BlockSpec(edge, lambda b, g, i: (nxt(b, i), g)),
            pl.BlockSpec(edge, lambda b, g, i: (g, prev(b, i))),
            pl.BlockSpec((SWA_HEAD_DIM, tq), lambda b, g, i: (g, b * nq + i)),
            pl.BlockSpec(edge, lambda b, g, i: (g, nxt(b, i))),
        ],
        out_specs=pl.BlockSpec((tq, gw), lambda b, g, i: (b * nq + i, g)),
        out_shape=jax.ShapeDtypeStruct((n, SWA_HEADS * SWA_HEAD_DIM), BF16),
        scratch_shapes=[pltpu.VMEM((tq + 2 * WINDOW, SWA_HEAD_DIM), BF16),
                        pltpu.VMEM((SWA_HEAD_DIM, tq + 2 * WINDOW), BF16)],
        compiler_params=_params("parallel", "parallel", "parallel"),
        name="swa_attn",
    )(sink, qt, k, k, k, vt, vt, vt)


def _out_proj_kernel(x_ref, oa_ref, ob_ref, ga_ref, gb_ref, wa_ref, wb_ref, g_mlp_ref, x1_ref, hm_ref):
    oa = _rms(oa_ref[...].astype(F32), ga_ref[...]).astype(BF16)
    ob = _rms(ob_ref[...].astype(F32), gb_ref[...]).astype(BF16)
    x1 = (x_ref[...] + jnp.dot(oa, wa_ref[...], preferred_element_type=F32)
          + jnp.dot(ob, wb_ref[...], preferred_element_type=F32))
    x1_ref[...] = x1.astype(x1_ref.dtype)
    hm_ref[...] = _rms(x1, g_mlp_ref[...]).astype(BF16)


def _out_proj(x, oa, ob, wts, tm):
    n, d = x.shape
    tm = min(tm, n)
    row = lambda i: (i, 0)
    consts = [wts[k] for k in ("g_out_mla", "g_out_swa", "w_out_a", "w_out_b", "g_mlp")]
    return pl.pallas_call(
        _out_proj_kernel, grid=(n // tm,),
        in_specs=[pl.BlockSpec((tm, d), row), pl.BlockSpec((tm, oa.shape[1]), row),
                  pl.BlockSpec((tm, ob.shape[1]), row)] + [_resident(a.shape) for a in consts],
        out_specs=(pl.BlockSpec((tm, d), row), pl.BlockSpec((tm, d), row)),
        out_shape=(jax.ShapeDtypeStruct((n, d), STREAM), jax.ShapeDtypeStruct((n, d), BF16)),
        compiler_params=_params("parallel"), name="out_proj",
    )(x, oa, ob, *consts)


def _mlp_kernel(x1_ref, hm_ref, wu_ref, wd_ref, o_ref, acc_ref):
    j = pl.program_id(1)

    @pl.when(j == 0)
    def _():
        acc_ref[...] = x1_ref[...].astype(F32)

    up = jnp.dot(hm_ref[...], wu_ref[...], preferred_element_type=F32)
    act = jnp.square(jnp.maximum(up, 0.0)).astype(BF16)
    acc_ref[...] += jnp.dot(act, wd_ref[...], preferred_element_type=F32)

    @pl.when(j == pl.num_programs(1) - 1)
    def _():
        o_ref[...] = acc_ref[...].astype(o_ref.dtype)


def _mlp(x1, hm, w_up, w_down, tm, tf):
    n, d = x1.shape
    f = w_up.shape[1]
    tm, tf = min(tm, n), min(tf, f)
    row = lambda i, j: (i, 0)
    early = lambda i, j: (jnp.minimum(i + jnp.where(j >= 2, 1, 0), n // tm - 1), 0)
    return pl.pallas_call(
        _mlp_kernel, grid=(n // tm, f // tf),
        in_specs=[pl.BlockSpec((tm, d), early), pl.BlockSpec((tm, d), row),
                  pl.BlockSpec((d, tf), lambda i, j: (0, j)), pl.BlockSpec((tf, d), lambda i, j: (j, 0))],
        out_specs=pl.BlockSpec((tm, d), row),
        out_shape=jax.ShapeDtypeStruct((n, d), STREAM),
        scratch_shapes=[pltpu.VMEM((tm, d), F32)],
        compiler_params=_params("parallel", "arbitrary"), name="mlp",
    )(x1, hm, w_up, w_down)


def _ple_kernel(x_ref, p_ref, g_ref, wg_ref, wp_ref, o_ref):
    x = x_ref[...].astype(F32)
    z = jnp.dot(_rms(x, g_ref[...]).astype(BF16), wg_ref[...], preferred_element_type=F32)
    gate = 1.0 / (1.0 + jnp.exp(-z))
    e = jnp.dot(p_ref[...].astype(BF16), wp_ref[...], preferred_element_type=F32)
    o_ref[...] = x + e * gate


def _ple(x2, p, wts, tm):
    n, d = x2.shape
    tm = min(tm, n)
    row = lambda i: (i, 0)
    consts = [wts[k] for k in ("g_ple", "w_gate", "w_ple")]
    return pl.pallas_call(
        _ple_kernel, grid=(n // tm,),
        in_specs=[pl.BlockSpec((tm, d), row), pl.BlockSpec((tm, p.shape[1]), row)]
                 + [_resident(a.shape) for a in consts],
        out_specs=pl.BlockSpec((tm, d), row),
        out_shape=jax.ShapeDtypeStruct((n, d), F32),
        compiler_params=_params("parallel"), name="ple",
    )(x2, p, *consts)


def _swap_halves(a):
    half = a.shape[-1] // 2
    return jnp.concatenate([a[..., half:], a[..., :half]], axis=-1)


def _prepare_weights(g_attn, w_in, g_qa, w_qb, g_kva, w_kvb, g_qn_mla, g_kn_mla, g_q_swa, g_k_swa,
                     g_out_mla, g_out_swa, w_out, g_mlp, w_up, w_down, g_ple, w_ple, w_gate):
    row = lambda g: g.reshape(1, -1)
    o_kpe = MLA_Q_LORA + MLA_KV_LORA
    o_qs = o_kpe + MLA_ROPE
    o_ks = o_qs + SWA_HEADS * SWA_HEAD_DIM
    o_vs = o_ks + SWA_KV_HEADS * SWA_HEAD_DIM
    kpe = w_in[:, o_kpe:o_qs]
    w_tm = jnp.concatenate([w_in[:, :o_kpe], kpe, _swap_halves(kpe), w_in[:, o_ks:o_vs]], axis=1)
    w_fm = jnp.concatenate([w_in[:, o_qs:o_ks], w_in[:, o_vs:]], axis=1).T
    w_kv = w_kvb.reshape(MLA_KV_LORA, MLA_HEADS, MLA_NOPE + MLA_V)
    w_kn = w_kv[..., :MLA_NOPE].reshape(MLA_KV_LORA, MLA_HEADS * MLA_NOPE)
    w_vt = w_kv[..., MLA_NOPE:].reshape(MLA_KV_LORA, MLA_HEADS * MLA_V).T
    g_rope = g_kn_mla[MLA_NOPE:]
    n_a = MLA_HEADS * MLA_V
    logit_span = 2.0 * MLA_QK ** 0.5 * LOG2E * jnp.max(jnp.abs(g_qn_mla)) * jnp.max(jnp.abs(g_kn_mla))
    return {
        "mla_logit_span": logit_span,
        "g_attn": row(g_attn), "w_tm": w_tm.astype(BF16), "w_fm": w_fm.astype(BF16),
        "g_qa": row(g_qa), "w_qbt": w_qb.T.astype(BF16),
        "g_kva": row(g_kva), "w_kn": w_kn.astype(BF16), "w_vt": w_vt.astype(BF16),
        "g_qn_mla": g_qn_mla, "gk_a": row(g_kn_mla[:MLA_NOPE]),
        "gk_b": row(jnp.concatenate([g_rope, _swap_halves(g_rope)])),
        "g_q_swa": g_q_swa, "g_k_swa": row(g_k_swa),
        "g_out_mla": row(g_out_mla), "g_out_swa": row(g_out_swa),
        "w_out_a": w_out[:n_a].astype(BF16), "w_out_b": w_out[n_a:].astype(BF16),
        "g_mlp": row(g_mlp), "w_up": w_up.astype(BF16), "w_down": w_down.astype(BF16),
        "g_ple": row(g_ple), "w_gate": w_gate.astype(BF16), "w_ple": w_ple.astype(BF16),
    }


def _layer(x, p, sink, wts, tabs):
    batch, seq, d = x.shape
    n = batch * seq
    xf = x.reshape(n, d)
    qt, k, vt, qst, ks, vst = _proj(xf, seq, wts, tabs, tm=512)
    o_a = lax.cond(
        wts["mla_logit_span"] <= SAFE_LOG2_RANGE,
        functools.partial(_mla_attention, batch=batch, seq=seq, tq=1024, tk=4096, ck=256, track_max=False),
        functools.partial(_mla_attention, batch=batch, seq=seq, tq=512, tk=512, ck=512, track_max=True),
        qt, k, vt)
    o_b = _swa_attention(sink, qst, ks, vst, batch, seq, tq=512, sub=256)
    x1, hm = _out_proj(xf, o_a, o_b, wts, tm=512)
    x2 = _mlp(x1, hm, wts["w_up"], wts["w_down"], tm=1024, tf=1024)
    y = _ple(x2, p.reshape(n, -1), wts, tm=512)
    return y.reshape(batch, seq, d)


def kernel(x_prompt, x_sample, p_prompt, p_sample, g_attn, w_in, g_qa, w_qb, g_kva, w_kvb, g_qn_mla, g_kn_mla,
           g_q_swa, g_k_swa, sink, g_out_mla, g_out_swa, w_out, g_mlp, w_up, w_down, g_ple, w_ple, w_gate):
    y_prompt, y_sample = x_prompt, x_sample
    tabs = _rope_inputs(max(x_prompt.shape[1], x_sample.shape[1]))
    for l in range(g_attn.shape[0]):
        wts = _prepare_weights(g_attn[l], w_in[l], g_qa[l], w_qb[l], g_kva[l], w_kvb[l], g_qn_mla[l],
                               g_kn_mla[l], g_q_swa[l], g_k_swa[l], g_out_mla[l], g_out_swa[l], w_out[l],
                               g_mlp[l], w_up[l], w_down[l], g_ple[l], w_ple[l], w_gate[l])
        y_prompt = _layer(y_prompt, p_prompt[l], sink[l], wts, tabs)
        y_sample = _layer(y_sample, p_sample[l], sink[l], wts, tabs)
    return (y_prompt, y_sample)
```

```python
import functools
import math

import jax
import jax.numpy as jnp
from jax import lax
from jax.experimental import pallas as pl
from jax.experimental.pallas import tpu as pltpu

EPS = 1e-6
ROPE_THETA = 10000.0
WINDOW = 128
MLA_HEADS = 8
MLA_Q_LORA = 512
MLA_KV_LORA = 256
MLA_NOPE = 128
MLA_ROPE = 64
MLA_V = 128
MLA_QK = MLA_NOPE + MLA_ROPE
SWA_HEADS = 8
SWA_KV_HEADS = 2
SWA_GROUP = SWA_HEADS // SWA_KV_HEADS
SWA_HEAD_DIM = 128
LANES = 128
SUBLANES = 8
MLA_PAD = 2 * LANES
SAFE_LOG2_RANGE = 100.0
LOG2E = math.log2(math.e)
VMEM_LIMIT = 56 * 1024 * 1024

BF16 = jnp.bfloat16
F32 = jnp.float32
STREAM = jnp.bfloat16
NT = (((1,), (1,)), ((), ()))


def _params(*sem):
    return pltpu.CompilerParams(dimension_semantics=sem, vmem_limit_bytes=VMEM_LIMIT)


def _resident(shape):
    nd = len(shape)
    return pl.BlockSpec(shape, lambda *_: (0,) * nd, pipeline_mode=pl.Buffered(1))


def _rms(x, g):
    return x * lax.rsqrt(jnp.mean(x * x, axis=-1, keepdims=True) + EPS) * g


def _rope_lanes(x, t_cos, t_sin):
    return x * t_cos + pltpu.roll(x, LANES // 2, axis=1) * t_sin


def _rope_rows(x1, x2, cos, sin):
    return x1 * cos - x2 * sin, x2 * cos + x1 * sin


def _proj_kernel(x_ref, g_attn_ref, w_tm_ref, w_fm_ref, g_qa_ref, w_qbt_ref, g_kva_ref, w_kn_ref, w_vt_ref,
                 gq_ref, gk_a_ref, gk_b_ref, gqs_ref, gks_ref,
                 mc_ref, ms_ref, sc_ref, ss_ref, tkc_ref, tks_ref, tsc_ref, tss_ref,
                 qt_ref, k_ref, vt_ref, qst_ref, ks_ref, vst_ref):
    tm = x_ref.shape[0]
    h = _rms(x_ref[...], g_attn_ref[...]).astype(BF16)
    p_tm = jnp.dot(h, w_tm_ref[...], preferred_element_type=F32)
    o_ckv = MLA_Q_LORA
    o_kpe = o_ckv + MLA_KV_LORA
    o_ks = o_kpe + LANES

    k_bound = MLA_QK ** 0.5 * jnp.maximum(jnp.max(jnp.abs(gk_a_ref[...]), axis=-1, keepdims=True),
                                          jnp.max(jnp.abs(gk_b_ref[...]), axis=-1, keepdims=True))
    shift_lane = (lax.broadcasted_iota(jnp.int32, (1, LANES), 1) == MLA_QK - LANES).astype(F32)
    first_row = lax.broadcasted_iota(jnp.int32, (MLA_PAD - MLA_QK, tm), 0) == 0

    cqn = _rms(p_tm[:, :o_ckv], g_qa_ref[...]).astype(BF16)
    q_t = lax.dot_general(w_qbt_ref[...], cqn, NT, preferred_element_type=F32)
    q_scale = MLA_QK ** -0.5 * LOG2E
    half = MLA_ROPE // 2
    for hd in range(MLA_HEADS):
        blk = q_t[hd * MLA_QK:(hd + 1) * MLA_QK]
        r = lax.rsqrt(jnp.sum(blk * blk, axis=0, keepdims=True) * (1.0 / MLA_QK) + EPS) * q_scale
        y = blk * gq_ref[...] * r
        shift = -(jnp.sqrt(jnp.sum(y * y, axis=0, keepdims=True)) * k_bound)
        o1, o2 = _rope_rows(y[MLA_NOPE:MLA_NOPE + half], y[MLA_NOPE + half:], mc_ref[...], ms_ref[...])
        qt_ref[hd, :MLA_NOPE, :] = y[:MLA_NOPE].astype(BF16)
        qt_ref[hd, MLA_NOPE:MLA_NOPE + half, :] = o1.astype(BF16)
        qt_ref[hd, MLA_NOPE + half:MLA_QK, :] = o2.astype(BF16)
        qt_ref[hd, MLA_QK:, :] = jnp.where(first_row, shift, 0.0).astype(BF16)

    ckvn = _rms(p_tm[:, o_ckv:o_kpe], g_kva_ref[...]).astype(BF16)
    kn_all = jnp.dot(ckvn, w_kn_ref[...], preferred_element_type=F32)
    kb = p_tm[:, o_kpe:o_ks]
    s_pe = 0.5 * jnp.sum(kb * kb, axis=-1, keepdims=True)
    kr = _rope_lanes(kb * gk_b_ref[...], tkc_ref[...], tks_ref[...])
    for hd in range(MLA_HEADS):
        kn = kn_all[:, hd * MLA_NOPE:(hd + 1) * MLA_NOPE]
        r = lax.rsqrt((jnp.sum(kn * kn, axis=-1, keepdims=True) + s_pe) * (1.0 / MLA_QK) + EPS)
        k_ref[hd, :, :LANES] = (kn * gk_a_ref[...] * r).astype(BF16)
        k_ref[hd, :, LANES:] = (kr * r + shift_lane).astype(BF16)
    v_t = lax.dot_general(w_vt_ref[...], ckvn, NT, preferred_element_type=F32)
    for hd in range(MLA_HEADS):
        vt_ref[hd] = v_t[hd * MLA_V:(hd + 1) * MLA_V].astype(BF16)

    p_fm = lax.dot_general(w_fm_ref[...], h, NT, preferred_element_type=F32)
    s_scale = SWA_HEAD_DIM ** -0.5 * LOG2E
    sh = SWA_HEAD_DIM // 2
    for hd in range(SWA_HEADS):
        blk = p_fm[hd * SWA_HEAD_DIM:(hd + 1) * SWA_HEAD_DIM]
        r = lax.rsqrt(jnp.mean(blk * blk, axis=0, keepdims=True) + EPS) * s_scale
        y = blk * gqs_ref[...] * r
        o1, o2 = _rope_rows(y[:sh], y[sh:], sc_ref[...], ss_ref[...])
        qst_ref[hd * SWA_HEAD_DIM:hd * SWA_HEAD_DIM + sh, :] = o1.astype(BF16)
        qst_ref[hd * SWA_HEAD_DIM + sh:(hd + 1) * SWA_HEAD_DIM, :] = o2.astype(BF16)
    vst_ref[...] = p_fm[SWA_HEADS * SWA_HEAD_DIM:].astype(BF16)
    for hd in range(SWA_KV_HEADS):
        xs = p_tm[:, o_ks + hd * SWA_HEAD_DIM: o_ks + (hd + 1) * SWA_HEAD_DIM]
        kn = _rope_lanes(_rms(xs, gks_ref[...]), tsc_ref[...], tss_ref[...])
        ks_ref[:, hd * SWA_HEAD_DIM:(hd + 1) * SWA_HEAD_DIM] = kn.astype(BF16)


def _rope_inputs(seq):
    fm_tabs, tm_tabs = [], []
    for dim in (MLA_ROPE, SWA_HEAD_DIM):
        half = dim // 2
        inv = 1.0 / (ROPE_THETA ** (jnp.arange(0, dim, 2, dtype=F32) / dim))
        fm_tabs += list(_cos_sin(inv[:, None], seq, axis=1))
        pad = jnp.zeros((LANES - dim,), F32)
        cos, sin = _cos_sin(jnp.concatenate([inv, inv, pad])[None, :], seq, axis=0)
        ones = jnp.ones((half,), F32)
        tm_tabs += [cos * jnp.concatenate([ones, ones, pad]), sin * jnp.concatenate([-ones, ones, pad])]
    return fm_tabs, tm_tabs


def _cos_sin(inv, seq, axis):
    blk = LANES
    assert seq % blk == 0
    shape = [1, 1]
    shape[axis] = -1
    pos_a = (jnp.arange(seq // blk, dtype=F32) * blk).reshape(shape)
    pos_b = jnp.arange(blk, dtype=F32).reshape(shape)
    ca, sa = jnp.cos(pos_a * inv), jnp.sin(pos_a * inv)
    cb, sb = jnp.cos(pos_b * inv), jnp.sin(pos_b * inv)
    ca, sa = jnp.expand_dims(ca, axis + 1), jnp.expand_dims(sa, axis + 1)
    cb, sb = jnp.expand_dims(cb, axis), jnp.expand_dims(sb, axis)
    out_shape = list(inv.shape)
    out_shape[axis] = seq
    return (ca * cb - sa * sb).reshape(out_shape), (sa * cb + ca * sb).reshape(out_shape)


def _proj(x, seq, wts, tabs, tm):
    n, d = x.shape
    tm = min(tm, seq)
    nt_seq = seq // tm
    row = lambda i: (i, 0)
    col = lambda i: (0, i)
    bcast = lambda g: jnp.broadcast_to(g.reshape(-1, 1), (g.size, tm))
    consts = [wts["g_attn"], wts["w_tm"], wts["w_fm"], wts["g_qa"], wts["w_qbt"], wts["g_kva"], wts["w_kn"],
              wts["w_vt"], bcast(wts["g_qn_mla"]), wts["gk_a"], wts["gk_b"], bcast(wts["g_q_swa"]), wts["g_k_swa"]]
    fm_tabs, tm_tabs = tabs
    in_specs = ([pl.BlockSpec((tm, d), row)] + [_resident(a.shape) for a in consts]
                + [pl.BlockSpec((t.shape[0], tm), lambda i: (0, i % nt_seq)) for t in fm_tabs]
                + [pl.BlockSpec((tm, LANES), lambda i: (i % nt_seq, 0))] * 4)
    n_qs = SWA_HEADS * SWA_HEAD_DIM
    n_ks = SWA_KV_HEADS * SWA_HEAD_DIM
    out_shape = (
        jax.ShapeDtypeStruct((MLA_HEADS, MLA_PAD, n), BF16),
        jax.ShapeDtypeStruct((MLA_HEADS, n, MLA_PAD), BF16),
        jax.ShapeDtypeStruct((MLA_HEADS, MLA_V, n), BF16),
        jax.ShapeDtypeStruct((n_qs, n), BF16),
        jax.ShapeDtypeStruct((n, n_ks), BF16),
        jax.ShapeDtypeStruct((n_ks, n), BF16),
    )
    out_specs = (
        pl.BlockSpec((MLA_HEADS, MLA_PAD, tm), lambda i: (0, 0, i)),
        pl.BlockSpec((MLA_HEADS, tm, MLA_PAD), lambda i: (0, i, 0)),
        pl.BlockSpec((MLA_HEADS, MLA_V, tm), lambda i: (0, 0, i)),
        pl.BlockSpec((n_qs, tm), col),
        pl.BlockSpec((tm, n_ks), row),
        pl.BlockSpec((n_ks, tm), col),
    )
    return pl.pallas_call(
        _proj_kernel, grid=(n // tm,), in_specs=in_specs, out_specs=out_specs,
        out_shape=out_shape, compiler_params=_params("parallel"), name="proj",
    )(x, *consts, *fm_tabs, *tm_tabs)


def _mla_kernel(qt_ref, k_ref, vt_ref, o_ref, *scratch, ck, track_max):
    j = pl.program_id(3)
    tk, tq = k_ref.shape[1], qt_ref.shape[2]
    if track_max:
        m_ref, l_ref, acc_ref = scratch
    else:
        l_ref, acc_ref, p_ref = scratch

    @pl.when(j == 0)
    def _():
        if track_max:
            m_ref[...] = jnp.full_like(m_ref, -jnp.inf)
        l_ref[...] = jnp.zeros_like(l_ref)
        acc_ref[...] = jnp.zeros_like(acc_ref)

    qt = qt_ref[0]
    if track_max:
        s = jnp.dot(k_ref[0], qt, preferred_element_type=F32)
        m_prev = m_ref[...]
        m_new = jnp.maximum(m_prev, jnp.max(s, axis=0, keepdims=True))
        alpha = jnp.exp2(m_prev - m_new)
        p = jnp.exp2(s - m_new)
        l_ref[...] = alpha * l_ref[...] + jnp.sum(p, axis=0, keepdims=True)
        acc_ref[...] = alpha * acc_ref[...] + jnp.dot(vt_ref[0], p.astype(BF16), preferred_element_type=F32)
        m_ref[...] = m_new
    else:
        l_part = l_ref[...]
        for c in range(tk // ck):
            s = jnp.dot(k_ref[0, c * ck:(c + 1) * ck, :], qt, preferred_element_type=F32)
            p = jnp.exp2(s)
            l_part = l_part + jnp.sum(p.reshape(ck // SUBLANES, SUBLANES, tq), axis=0)
            p_ref[c * ck:(c + 1) * ck, :] = p.astype(BF16)
        l_ref[...] = l_part
        acc_ref[...] += jnp.dot(vt_ref[0], p_ref[...], preferred_element_type=F32)

    @pl.when(j == pl.num_programs(3) - 1)
    def _():
        l = jnp.sum(l_ref[...], axis=0, keepdims=True)
        o_ref[...] = (acc_ref[...] / l).T.astype(o_ref.dtype)


def _mla_attention(qt, k, vt, batch, seq, tq, tk, ck, track_max):
    n = batch * seq
    tq, tk = min(tq, seq), min(tk, seq)
    nq, nk = seq // tq, seq // tk
    if track_max:
        scratch = [pltpu.VMEM((1, tq), F32), pltpu.VMEM((1, tq), F32), pltpu.VMEM((MLA_V, tq), F32)]
    else:
        scratch = [pltpu.VMEM((SUBLANES, tq), F32), pltpu.VMEM((MLA_V, tq), F32), pltpu.VMEM((tk, tq), BF16)]
    return pl.pallas_call(
        functools.partial(_mla_kernel, ck=min(ck, tk), track_max=track_max),
        grid=(batch, MLA_HEADS, nq, nk),
        in_specs=[
            pl.BlockSpec((1, MLA_PAD, tq), lambda b, h, i, j: (h, 0, b * nq + i)),
            pl.BlockSpec((1, tk, MLA_PAD), lambda b, h, i, j: (h, b * nk + j, 0)),
            pl.BlockSpec((1, MLA_V, tk), lambda b, h, i, j: (h, 0, b * nk + j)),
        ],
        out_specs=pl.BlockSpec((tq, MLA_V), lambda b, h, i, j: (b * nq + i, h)),
        out_shape=jax.ShapeDtypeStruct((n, MLA_HEADS * MLA_V), BF16),
        scratch_shapes=scratch,
        compiler_params=_params("parallel", "parallel", "parallel", "arbitrary"),
        name="mla_attn_online" if track_max else "mla_attn",
    )(qt, k, vt)


def _swa_kernel(sink_ref, qt_ref, kp_ref, km_ref, kn_ref, vp_ref, vm_ref, vn_ref, o_ref, kwin_ref, vwin_ref,
                *, nq, sub):
    g = pl.program_id(1)
    i = pl.program_id(2)
    tq = qt_ref.shape[1]
    kwin_ref[:WINDOW] = kp_ref[...]
    kwin_ref[WINDOW:WINDOW + tq] = km_ref[...]
    kwin_ref[WINDOW + tq:] = kn_ref[...]
    vwin_ref[:, :WINDOW] = vp_ref[...]
    vwin_ref[:, WINDOW:WINDOW + tq] = vm_ref[...]
    vwin_ref[:, WINDOW + tq:] = vn_ref[...]
    heads = range(SWA_GROUP)
    ws = sub + 2 * WINDOW
    sink = jnp.concatenate(
        [jnp.full((1, sub), sink_ref[g * SWA_GROUP + hd] * LOG2E, F32) for hd in heads], axis=1)
    key = lax.broadcasted_iota(jnp.int32, (ws, SWA_GROUP * sub), 0)
    qry = lax.broadcasted_iota(jnp.int32, (ws, SWA_GROUP * sub), 1) & (sub - 1)
    band = (key >= qry) & (key <= qry + 2 * WINDOW)
    no_prev = jnp.where(i > 0, 0.0, -jnp.inf)
    no_next = jnp.where(i < nq - 1, 0.0, -jnp.inf)
    for t in range(tq // sub):
        q_all = jnp.concatenate(
            [qt_ref[hd * SWA_HEAD_DIM:(hd + 1) * SWA_HEAD_DIM, t * sub:(t + 1) * sub] for hd in heads], axis=1)
        s = jnp.dot(kwin_ref[t * sub:t * sub + ws, :], q_all, preferred_element_type=F32)
        s = jnp.where(band, s, -jnp.inf)
        if t == 0:
            s = jnp.concatenate([s[:WINDOW] + no_prev, s[WINDOW:]], axis=0)
        if t == tq // sub - 1:
            s = jnp.concatenate([s[:ws - WINDOW], s[ws - WINDOW:] + no_next], axis=0)
        m = jnp.maximum(jnp.max(s, axis=0, keepdims=True), sink)
        p = jnp.exp2(s - m)
        denom = jnp.sum(p, axis=0, keepdims=True) + jnp.exp2(sink - m)
        o_t = jnp.dot(vwin_ref[:, t * sub:t * sub + ws], p.astype(BF16), preferred_element_type=F32) / denom
        for hd in heads:
            o_ref[t * sub:(t + 1) * sub, hd * SWA_HEAD_DIM:(hd + 1) * SWA_HEAD_DIM] = (
                o_t[:, hd * sub:(hd + 1) * sub].T.astype(o_ref.dtype))


def _swa_attention(sink, qt, k, vt, batch, seq, tq, sub):
    n = batch * seq
    tq = min(tq, seq)
    sub = min(sub, tq)
    assert sub & (sub - 1) == 0 and tq % sub == 0, "query index within a sub-tile is taken with a bit mask"
    nq = seq // tq
    r = tq // WINDOW
    nb = seq // WINDOW
    gw = SWA_GROUP * SWA_HEAD_DIM
    prev = lambda b, i: b * nb + jnp.maximum(i * r - 1, 0)
    nxt = lambda b, i: b * nb + jnp.minimum((i + 1) * r, nb - 1)
    edge = (WINDOW, SWA_HEAD_DIM)
    return pl.pallas_call(
        functools.partial(_swa_kernel, nq=nq, sub=sub),
        grid=(batch, SWA_KV_HEADS, nq),
        in_specs=[
            pl.BlockSpec(memory_space=pltpu.SMEM),
            pl.BlockSpec((gw, tq), lambda b, g, i: (g, b * nq + i)),
            pl.BlockSpec(edge, lambda b, g, i: (prev(b, i), g)),
            pl.BlockSpec((tq, SWA_HEAD_DIM), lambda b, g, i: (b * nq + i, g)),
            pl.BlockSpec(edge, lambda b, g, i: (nxt(b, i), g)),
            pl.BlockSpec(edge, lambda b, g, i: (g, prev(b, i))),
            pl.BlockSpec((SWA_HEAD_DIM, tq), lambda b, g, i: (g, b * nq + i)),
            pl.BlockSpec(edge, lambda b, g, i: (g, nxt(b, i))),
        ],
        out_specs=pl.BlockSpec((tq, gw), lambda b, g, i: (b * nq + i, g)),
        out_shape=jax.ShapeDtypeStruct((n, SWA_HEADS * SWA_HEAD_DIM), BF16),
        scratch_shapes=[pltpu.VMEM((tq + 2 * WINDOW, SWA_HEAD_DIM), BF16),
                        pltpu.VMEM((SWA_HEAD_DIM, tq + 2 * WINDOW), BF16)],
        compiler_params=_params("parallel", "parallel", "parallel"),
        name="swa_attn",
    )(sink, qt, k, k, k, vt, vt, vt)


def _out_proj_kernel(x_ref, oa_ref, ob_ref, ga_ref, gb_ref, wa_ref, wb_ref, g_mlp_ref, x1_ref, hm_ref):
    oa = _rms(oa_ref[...].astype(F32), ga_ref[...]).astype(BF16)
    ob = _rms(ob_ref[...].astype(F32), gb_ref[...]).astype(BF16)
    x1 = (x_ref[...] + jnp.dot(oa, wa_ref[...], preferred_element_type=F32)
          + jnp.dot(ob, wb_ref[...], preferred_element_type=F32))
    x1_ref[...] = x1.astype(x1_ref.dtype)
    hm_ref[...] = _rms(x1, g_mlp_ref[...]).astype(BF16)


def _out_proj(x, oa, ob, wts, tm):
    n, d = x.shape
    tm = min(tm, n)
    row = lambda i: (i, 0)
    consts = [wts[k] for k in ("g_out_mla", "g_out_swa", "w_out_a", "w_out_b", "g_mlp")]
    return pl.pallas_call(
        _out_proj_kernel, grid=(n // tm,),
        in_specs=[pl.BlockSpec((tm, d), row), pl.BlockSpec((tm, oa.shape[1]), row),
                  pl.BlockSpec((tm, ob.shape[1]), row)] + [_resident(a.shape) for a in consts],
        out_specs=(pl.BlockSpec((tm, d), row), pl.BlockSpec((tm, d), row)),
        out_shape=(jax.ShapeDtypeStruct((n, d), STREAM), jax.ShapeDtypeStruct((n, d), BF16)),
        compiler_params=_params("parallel"), name="out_proj",
    )(x, oa, ob, *consts)


def _mlp_kernel(x1_ref, hm_ref, wu_ref, wd_ref, o_ref):
    @pl.when(pl.program_id(1) == 0)
    def _():
        o_ref[...] = x1_ref[...].astype(F32)

    up = jnp.dot(hm_ref[...], wu_ref[...], preferred_element_type=F32)
    act = jnp.square(jnp.maximum(up, 0.0)).astype(BF16)
    o_ref[...] += jnp.dot(act, wd_ref[...], preferred_element_type=F32)


def _mlp(x1, hm, w_up, w_down, tm, tf):
    n, d = x1.shape
    f = w_up.shape[1]
    tm, tf = min(tm, n), min(tf, f)
    row = lambda i, j: (i, 0)
    early = lambda i, j: (jnp.minimum(i + jnp.where(j >= 2, 1, 0), n // tm - 1), 0)
    return pl.pallas_call(
        _mlp_kernel, grid=(n // tm, f // tf),
        in_specs=[pl.BlockSpec((tm, d), early), pl.BlockSpec((tm, d), row),
                  pl.BlockSpec((d, tf), lambda i, j: (0, j)), pl.BlockSpec((tf, d), lambda i, j: (j, 0))],
        out_specs=pl.BlockSpec((tm, d), row),
        out_shape=jax.ShapeDtypeStruct((n, d), F32),
        compiler_params=_params("parallel", "arbitrary"), name="mlp",
    )(x1, hm, w_up, w_down)


def _ple_kernel(x_ref, p_ref, g_ref, wg_ref, wp_ref, o_ref):
    x = x_ref[...]
    z = jnp.dot(_rms(x, g_ref[...]).astype(BF16), wg_ref[...], preferred_element_type=F32)
    gate = 1.0 / (1.0 + jnp.exp(-z))
    e = jnp.dot(p_ref[...].astype(BF16), wp_ref[...], preferred_element_type=F32)
    o_ref[...] = x + e * gate


def _ple(x2, p, wts, tm):
    n, d = x2.shape
    tm = min(tm, n)
    row = lambda i: (i, 0)
    consts = [wts[k] for k in ("g_ple", "w_gate", "w_ple")]
    return pl.pallas_call(
        _ple_kernel, grid=(n // tm,),
        in_specs=[pl.BlockSpec((tm, d), row), pl.BlockSpec((tm, p.shape[1]), row)]
                 + [_resident(a.shape) for a in consts],
        out_specs=pl.BlockSpec((tm, d), row),
        out_shape=jax.ShapeDtypeStruct((n, d), F32),
        compiler_params=_params("parallel"), name="ple",
    )(x2, p, *consts)


def _swap_halves(a):
    half = a.shape[-1] // 2
    return jnp.concatenate([a[..., half:], a[..., :half]], axis=-1)


def _prepare_weights(g_attn, w_in, g_qa, w_qb, g_kva, w_kvb, g_qn_mla, g_kn_mla, g_q_swa, g_k_swa,
                     g_out_mla, g_out_swa, w_out, g_mlp, w_up, w_down, g_ple, w_ple, w_gate):
    row = lambda g: g.reshape(1, -1)
    o_kpe = MLA_Q_LORA + MLA_KV_LORA
    o_qs = o_kpe + MLA_ROPE
    o_ks = o_qs + SWA_HEADS * SWA_HEAD_DIM
    o_vs = o_ks + SWA_KV_HEADS * SWA_HEAD_DIM
    kpe = w_in[:, o_kpe:o_qs]
    w_tm = jnp.concatenate([w_in[:, :o_kpe], kpe, _swap_halves(kpe), w_in[:, o_ks:o_vs]], axis=1)
    w_fm = jnp.concatenate([w_in[:, o_qs:o_ks], w_in[:, o_vs:]], axis=1).T
    w_kv = w_kvb.reshape(MLA_KV_LORA, MLA_HEADS, MLA_NOPE + MLA_V)
    w_kn = w_kv[..., :MLA_NOPE].reshape(MLA_KV_LORA, MLA_HEADS * MLA_NOPE)
    w_vt = w_kv[..., MLA_NOPE:].reshape(MLA_KV_LORA, MLA_HEADS * MLA_V).T
    g_rope = g_kn_mla[MLA_NOPE:]
    n_a = MLA_HEADS * MLA_V
    logit_span = 2.0 * MLA_QK ** 0.5 * LOG2E * jnp.max(jnp.abs(g_qn_mla)) * jnp.max(jnp.abs(g_kn_mla))
    return {
        "mla_logit_span": logit_span,
        "g_attn": row(g_attn), "w_tm": w_tm.astype(BF16), "w_fm": w_fm.astype(BF16),
        "g_qa": row(g_qa), "w_qbt": w_qb.T.astype(BF16),
        "g_kva": row(g_kva), "w_kn": w_kn.astype(BF16), "w_vt": w_vt.astype(BF16),
        "g_qn_mla": g_qn_mla, "gk_a": row(g_kn_mla[:MLA_NOPE]),
        "gk_b": row(jnp.concatenate([g_rope, _swap_halves(g_rope)])),
        "g_q_swa": g_q_swa, "g_k_swa": row(g_k_swa),
        "g_out_mla": row(g_out_mla), "g_out_swa": row(g_out_swa),
        "w_out_a": w_out[:n_a].astype(BF16), "w_out_b": w_out[n_a:].astype(BF16),
        "g_mlp": row(g_mlp), "w_up": w_up.astype(BF16), "w_down": w_down.astype(BF16),
        "g_ple": row(g_ple), "w_gate": w_gate.astype(BF16), "w_ple": w_ple.astype(BF16),
    }


def _layer(x, p, sink, wts, tabs):
    batch, seq, d = x.shape
    n = batch * seq
    xf = x.reshape(n, d)
    qt, k, vt, qst, ks, vst = _proj(xf, seq, wts, tabs, tm=512)
    o_a = lax.cond(
        wts["mla_logit_span"] <= SAFE_LOG2_RANGE,
        functools.partial(_mla_attention, batch=batch, seq=seq, tq=1024, tk=4096, ck=256, track_max=False),
        functools.partial(_mla_attention, batch=batch, seq=seq, tq=512, tk=512, ck=512, track_max=True),
        qt, k, vt)
    o_b = _swa_attention(sink, qst, ks, vst, batch, seq, tq=512, sub=256)
    x1, hm = _out_proj(xf, o_a, o_b, wts, tm=512)
    x2 = _mlp(x1, hm, wts["w_up"], wts["w_down"], tm=1024, tf=1024)
    y = _ple(x2, p.reshape(n, -1), wts, tm=512)
    return y.reshape(batch, seq, d)


def kernel(x_prompt, x_sample, p_prompt, p_sample, g_attn, w_in, g_qa, w_qb, g_kva, w_kvb, g_qn_mla, g_kn_mla,
           g_q_swa, g_k_swa, sink, g_out_mla, g_out_swa, w_out, g_mlp, w_up, w_down, g_ple, w_ple, w_gate):
    y_prompt, y_sample = x_prompt, x_sample
    tabs = _rope_inputs(max(x_prompt.shape[1], x_sample.shape[1]))
    for l in range(g_attn.shape[0]):
        wts = _prepare_weights(g_attn[l], w_in[l], g_qa[l], w_qb[l], g_kva[l], w_kvb[l], g_qn_mla[l],
                               g_kn_mla[l], g_q_swa[l], g_k_swa[l], g_out_mla[l], g_out_swa[l], w_out[l],
                               g_mlp[l], w_up[l], w_down[l], g_ple[l], w_ple[l], w_gate[l])
        y_prompt = _layer(y_prompt, p_prompt[l], sink[l], wts, tabs)
        y_sample = _layer(y_sample, p_sample[l], sink[l], wts, tabs)
    return (y_prompt, y_sample)
```

```python
import functools
import math

import jax
import jax.numpy as jnp
from jax import lax
from jax.experimental import pallas as pl
from jax.experimental.pallas import tpu as pltpu

EPS = 1e-6
ROPE_THETA = 10000.0
WINDOW = 128
MLA_HEADS = 8
MLA_Q_LORA = 512
MLA_KV_LORA = 256
MLA_NOPE = 128
MLA_ROPE = 64
MLA_V = 128
MLA_QK = MLA_NOPE + MLA_ROPE
SWA_HEADS = 8
SWA_KV_HEADS = 2
SWA_GROUP = SWA_HEADS // SWA_KV_HEADS
SWA_HEAD_DIM = 128
LANES = 128
SUBLANES = 8
MLA_PAD = 2 * LANES
SAFE_LOG2_RANGE = 100.0
LOG2E = math.log2(math.e)
VMEM_LIMIT = 56 * 1024 * 1024

BF16 = jnp.bfloat16
F32 = jnp.float32
STREAM = jnp.bfloat16
NT = (((1,), (1,)), ((), ()))


def _params(*sem):
    return pltpu.CompilerParams(dimension_semantics=sem, vmem_limit_bytes=VMEM_LIMIT)


def _resident(shape):
    nd = len(shape)
    return pl.BlockSpec(shape, lambda *_: (0,) * nd, pipeline_mode=pl.Buffered(1))


def _rms(x, g):
    return x * lax.rsqrt(jnp.mean(x * x, axis=-1, keepdims=True) + EPS) * g


def _rope_lanes(x, t_cos, t_sin):
    return x * t_cos + pltpu.roll(x, LANES // 2, axis=1) * t_sin


def _rope_rows(x1, x2, cos, sin):
    return x1 * cos - x2 * sin, x2 * cos + x1 * sin


def _proj_kernel(x_ref, g_attn_ref, w_tm_ref, w_fm_ref, g_qa_ref, w_qbt_ref, g_kva_ref, w_kn_ref, w_vt_ref,
                 gq_ref, gk_a_ref, gk_b_ref, gqs_ref, gks_ref,
                 mc_ref, ms_ref, sc_ref, ss_ref, tkc_ref, tks_ref, tsc_ref, tss_ref,
                 qt_ref, k_ref, vt_ref, qst_ref, ks_ref, vst_ref):
    tm = x_ref.shape[0]
    h = _rms(x_ref[...], g_attn_ref[...]).astype(BF16)
    p_tm = jnp.dot(h, w_tm_ref[...], preferred_element_type=F32)
    o_ckv = MLA_Q_LORA
    o_kpe = o_ckv + MLA_KV_LORA
    o_ks = o_kpe + LANES

    k_bound = MLA_QK ** 0.5 * jnp.maximum(jnp.max(jnp.abs(gk_a_ref[...]), axis=-1, keepdims=True),
                                          jnp.max(jnp.abs(gk_b_ref[...]), axis=-1, keepdims=True))
    shift_lane = (lax.broadcasted_iota(jnp.int32, (1, LANES), 1) == MLA_QK - LANES).astype(F32)
    first_row = lax.broadcasted_iota(jnp.int32, (MLA_PAD - MLA_QK, tm), 0) == 0

    cqn = _rms(p_tm[:, :o_ckv], g_qa_ref[...]).astype(BF16)
    q_t = lax.dot_general(w_qbt_ref[...], cqn, NT, preferred_element_type=F32)
    q_scale = MLA_QK ** -0.5 * LOG2E
    half = MLA_ROPE // 2
    for hd in range(MLA_HEADS):
        blk = q_t[hd * MLA_QK:(hd + 1) * MLA_QK]
        r = lax.rsqrt(jnp.sum(blk * blk, axis=0, keepdims=True) * (1.0 / MLA_QK) + EPS) * q_scale
        y = blk * gq_ref[...] * r
        shift = -(jnp.sqrt(jnp.sum(y * y, axis=0, keepdims=True)) * k_bound)
        o1, o2 = _rope_rows(y[MLA_NOPE:MLA_NOPE + half], y[MLA_NOPE + half:], mc_ref[...], ms_ref[...])
        qt_ref[hd, :MLA_NOPE, :] = y[:MLA_NOPE].astype(BF16)
        qt_ref[hd, MLA_NOPE:MLA_NOPE + half, :] = o1.astype(BF16)
        qt_ref[hd, MLA_NOPE + half:MLA_QK, :] = o2.astype(BF16)
        qt_ref[hd, MLA_QK:, :] = jnp.where(first_row, shift, 0.0).astype(BF16)

    ckvn = _rms(p_tm[:, o_ckv:o_kpe], g_kva_ref[...]).astype(BF16)
    kn_all = jnp.dot(ckvn, w_kn_ref[...], preferred_element_type=F32)
    kb = p_tm[:, o_kpe:o_ks]
    s_pe = 0.5 * jnp.sum(kb * kb, axis=-1, keepdims=True)
    kr = _rope_lanes(kb * gk_b_ref[...], tkc_ref[...], tks_ref[...])
    for hd in range(MLA_HEADS):
        kn = kn_all[:, hd * MLA_NOPE:(hd + 1) * MLA_NOPE]
        r = lax.rsqrt((jnp.sum(kn * kn, axis=-1, keepdims=True) + s_pe) * (1.0 / MLA_QK) + EPS)
        k_ref[hd, :, :LANES] = (kn * gk_a_ref[...] * r).astype(BF16)
        k_ref[hd, :, LANES:] = (kr * r + shift_lane).astype(BF16)
    v_t = lax.dot_general(w_vt_ref[...], ckvn, NT, preferred_element_type=F32)
    for hd in range(MLA_HEADS):
        vt_ref[hd] = v_t[hd * MLA_V:(hd + 1) * MLA_V].astype(BF16)

    p_fm = lax.dot_general(w_fm_ref[...], h, NT, preferred_element_type=F32)
    s_scale = SWA_HEAD_DIM ** -0.5 * LOG2E
    sh = SWA_HEAD_DIM // 2
    for hd in range(SWA_HEADS):
        blk = p_fm[hd * SWA_HEAD_DIM:(hd + 1) * SWA_HEAD_DIM]
        r = lax.rsqrt(jnp.mean(blk * blk, axis=0, keepdims=True) + EPS) * s_scale
        y = blk * gqs_ref[...] * r
        o1, o2 = _rope_rows(y[:sh], y[sh:], sc_ref[...], ss_ref[...])
        qst_ref[hd * SWA_HEAD_DIM:hd * SWA_HEAD_DIM + sh, :] = o1.astype(BF16)
        qst_ref[hd * SWA_HEAD_DIM + sh:(hd + 1) * SWA_HEAD_DIM, :] = o2.astype(BF16)
    vst_ref[...] = p_fm[SWA_HEADS * SWA_HEAD_DIM:].astype(BF16)
    for hd in range(SWA_KV_HEADS):
        xs = p_tm[:, o_ks + hd * SWA_HEAD_DIM: o_ks + (hd + 1) * SWA_HEAD_DIM]
        kn = _rope_lanes(_rms(xs, gks_ref[...]), tsc_ref[...], tss_ref[...])
        ks_ref[:, hd * SWA_HEAD_DIM:(hd + 1) * SWA_HEAD_DIM] = kn.astype(BF16)


def _rope_inputs(seq):
    fm_tabs, tm_tabs = [], []
    for dim in (MLA_ROPE, SWA_HEAD_DIM):
        half = dim // 2
        inv = 1.0 / (ROPE_THETA ** (jnp.arange(0, dim, 2, dtype=F32) / dim))
        fm_tabs += list(_cos_sin(inv[:, None], seq, axis=1))
        pad = jnp.zeros((LANES - dim,), F32)
        cos, sin = _cos_sin(jnp.concatenate([inv, inv, pad])[None, :], seq, axis=0)
        ones = jnp.ones((half,), F32)
        tm_tabs += [cos * jnp.concatenate([ones, ones, pad]), sin * jnp.concatenate([-ones, ones, pad])]
    return fm_tabs, tm_tabs


def _cos_sin(inv, seq, axis):
    blk = LANES
    assert seq % blk == 0
    shape = [1, 1]
    shape[axis] = -1
    pos_a = (jnp.arange(seq // blk, dtype=F32) * blk).reshape(shape)
    pos_b = jnp.arange(blk, dtype=F32).reshape(shape)
    ca, sa = jnp.cos(pos_a * inv), jnp.sin(pos_a * inv)
    cb, sb = jnp.cos(pos_b * inv), jnp.sin(pos_b * inv)
    ca, sa = jnp.expand_dims(ca, axis + 1), jnp.expand_dims(sa, axis + 1)
    cb, sb = jnp.expand_dims(cb, axis), jnp.expand_dims(sb, axis)
    out_shape = list(inv.shape)
    out_shape[axis] = seq
    return (ca * cb - sa * sb).reshape(out_shape), (sa * cb + ca * sb).reshape(out_shape)


def _proj(x, seq, wts, tabs, tm):
    n, d = x.shape
    tm = min(tm, seq)
    nt_seq = seq // tm
    row = lambda i: (i, 0)
    col = lambda i: (0, i)
    bcast = lambda g: jnp.broadcast_to(g.reshape(-1, 1), (g.size, tm))
    consts = [wts["g_attn"], wts["w_tm"], wts["w_fm"], wts["g_qa"], wts["w_qbt"], wts["g_kva"], wts["w_kn"],
              wts["w_vt"], bcast(wts["g_qn_mla"]), wts["gk_a"], wts["gk_b"], bcast(wts["g_q_swa"]), wts["g_k_swa"]]
    fm_tabs, tm_tabs = tabs
    in_specs = ([pl.BlockSpec((tm, d), row)] + [_resident(a.shape) for a in consts]
                + [pl.BlockSpec((t.shape[0], tm), lambda i: (0, i % nt_seq)) for t in fm_tabs]
                + [pl.BlockSpec((tm, LANES), lambda i: (i % nt_seq, 0))] * 4)
    n_qs = SWA_HEADS * SWA_HEAD_DIM
    n_ks = SWA_KV_HEADS * SWA_HEAD_DIM
    out_shape = (
        jax.ShapeDtypeStruct((MLA_HEADS, MLA_PAD, n), BF16),
        jax.ShapeDtypeStruct((MLA_HEADS, n, MLA_PAD), BF16),
        jax.ShapeDtypeStruct((MLA_HEADS, MLA_V, n), BF16),
        jax.ShapeDtypeStruct((n_qs, n), BF16),
        jax.ShapeDtypeStruct((n, n_ks), BF16),
        jax.ShapeDtypeStruct((n_ks, n), BF16),
    )
    out_specs = (
        pl.BlockSpec((MLA_HEADS, MLA_PAD, tm), lambda i: (0, 0, i)),
        pl.BlockSpec((MLA_HEADS, tm, MLA_PAD), lambda i: (0, i, 0)),
        pl.BlockSpec((MLA_HEADS, MLA_V, tm), lambda i: (0, 0, i)),
        pl.BlockSpec((n_qs, tm), col),
        pl.BlockSpec((tm, n_ks), row),
        pl.BlockSpec((n_ks, tm), col),
    )
    return pl.pallas_call(
        _proj_kernel, grid=(n // tm,), in_specs=in_specs, out_specs=out_specs,
        out_shape=out_shape, compiler_params=_params("parallel"), name="proj",
    )(x, *consts, *fm_tabs, *tm_tabs)


def _mla_kernel(qt_ref, k_ref, vt_ref, o_ref, *scratch, ck, track_max):
    j = pl.program_id(3)
    tk, tq = k_ref.shape[1], qt_ref.shape[2]
    if track_max:
        m_ref, l_ref, acc_ref = scratch
    else:
        l_ref, acc_ref, p_ref = scratch

    @pl.when(j == 0)
    def _():
        if track_max:
            m_ref[...] = jnp.full_like(m_ref, -jnp.inf)
        l_ref[...] = jnp.zeros_like(l_ref)
        acc_ref[...] = jnp.zeros_like(acc_ref)

    qt = qt_ref[0]
    if track_max:
        s = jnp.dot(k_ref[0], qt, preferred_element_type=F32)
        m_prev = m_ref[...]
        m_new = jnp.maximum(m_prev, jnp.max(s, axis=0, keepdims=True))
        alpha = jnp.exp2(m_prev - m_new)
        p = jnp.exp2(s - m_new)
        l_ref[...] = alpha * l_ref[...] + jnp.sum(p, axis=0, keepdims=True)
        acc_ref[...] = alpha * acc_ref[...] + jnp.dot(vt_ref[0], p.astype(BF16), preferred_element_type=F32)
        m_ref[...] = m_new
    else:
        l_part = l_ref[...]
        for c in range(tk // ck):
            s = jnp.dot(k_ref[0, c * ck:(c + 1) * ck, :], qt, preferred_element_type=F32)
            p = jnp.exp2(s)
            l_part = l_part + jnp.sum(p.reshape(ck // SUBLANES, SUBLANES, tq), axis=0)
            p_ref[c * ck:(c + 1) * ck, :] = p.astype(BF16)
        l_ref[...] = l_part
        acc_ref[...] += jnp.dot(vt_ref[0], p_ref[...], preferred_element_type=F32)

    @pl.when(j == pl.num_programs(3) - 1)
    def _():
        l = jnp.sum(l_ref[...], axis=0, keepdims=True)
        o_ref[...] = (acc_ref[...] / l).T.astype(o_ref.dtype)


def _mla_attention(qt, k, vt, batch, seq, tq, tk, ck, track_max):
    n = batch * seq
    tq, tk = min(tq, seq), min(tk, seq)
    nq, nk = seq // tq, seq // tk
    if track_max:
        scratch = [pltpu.VMEM((1, tq), F32), pltpu.VMEM((1, tq), F32), pltpu.VMEM((MLA_V, tq), F32)]
    else:
        scratch = [pltpu.VMEM((SUBLANES, tq), F32), pltpu.VMEM((MLA_V, tq), F32), pltpu.VMEM((tk, tq), BF16)]
    return pl.pallas_call(
        functools.partial(_mla_kernel, ck=min(ck, tk), track_max=track_max),
        grid=(batch, MLA_HEADS, nq, nk),
        in_specs=[
            pl.BlockSpec((1, MLA_PAD, tq), lambda b, h, i, j: (h, 0, b * nq + i)),
            pl.BlockSpec((1, tk, MLA_PAD), lambda b, h, i, j: (h, b * nk + j, 0)),
            pl.BlockSpec((1, MLA_V, tk), lambda b, h, i, j: (h, 0, b * nk + j)),
        ],
        out_specs=pl.BlockSpec((tq, MLA_V), lambda b, h, i, j: (b * nq + i, h)),
        out_shape=jax.ShapeDtypeStruct((n, MLA_HEADS * MLA_V), BF16),
        scratch_shapes=scratch,
        compiler_params=_params("parallel", "parallel", "parallel", "arbitrary"),
        name="mla_attn_online" if track_max else "mla_attn",
    )(qt, k, vt)


def _swa_kernel(sink_ref, qt_ref, kp_ref, km_ref, kn_ref, vp_ref, vm_ref, vn_ref, o_ref, kwin_ref, vwin_ref,
                *, nq, sub):
    g = pl.program_id(1)
    i = pl.program_id(2)
    tq = qt_ref.shape[1]
    kwin_ref[:WINDOW] = kp_ref[...]
    kwin_ref[WINDOW:WINDOW + tq] = km_ref[...]
    kwin_ref[WINDOW + tq:] = kn_ref[...]
    vwin_ref[:, :WINDOW] = vp_ref[...]
    vwin_ref[:, WINDOW:WINDOW + tq] = vm_ref[...]
    vwin_ref[:, WINDOW + tq:] = vn_ref[...]
    heads = range(SWA_GROUP)
    ws = sub + 2 * WINDOW
    sink = jnp.concatenate(
        [jnp.full((1, sub), sink_ref[g * SWA_GROUP + hd] * LOG2E, F32) for hd in heads], axis=1)
    key = lax.broadcasted_iota(jnp.int32, (ws, SWA_GROUP * sub), 0)
    qry = lax.broadcasted_iota(jnp.int32, (ws, SWA_GROUP * sub), 1) & (sub - 1)
    band = (key >= qry) & (key <= qry + 2 * WINDOW)
    no_prev = jnp.where(i > 0, 0.0, -jnp.inf)
    no_next = jnp.where(i < nq - 1, 0.0, -jnp.inf)
    for t in range(tq // sub):
        q_all = jnp.concatenate(
            [qt_ref[hd * SWA_HEAD_DIM:(hd + 1) * SWA_HEAD_DIM, t * sub:(t + 1) * sub] for hd in heads], axis=1)
        s = jnp.dot(kwin_ref[t * sub:t * sub + ws, :], q_all, preferred_element_type=F32)
        s = jnp.where(band, s, -jnp.inf)
        if t == 0:
            s = jnp.concatenate([s[:WINDOW] + no_prev, s[WINDOW:]], axis=0)
        if t == tq // sub - 1:
            s = jnp.concatenate([s[:ws - WINDOW], s[ws - WINDOW:] + no_next], axis=0)
        m = jnp.maximum(jnp.max(s, axis=0, keepdims=True), sink)
        p = jnp.exp2(s - m)
        denom = jnp.sum(p, axis=0, keepdims=True) + jnp.exp2(sink - m)
        o_t = jnp.dot(vwin_ref[:, t * sub:t * sub + ws], p.astype(BF16), preferred_element_type=F32) / denom
        for hd in heads:
            o_ref[t * sub:(t + 1) * sub, hd * SWA_HEAD_DIM:(hd + 1) * SWA_HEAD_DIM] = (
                o_t[:, hd * sub:(hd + 1) * sub].T.astype(o_ref.dtype))


def _swa_attention(sink, qt, k, vt, batch, seq, tq, sub):
    n = batch * seq
    tq = min(tq, seq)
    sub = min(sub, tq)
    assert sub & (sub - 1) == 0 and tq % sub == 0, "query index within a sub-tile is taken with a bit mask"
    nq = seq // tq
    r = tq // WINDOW
    nb = seq // WINDOW
    gw = SWA_GROUP * SWA_HEAD_DIM
    prev = lambda b, i: b * nb + jnp.maximum(i * r - 1, 0)
    nxt = lambda b, i: b * nb + jnp.minimum((i + 1) * r, nb - 1)
    edge = (WINDOW, SWA_HEAD_DIM)
    return pl.pallas_call(
        functools.partial(_swa_kernel, nq=nq, sub=sub),
        grid=(batch, SWA_KV_HEADS, nq),
        in_specs=[
            pl.BlockSpec(memory_space=pltpu.SMEM),
            pl.BlockSpec((gw, tq), lambda b, g, i: (g, b * nq + i)),
            pl.BlockSpec(edge, lambda b, g, i: (prev(b, i), g)),
            pl.BlockSpec((tq, SWA_HEAD_DIM), lambda b, g, i: (b * nq + i, g)),
            pl.BlockSpec(edge, lambda b, g, i: (nxt(b, i), g)),
            pl.BlockSpec(edge, lambda b, g, i: (g, prev(b, i))),
            pl.BlockSpec((SWA_HEAD_DIM, tq), lambda b, g, i: (g, b * nq + i)),
            pl.BlockSpec(edge, lambda b, g, i: (g, nxt(b, i))),
        ],
        out_specs=pl.BlockSpec((tq, gw), lambda b, g, i: (b * nq + i, g)),
        out_shape=jax.ShapeDtypeStruct((n, SWA_HEADS * SWA_HEAD_DIM), BF16),
        scratch_shapes=[pltpu.VMEM((tq + 2 * WINDOW, SWA_HEAD_DIM), BF16),
                        pltpu.VMEM((SWA_HEAD_DIM, tq + 2 * WINDOW), BF16)],
        compiler_params=_params("parallel", "parallel", "parallel"),
        name="swa_attn",
    )(sink, qt, k, k, k, vt, vt, vt)


def _out_proj_kernel(x_ref, oa_ref, ob_ref, ga_ref, gb_ref, wa_ref, wb_ref, g_mlp_ref, x1_ref, hm_ref):
    oa = _rms(oa_ref[...].astype(F32), ga_ref[...]).astype(BF16)
    ob = _rms(ob_ref[...].astype(F32), gb_ref[...]).astype(BF16)
    x1 = (x_ref[...] + jnp.dot(oa, wa_ref[...], preferred_element_type=F32)
          + jnp.dot(ob, wb_ref[...], preferred_element_type=F32))
    x1_ref[...] = x1.astype(x1_ref.dtype)
    hm_ref[...] = _rms(x1, g_mlp_ref[...]).astype(BF16)


def _out_proj(x, oa, ob, wts, tm):
    n, d = x.shape
    tm = min(tm, n)
    row = lambda i: (i, 0)
    consts = [wts[k] for k in ("g_out_mla", "g_out_swa", "w_out_a", "w_out_b", "g_mlp")]
    return pl.pallas_call(
        _out_proj_kernel, grid=(n // tm,),
        in_specs=[pl.BlockSpec((tm, d), row), pl.BlockSpec((tm, oa.shape[1]), row),
                  pl.BlockSpec((tm, ob.shape[1]), row)] + [_resident(a.shape) for a in consts],
        out_specs=(pl.BlockSpec((tm, d), row), pl.BlockSpec((tm, d), row)),
        out_shape=(jax.ShapeDtypeStruct((n, d), STREAM), jax.ShapeDtypeStruct((n, d), BF16)),
        compiler_params=_params("parallel"), name="out_proj",
    )(x, oa, ob, *consts)


def _mlp_kernel(x1_ref, hm_ref, wu_ref, wd_ref, o_ref):
    @pl.when(pl.program_id(1) == 0)
    def _():
        o_ref[...] = x1_ref[...].astype(F32)

    up = jnp.dot(hm_ref[...], wu_ref[...], preferred_element_type=F32)
    act = jnp.square(jnp.maximum(up, 0.0)).astype(BF16)
    o_ref[...] += jnp.dot(act, wd_ref[...], preferred_element_type=F32)


def _mlp(x1, hm, w_up, w_down, tm, tf):
    n, d = x1.shape
    f = w_up.shape[1]
    tm, tf = min(tm, n), min(tf, f)
    row = lambda i, j: (i, 0)
    early = lambda i, j: (jnp.minimum(i + jnp.where(j >= 2, 1, 0), n // tm - 1), 0)
    return pl.pallas_call(
        _mlp_kernel, grid=(n // tm, f // tf),
        in_specs=[pl.BlockSpec((tm, d), early), pl.BlockSpec((tm, d), row),
                  pl.BlockSpec((d, tf), lambda i, j: (0, j)), pl.BlockSpec((tf, d), lambda i, j: (j, 0))],
        out_specs=pl.BlockSpec((tm, d), row),
        out_shape=jax.ShapeDtypeStruct((n, d), F32),
        compiler_params=_params("parallel", "arbitrary"), name="mlp",
    )(x1, hm, w_up, w_down)


def _ple_kernel(x_ref, p_ref, g_ref, wg_ref, wp_ref, o_ref):
    x = x_ref[...]
    z = jnp.dot(_rms(x, g_ref[...]).astype(BF16), wg_ref[...], preferred_element_type=F32)
    gate = 1.0 / (1.0 + jnp.exp(-z))
    e = jnp.dot(p_ref[...].astype(BF16), wp_ref[...], preferred_element_type=F32)
    o_ref[...] = x + e * gate


def _ple(x2, p, wts, tm):
    n, d = x2.shape
    tm = min(tm, n)
    row = lambda i: (i, 0)
    consts = [wts[k] for k in ("g_ple", "w_gate", "w_ple")]
    return pl.pallas_call(
        _ple_kernel, grid=(n // tm,),
        in_specs=[pl.BlockSpec((tm, d), row), pl.BlockSpec((tm, p.shape[1]), row)]
                 + [_resident(a.shape) for a in consts],
        out_specs=pl.BlockSpec((tm, d), row),
        out_shape=jax.ShapeDtypeStruct((n, d), F32),
        compiler_params=_params("parallel"), name="ple",
    )(x2, p, *consts)


def _swap_halves(a):
    half = a.shape[-1] // 2
    return jnp.concatenate([a[..., half:], a[..., :half]], axis=-1)


def _prepare_weights(g_attn, w_in, g_qa, w_qb, g_kva, w_kvb, g_qn_mla, g_kn_mla, g_q_swa, g_k_swa,
                     g_out_mla, g_out_swa, w_out, g_mlp, w_up, w_down, g_ple, w_ple, w_gate):
    row = lambda g: g.reshape(1, -1)
    o_kpe = MLA_Q_LORA + MLA_KV_LORA
    o_qs = o_kpe + MLA_ROPE
    o_ks = o_qs + SWA_HEADS * SWA_HEAD_DIM
    o_vs = o_ks + SWA_KV_HEADS * SWA_HEAD_DIM
    kpe = w_in[:, o_kpe:o_qs]
    w_tm = jnp.concatenate([w_in[:, :o_kpe], kpe, _swap_halves(kpe), w_in[:, o_ks:o_vs]], axis=1)
    w_fm = jnp.concatenate([w_in[:, o_qs:o_ks], w_in[:, o_vs:]], axis=1).T
    w_kv = w_kvb.reshape(MLA_KV_LORA, MLA_HEADS, MLA_NOPE + MLA_V)
    w_kn = w_kv[..., :MLA_NOPE].reshape(MLA_KV_LORA, MLA_HEADS * MLA_NOPE)
    w_vt = w_kv[..., MLA_NOPE:].reshape(MLA_KV_LORA, MLA_HEADS * MLA_V).T
    g_rope = g_kn_mla[MLA_NOPE:]
    n_a = MLA_HEADS * MLA_V
    logit_span = 2.0 * MLA_QK ** 0.5 * LOG2E * jnp.max(jnp.abs(g_qn_mla)) * jnp.max(jnp.abs(g_kn_mla))
    return {
        "mla_logit_span": logit_span,
        "g_attn": row(g_attn), "w_tm": w_tm.astype(BF16), "w_fm": w_fm.astype(BF16),
        "g_qa": row(g_qa), "w_qbt": w_qb.T.astype(BF16),
        "g_kva": row(g_kva), "w_kn": w_kn.astype(BF16), "w_vt": w_vt.astype(BF16),
        "g_qn_mla": g_qn_mla, "gk_a": row(g_kn_mla[:MLA_NOPE]),
        "gk_b": row(jnp.concatenate([g_rope, _swap_halves(g_rope)])),
        "g_q_swa": g_q_swa, "g_k_swa": row(g_k_swa),
        "g_out_mla": row(g_out_mla), "g_out_swa": row(g_out_swa),
        "w_out_a": w_out[:n_a].astype(BF16), "w_out_b": w_out[n_a:].astype(BF16),
        "g_mlp": row(g_mlp), "w_up": w_up.astype(BF16), "w_down": w_down.astype(BF16),
        "g_ple": row(g_ple), "w_gate": w_gate.astype(BF16), "w_ple": w_ple.astype(BF16),
    }


def _layer(x, p, sink, wts, tabs):
    batch, seq, d = x.shape
    n = batch * seq
    xf = x.reshape(n, d)
    qt, k, vt, qst, ks, vst = _proj(xf, seq, wts, tabs, tm=512)
    o_a = lax.cond(
        wts["mla_logit_span"] <= SAFE_LOG2_RANGE,
        functools.partial(_mla_attention, batch=batch, seq=seq, tq=1024, tk=4096, ck=256, track_max=False),
        functools.partial(_mla_attention, batch=batch, seq=seq, tq=512, tk=512, ck=512, track_max=True),
        qt, k, vt)
    o_b = _swa_attention(sink, qst, ks, vst, batch, seq, tq=1024, sub=256)
    x1, hm = _out_proj(xf, o_a, o_b, wts, tm=512)
    x2 = _mlp(x1, hm, wts["w_up"], wts["w_down"], tm=1024, tf=1024)
    y = _ple(x2, p.reshape(n, -1), wts, tm=512)
    return y.reshape(batch, seq, d)


def kernel(x_prompt, x_sample, p_prompt, p_sample, g_attn, w_in, g_qa, w_qb, g_kva, w_kvb, g_qn_mla, g_kn_mla,
           g_q_swa, g_k_swa, sink, g_out_mla, g_out_swa, w_out, g_mlp, w_up, w_down, g_ple, w_ple, w_gate):
    y_prompt, y_sample = x_prompt, x_sample
    tabs = _rope_inputs(max(x_prompt.shape[1], x_sample.shape[1]))
    for l in range(g_attn.shape[0]):
        wts = _prepare_weights(g_attn[l], w_in[l], g_qa[l], w_qb[l], g_kva[l], w_kvb[l], g_qn_mla[l],
                               g_kn_mla[l], g_q_swa[l], g_k_swa[l], g_out_mla[l], g_out_swa[l], w_out[l],
                               g_mlp[l], w_up[l], w_down[l], g_ple[l], w_ple[l], w_gate[l])
        y_prompt = _layer(y_prompt, p_prompt[l], sink[l], wts, tabs)
        y_sample = _layer(y_sample, p_sample[l], sink[l], wts, tabs)
    return (y_prompt, y_sample)
```

```python
import functools
import math

import jax
import jax.numpy as jnp
from jax import lax
from jax.experimental import pallas as pl
from jax.experimental.pallas import tpu as pltpu

EPS = 1e-6
ROPE_THETA = 10000.0
WINDOW = 128
MLA_HEADS = 8
MLA_Q_LORA = 512
MLA_KV_LORA = 256
MLA_NOPE = 128
MLA_ROPE = 64
MLA_V = 128
MLA_QK = MLA_NOPE + MLA_ROPE
SWA_HEADS = 8
SWA_KV_HEADS = 2
SWA_GROUP = SWA_HEADS // SWA_KV_HEADS
SWA_HEAD_DIM = 128
LANES = 128
SUBLANES = 8
MLA_PAD = 2 * LANES
SAFE_LOG2_RANGE = 100.0
LOG2E = math.log2(math.e)
VMEM_LIMIT = 56 * 1024 * 1024

BF16 = jnp.bfloat16
F32 = jnp.float32
STREAM = jnp.bfloat16
NT = (((1,), (1,)), ((), ()))


def _params(*sem):
    return pltpu.CompilerParams(dimension_semantics=sem, vmem_limit_bytes=VMEM_LIMIT)


def _resident(shape):
    nd = len(shape)
    return pl.BlockSpec(shape, lambda *_: (0,) * nd, pipeline_mode=pl.Buffered(1))


def _rms(x, g):
    return x * lax.rsqrt(jnp.mean(x * x, axis=-1, keepdims=True) + EPS) * g


def _rope_lanes(x, t_cos, t_sin):
    return x * t_cos + pltpu.roll(x, LANES // 2, axis=1) * t_sin


def _rope_rows(x1, x2, cos, sin):
    return x1 * cos - x2 * sin, x2 * cos + x1 * sin


def _proj_kernel(x_ref, g_attn_ref, w_tm_ref, w_fm_ref, g_qa_ref, w_qbt_ref, g_kva_ref, w_kn_ref, w_vt_ref,
                 gq_ref, gk_a_ref, gk_b_ref, gqs_ref, gks_ref,
                 mc_ref, ms_ref, sc_ref, ss_ref, tkc_ref, tks_ref, tsc_ref, tss_ref,
                 qt_ref, k_ref, vt_ref, qst_ref, ks_ref, vst_ref):
    tm = x_ref.shape[0]
    h = _rms(x_ref[...], g_attn_ref[...]).astype(BF16)
    p_tm = jnp.dot(h, w_tm_ref[...], preferred_element_type=F32)
    o_ckv = MLA_Q_LORA
    o_kpe = o_ckv + MLA_KV_LORA
    o_ks = o_kpe + LANES

    k_bound = MLA_QK ** 0.5 * jnp.maximum(jnp.max(jnp.abs(gk_a_ref[...]), axis=-1, keepdims=True),
                                          jnp.max(jnp.abs(gk_b_ref[...]), axis=-1, keepdims=True))
    shift_lane = (lax.broadcasted_iota(jnp.int32, (1, LANES), 1) == MLA_QK - LANES).astype(F32)
    first_row = lax.broadcasted_iota(jnp.int32, (MLA_PAD - MLA_QK, tm), 0) == 0

    cqn = _rms(p_tm[:, :o_ckv], g_qa_ref[...]).astype(BF16)
    q_t = lax.dot_general(w_qbt_ref[...], cqn, NT, preferred_element_type=F32)
    q_scale = MLA_QK ** -0.5 * LOG2E
    half = MLA_ROPE // 2
    for hd in range(MLA_HEADS):
        blk = q_t[hd * MLA_QK:(hd + 1) * MLA_QK]
        r = lax.rsqrt(jnp.sum(blk * blk, axis=0, keepdims=True) * (1.0 / MLA_QK) + EPS) * q_scale
        y = blk * gq_ref[...] * r
        shift = -(jnp.sqrt(jnp.sum(y * y, axis=0, keepdims=True)) * k_bound)
        o1, o2 = _rope_rows(y[MLA_NOPE:MLA_NOPE + half], y[MLA_NOPE + half:], mc_ref[...], ms_ref[...])
        qt_ref[hd, :MLA_NOPE, :] = y[:MLA_NOPE].astype(BF16)
        qt_ref[hd, MLA_NOPE:MLA_NOPE + half, :] = o1.astype(BF16)
        qt_ref[hd, MLA_NOPE + half:MLA_QK, :] = o2.astype(BF16)
        qt_ref[hd, MLA_QK:, :] = jnp.where(first_row, shift, 0.0).astype(BF16)

    ckvn = _rms(p_tm[:, o_ckv:o_kpe], g_kva_ref[...]).astype(BF16)
    kn_all = jnp.dot(ckvn, w_kn_ref[...], preferred_element_type=F32)
    kb = p_tm[:, o_kpe:o_ks]
    s_pe = 0.5 * jnp.sum(kb * kb, axis=-1, keepdims=True)
    kr = _rope_lanes(kb * gk_b_ref[...], tkc_ref[...], tks_ref[...])
    for hd in range(MLA_HEADS):
        kn = kn_all[:, hd * MLA_NOPE:(hd + 1) * MLA_NOPE]
        r = lax.rsqrt((jnp.sum(kn * kn, axis=-1, keepdims=True) + s_pe) * (1.0 / MLA_QK) + EPS)
        k_ref[hd, :, :LANES] = (kn * gk_a_ref[...] * r).astype(BF16)
        k_ref[hd, :, LANES:] = (kr * r + shift_lane).astype(BF16)
    v_t = lax.dot_general(w_vt_ref[...], ckvn, NT, preferred_element_type=F32)
    for hd in range(MLA_HEADS):
        vt_ref[hd] = v_t[hd * MLA_V:(hd + 1) * MLA_V].astype(BF16)

    p_fm = lax.dot_general(w_fm_ref[...], h, NT, preferred_element_type=F32)
    s_scale = SWA_HEAD_DIM ** -0.5 * LOG2E
    sh = SWA_HEAD_DIM // 2
    for hd in range(SWA_HEADS):
        blk = p_fm[hd * SWA_HEAD_DIM:(hd + 1) * SWA_HEAD_DIM]
        r = lax.rsqrt(jnp.mean(blk * blk, axis=0, keepdims=True) + EPS) * s_scale
        y = blk * gqs_ref[...] * r
        o1, o2 = _rope_rows(y[:sh], y[sh:], sc_ref[...], ss_ref[...])
        qst_ref[hd * SWA_HEAD_DIM:hd * SWA_HEAD_DIM + sh, :] = o1.astype(BF16)
        qst_ref[hd * SWA_HEAD_DIM + sh:(hd + 1) * SWA_HEAD_DIM, :] = o2.astype(BF16)
    vst_ref[...] = p_fm[SWA_HEADS * SWA_HEAD_DIM:].astype(BF16)
    for hd in range(SWA_KV_HEADS):
        xs = p_tm[:, o_ks + hd * SWA_HEAD_DIM: o_ks + (hd + 1) * SWA_HEAD_DIM]
        kn = _rope_lanes(_rms(xs, gks_ref[...]), tsc_ref[...], tss_ref[...])
        ks_ref[:, hd * SWA_HEAD_DIM:(hd + 1) * SWA_HEAD_DIM] = kn.astype(BF16)


def _rope_inputs(seq):
    fm_tabs, tm_tabs = [], []
    for dim in (MLA_ROPE, SWA_HEAD_DIM):
        half = dim // 2
        inv = 1.0 / (ROPE_THETA ** (jnp.arange(0, dim, 2, dtype=F32) / dim))
        fm_tabs += list(_cos_sin(inv[:, None], seq, axis=1))
        pad = jnp.zeros((LANES - dim,), F32)
        cos, sin = _cos_sin(jnp.concatenate([inv, inv, pad])[None, :], seq, axis=0)
        ones = jnp.ones((half,), F32)
        tm_tabs += [cos * jnp.concatenate([ones, ones, pad]), sin * jnp.concatenate([-ones, ones, pad])]
    return fm_tabs, tm_tabs


def _cos_sin(inv, seq, axis):
    blk = LANES
    assert seq % blk == 0
    shape = [1, 1]
    shape[axis] = -1
    pos_a = (jnp.arange(seq // blk, dtype=F32) * blk).reshape(shape)
    pos_b = jnp.arange(blk, dtype=F32).reshape(shape)
    ca, sa = jnp.cos(pos_a * inv), jnp.sin(pos_a * inv)
    cb, sb = jnp.cos(pos_b * inv), jnp.sin(pos_b * inv)
    ca, sa = jnp.expand_dims(ca, axis + 1), jnp.expand_dims(sa, axis + 1)
    cb, sb = jnp.expand_dims(cb, axis), jnp.expand_dims(sb, axis)
    out_shape = list(inv.shape)
    out_shape[axis] = seq
    return (ca * cb - sa * sb).reshape(out_shape), (sa * cb + ca * sb).reshape(out_shape)


def _proj(x, seq, wts, tabs, tm):
    n, d = x.shape
    tm = min(tm, seq)
    nt_seq = seq // tm
    row = lambda i: (i, 0)
    col = lambda i: (0, i)
    bcast = lambda g: jnp.broadcast_to(g.reshape(-1, 1), (g.size, tm))
    consts = [wts["g_attn"], wts["w_tm"], wts["w_fm"], wts["g_qa"], wts["w_qbt"], wts["g_kva"], wts["w_kn"],
              wts["w_vt"], bcast(wts["g_qn_mla"]), wts["gk_a"], wts["gk_b"], bcast(wts["g_q_swa"]), wts["g_k_swa"]]
    fm_tabs, tm_tabs = tabs
    in_specs = ([pl.BlockSpec((tm, d), row)] + [_resident(a.shape) for a in consts]
                + [pl.BlockSpec((t.shape[0], tm), lambda i: (0, i % nt_seq)) for t in fm_tabs]
                + [pl.BlockSpec((tm, LANES), lambda i: (i % nt_seq, 0))] * 4)
    n_qs = SWA_HEADS * SWA_HEAD_DIM
    n_ks = SWA_KV_HEADS * SWA_HEAD_DIM
    out_shape = (
        jax.ShapeDtypeStruct((MLA_HEADS, MLA_PAD, n), BF16),
        jax.ShapeDtypeStruct((MLA_HEADS, n, MLA_PAD), BF16),
        jax.ShapeDtypeStruct((MLA_HEADS, MLA_V, n), BF16),
        jax.ShapeDtypeStruct((n_qs, n), BF16),
        jax.ShapeDtypeStruct((n, n_ks), BF16),
        jax.ShapeDtypeStruct((n_ks, n), BF16),
    )
    out_specs = (
        pl.BlockSpec((MLA_HEADS, MLA_PAD, tm), lambda i: (0, 0, i)),
        pl.BlockSpec((MLA_HEADS, tm, MLA_PAD), lambda i: (0, i, 0)),
        pl.BlockSpec((MLA_HEADS, MLA_V, tm), lambda i: (0, 0, i)),
        pl.BlockSpec((n_qs, tm), col),
        pl.BlockSpec((tm, n_ks), row),
        pl.BlockSpec((n_ks, tm), col),
    )
    return pl.pallas_call(
        _proj_kernel, grid=(n // tm,), in_specs=in_specs, out_specs=out_specs,
        out_shape=out_shape, compiler_params=_params("parallel"), name="proj",
    )(x, *consts, *fm_tabs, *tm_tabs)


def _mla_kernel(qt_ref, k_ref, vt_ref, o_ref, *scratch, ck, track_max):
    j = pl.program_id(3)
    tk, tq = k_ref.shape[1], qt_ref.shape[2]
    if track_max:
        m_ref, l_ref, acc_ref = scratch
    else:
        l_ref, acc_ref, p_ref = scratch

    @pl.when(j == 0)
    def _():
        if track_max:
            m_ref[...] = jnp.full_like(m_ref, -jnp.inf)
        l_ref[...] = jnp.zeros_like(l_ref)
        acc_ref[...] = jnp.zeros_like(acc_ref)

    qt = qt_ref[0]
    if track_max:
        s = jnp.dot(k_ref[0], qt, preferred_element_type=F32)
        m_prev = m_ref[...]
        m_new = jnp.maximum(m_prev, jnp.max(s, axis=0, keepdims=True))
        alpha = jnp.exp2(m_prev - m_new)
        p = jnp.exp2(s - m_new)
        l_ref[...] = alpha * l_ref[...] + jnp.sum(p, axis=0, keepdims=True)
        acc_ref[...] = alpha * acc_ref[...] + jnp.dot(vt_ref[0], p.astype(BF16), preferred_element_type=F32)
        m_ref[...] = m_new
    else:
        l_part = l_ref[...]
        for c in range(tk // ck):
            s = jnp.dot(k_ref[0, c * ck:(c + 1) * ck, :], qt, preferred_element_type=F32)
            p = jnp.exp2(s)
            l_part = l_part + jnp.sum(p.reshape(ck // SUBLANES, SUBLANES, tq), axis=0)
            p_ref[c * ck:(c + 1) * ck, :] = p.astype(BF16)
        l_ref[...] = l_part
        acc_ref[...] += jnp.dot(vt_ref[0], p_ref[...], preferred_element_type=F32)

    @pl.when(j == pl.num_programs(3) - 1)
    def _():
        l = jnp.sum(l_ref[...], axis=0, keepdims=True)
        o_ref[...] = (acc_ref[...] / l).T.astype(o_ref.dtype)


def _mla_attention(qt, k, vt, batch, seq, tq, tk, ck, track_max):
    n = batch * seq
    tq, tk = min(tq, seq), min(tk, seq)
    nq, nk = seq // tq, seq // tk
    if track_max:
        scratch = [pltpu.VMEM((1, tq), F32), pltpu.VMEM((1, tq), F32), pltpu.VMEM((MLA_V, tq), F32)]
    else:
        scratch = [pltpu.VMEM((SUBLANES, tq), F32), pltpu.VMEM((MLA_V, tq), F32), pltpu.VMEM((tk, tq), BF16)]
    return pl.pallas_call(
        functools.partial(_mla_kernel, ck=min(ck, tk), track_max=track_max),
        grid=(batch, MLA_HEADS, nq, nk),
        in_specs=[
            pl.BlockSpec((1, MLA_PAD, tq), lambda b, h, i, j: (h, 0, b * nq + i)),
            pl.BlockSpec((1, tk, MLA_PAD), lambda b, h, i, j: (h, b * nk + j, 0)),
            pl.BlockSpec((1, MLA_V, tk), lambda b, h, i, j: (h, 0, b * nk + j)),
        ],
        out_specs=pl.BlockSpec((tq, MLA_V), lambda b, h, i, j: (b * nq + i, h)),
        out_shape=jax.ShapeDtypeStruct((n, MLA_HEADS * MLA_V), BF16),
        scratch_shapes=scratch,
        compiler_params=_params("parallel", "parallel", "parallel", "arbitrary"),
        name="mla_attn_online" if track_max else "mla_attn",
    )(qt, k, vt)


def _swa_kernel(sink_ref, qt_ref, kp_ref, km_ref, kn_ref, vp_ref, vm_ref, vn_ref, o_ref, kwin_ref, vwin_ref,
                *, nq, sub):
    g = pl.program_id(1)
    i = pl.program_id(2)
    tq = qt_ref.shape[1]
    kwin_ref[:WINDOW] = kp_ref[...]
    kwin_ref[WINDOW:WINDOW + tq] = km_ref[...]
    kwin_ref[WINDOW + tq:] = kn_ref[...]
    vwin_ref[:, :WINDOW] = vp_ref[...]
    vwin_ref[:, WINDOW:WINDOW + tq] = vm_ref[...]
    vwin_ref[:, WINDOW + tq:] = vn_ref[...]
    heads = range(SWA_GROUP)
    ws = sub + 2 * WINDOW
    sink = jnp.concatenate(
        [jnp.full((1, sub), sink_ref[g * SWA_GROUP + hd] * LOG2E, F32) for hd in heads], axis=1)
    key = lax.broadcasted_iota(jnp.int32, (ws, SWA_GROUP * sub), 0)
    qry = lax.broadcasted_iota(jnp.int32, (ws, SWA_GROUP * sub), 1) & (sub - 1)
    band = (key >= qry) & (key <= qry + 2 * WINDOW)
    no_prev = jnp.where(i > 0, 0.0, -jnp.inf)
    no_next = jnp.where(i < nq - 1, 0.0, -jnp.inf)
    for t in range(tq // sub):
        q_all = jnp.concatenate(
            [qt_ref[hd * SWA_HEAD_DIM:(hd + 1) * SWA_HEAD_DIM, t * sub:(t + 1) * sub] for hd in heads], axis=1)
        s = jnp.dot(kwin_ref[t * sub:t * sub + ws, :], q_all, preferred_element_type=F32)
        s = jnp.where(band, s, -jnp.inf)
        if t == 0:
            s = jnp.concatenate([s[:WINDOW] + no_prev, s[WINDOW:]], axis=0)
        if t == tq // sub - 1:
            s = jnp.concatenate([s[:ws - WINDOW], s[ws - WINDOW:] + no_next], axis=0)
        m = jnp.maximum(jnp.max(s, axis=0, keepdims=True), sink)
        p = jnp.exp2(s - m)
        denom = jnp.sum(p, axis=0, keepdims=True) + jnp.exp2(sink - m)
        o_t = jnp.dot(vwin_ref[:, t * sub:t * sub + ws], p.astype(BF16), preferred_element_type=F32) / denom
        for hd in heads:
            o_ref[t * sub:(t + 1) * sub, hd * SWA_HEAD_DIM:(hd + 1) * SWA_HEAD_DIM] = (
                o_t[:, hd * sub:(hd + 1) * sub].T.astype(o_ref.dtype))


def _swa_attention(sink, qt, k, vt, batch, seq, tq, sub):
    n = batch * seq
    tq = min(tq, seq)
    sub = min(sub, tq)
    assert sub & (sub - 1) == 0 and tq % sub == 0, "query index within a sub-tile is taken with a bit mask"
    nq = seq // tq
    r = tq // WINDOW
    nb = seq // WINDOW
    gw = SWA_GROUP * SWA_HEAD_DIM
    prev = lambda b, i: b * nb + jnp.maximum(i * r - 1, 0)
    nxt = lambda b, i: b * nb + jnp.minimum((i + 1) * r, nb - 1)
    edge = (WINDOW, SWA_HEAD_DIM)
    return pl.pallas_call(
        functools.partial(_swa_kernel, nq=nq, sub=sub),
        grid=(batch, SWA_KV_HEADS, nq),
        in_specs=[
            pl.BlockSpec(memory_space=pltpu.SMEM),
            pl.BlockSpec((gw, tq), lambda b, g, i: (g, b * nq + i)),
            pl.BlockSpec(edge, lambda b, g, i: (prev(b, i), g)),
            pl.BlockSpec((tq, SWA_HEAD_DIM), lambda b, g, i: (b * nq + i, g)),
            pl.BlockSpec(edge, lambda b, g, i: (nxt(b, i), g)),
            pl.BlockSpec(edge, lambda b, g, i: (g, prev(b, i))),
            pl.BlockSpec((SWA_HEAD_DIM, tq), lambda b, g, i: (g, b * nq + i)),
            pl.BlockSpec(edge, lambda b, g, i: (g, nxt(b, i))),
        ],
        out_specs=pl.BlockSpec((tq, gw), lambda b, g, i: (b * nq + i, g)),
        out_shape=jax.ShapeDtypeStruct((n, SWA_HEADS * SWA_HEAD_DIM), BF16),
        scratch_shapes=[pltpu.VMEM((tq + 2 * WINDOW, SWA_HEAD_DIM), BF16),
                        pltpu.VMEM((SWA_HEAD_DIM, tq + 2 * WINDOW), BF16)],
        compiler_params=_params("parallel", "parallel", "parallel"),
        name="swa_attn",
    )(sink, qt, k, k, k, vt, vt, vt)


def _out_proj_kernel(x_ref, oa_ref, ob_ref, ga_ref, gb_ref, wa_ref, wb_ref, g_mlp_ref, x1_ref, hm_ref):
    oa = _rms(oa_ref[...].astype(F32), ga_ref[...]).astype(BF16)
    ob = _rms(ob_ref[...].astype(F32), gb_ref[...]).astype(BF16)
    x1 = (x_ref[...] + jnp.dot(oa, wa_ref[...], preferred_element_type=F32)
          + jnp.dot(ob, wb_ref[...], preferred_element_type=F32))
    x1_ref[...] = x1.astype(x1_ref.dtype)
    hm_ref[...] = _rms(x1, g_mlp_ref[...]).astype(BF16)


def _out_proj(x, oa, ob, wts, tm):
    n, d = x.shape
    tm = min(tm, n)
    row = lambda i: (i, 0)
    consts = [wts[k] for k in ("g_out_mla", "g_out_swa", "w_out_a", "w_out_b", "g_mlp")]
    return pl.pallas_call(
        _out_proj_kernel, grid=(n // tm,),
        in_specs=[pl.BlockSpec((tm, d), row), pl.BlockSpec((tm, oa.shape[1]), row),
                  pl.BlockSpec((tm, ob.shape[1]), row)] + [_resident(a.shape) for a in consts],
        out_specs=(pl.BlockSpec((tm, d), row), pl.BlockSpec((tm, d), row)),
        out_shape=(jax.ShapeDtypeStruct((n, d), STREAM), jax.ShapeDtypeStruct((n, d), BF16)),
        compiler_params=_params("parallel"), name="out_proj",
    )(x, oa, ob, *consts)


def _mlp_kernel(x1_ref, hm_ref, wu_ref, wd_ref, o_ref):
    @pl.when(pl.program_id(1) == 0)
    def _():
        o_ref[...] = x1_ref[...].astype(F32)

    up = jnp.dot(hm_ref[...], wu_ref[...], preferred_element_type=F32)
    act = jnp.square(jnp.maximum(up, 0.0)).astype(BF16)
    o_ref[...] += jnp.dot(act, wd_ref[...], preferred_element_type=F32)


def _mlp(x1, hm, w_up, w_down, tm, tf):
    n, d = x1.shape
    f = w_up.shape[1]
    tm, tf = min(tm, n), min(tf, f)
    row = lambda i, j: (i, 0)
    early = lambda i, j: (jnp.minimum(i + jnp.where(j >= 2, 1, 0), n // tm - 1), 0)
    return pl.pallas_call(
        _mlp_kernel, grid=(n // tm, f // tf),
        in_specs=[pl.BlockSpec((tm, d), early), pl.BlockSpec((tm, d), row),
                  pl.BlockSpec((d, tf), lambda i, j: (0, j)), pl.BlockSpec((tf, d), lambda i, j: (j, 0))],
        out_specs=pl.BlockSpec((tm, d), row),
        out_shape=jax.ShapeDtypeStruct((n, d), F32),
        compiler_params=_params("parallel", "arbitrary"), name="mlp",
    )(x1, hm, w_up, w_down)


def _ple_kernel(x_ref, p_ref, g_ref, wg_ref, wp_ref, o_ref):
    x = x_ref[...]
    z = jnp.dot(_rms(x, g_ref[...]).astype(BF16), wg_ref[...], preferred_element_type=F32)
    gate = 1.0 / (1.0 + jnp.exp(-z))
    e = jnp.dot(p_ref[...].astype(BF16), wp_ref[...], preferred_element_type=F32)
    o_ref[...] = x + e * gate


def _ple(x2, p, wts, tm):
    n, d = x2.shape
    tm = min(tm, n)
    row = lambda i: (i, 0)
    consts = [wts[k] for k in ("g_ple", "w_gate", "w_ple")]
    return pl.pallas_call(
        _ple_kernel, grid=(n // tm,),
        in_specs=[pl.BlockSpec((tm, d), row), pl.BlockSpec((tm, p.shape[1]), row)]
                 + [_resident(a.shape) for a in consts],
        out_specs=pl.BlockSpec((tm, d), row),
        out_shape=jax.ShapeDtypeStruct((n, d), F32),
        compiler_params=_params("parallel"), name="ple",
    )(x2, p, *consts)


def _swap_halves(a):
    half = a.shape[-1] // 2
    return jnp.concatenate([a[..., half:], a[..., :half]], axis=-1)


def _prepare_weights(g_attn, w_in, g_qa, w_qb, g_kva, w_kvb, g_qn_mla, g_kn_mla, g_q_swa, g_k_swa,
                     g_out_mla, g_out_swa, w_out, g_mlp, w_up, w_down, g_ple, w_ple, w_gate):
    row = lambda g: g.reshape(1, -1)
    o_kpe = MLA_Q_LORA + MLA_KV_LORA
    o_qs = o_kpe + MLA_ROPE
    o_ks = o_qs + SWA_HEADS * SWA_HEAD_DIM
    o_vs = o_ks + SWA_KV_HEADS * SWA_HEAD_DIM
    kpe = w_in[:, o_kpe:o_qs]
    w_tm = jnp.concatenate([w_in[:, :o_kpe], kpe, _swap_halves(kpe), w_in[:, o_ks:o_vs]], axis=1)
    w_fm = jnp.concatenate([w_in[:, o_qs:o_ks], w_in[:, o_vs:]], axis=1).T
    w_kv = w_kvb.reshape(MLA_KV_LORA, MLA_HEADS, MLA_NOPE + MLA_V)
    w_kn = w_kv[..., :MLA_NOPE].reshape(MLA_KV_LORA, MLA_HEADS * MLA_NOPE)
    w_vt = w_kv[..., MLA_NOPE:].reshape(MLA_KV_LORA, MLA_HEADS * MLA_V).T
    g_rope = g_kn_mla[MLA_NOPE:]
    n_a = MLA_HEADS * MLA_V
    logit_span = 2.0 * MLA_QK ** 0.5 * LOG2E * jnp.max(jnp.abs(g_qn_mla)) * jnp.max(jnp.abs(g_kn_mla))
    return {
        "mla_logit_span": logit_span,
        "g_attn": row(g_attn), "w_tm": w_tm.astype(BF16), "w_fm": w_fm.astype(BF16),
        "g_qa": row(g_qa), "w_qbt": w_qb.T.astype(BF16),
        "g_kva": row(g_kva), "w_kn": w_kn.astype(BF16), "w_vt": w_vt.astype(BF16),
        "g_qn_mla": g_qn_mla, "gk_a": row(g_kn_mla[:MLA_NOPE]),
        "gk_b": row(jnp.concatenate([g_rope, _swap_halves(g_rope)])),
        "g_q_swa": g_q_swa, "g_k_swa": row(g_k_swa),
        "g_out_mla": row(g_out_mla), "g_out_swa": row(g_out_swa),
        "w_out_a": w_out[:n_a].astype(BF16), "w_out_b": w_out[n_a:].astype(BF16),
        "g_mlp": row(g_mlp), "w_up": w_up.astype(BF16), "w_down": w_down.astype(BF16),
        "g_ple": row(g_ple), "w_gate": w_gate.astype(BF16), "w_ple": w_ple.astype(BF16),
    }


def _layer(x, p, sink, wts, tabs):
    batch, seq, d = x.shape
    n = batch * seq
    xf = x.reshape(n, d)
    qt, k, vt, qst, ks, vst = _proj(xf, seq, wts, tabs, tm=512)
    o_a = lax.cond(
        wts["mla_logit_span"] <= SAFE_LOG2_RANGE,
        functools.partial(_mla_attention, batch=batch, seq=seq, tq=1024, tk=8192, ck=256, track_max=False),
        functools.partial(_mla_attention, batch=batch, seq=seq, tq=512, tk=512, ck=512, track_max=True),
        qt, k, vt)
    o_b = _swa_attention(sink, qst, ks, vst, batch, seq, tq=1024, sub=256)
    x1, hm = _out_proj(xf, o_a, o_b, wts, tm=512)
    x2 = _mlp(x1, hm, wts["w_up"], wts["w_down"], tm=1024, tf=1024)
    y = _ple(x2, p.reshape(n, -1), wts, tm=512)
    return y.reshape(batch, seq, d)


def kernel(x_prompt, x_sample, p_prompt, p_sample, g_attn, w_in, g_qa, w_qb, g_kva, w_kvb, g_qn_mla, g_kn_mla,
           g_q_swa, g_k_swa, sink, g_out_mla, g_out_swa, w_out, g_mlp, w_up, w_down, g_ple, w_ple, w_gate):
    y_prompt, y_sample = x_prompt, x_sample
    tabs = _rope_inputs(max(x_prompt.shape[1], x_sample.shape[1]))
    for l in range(g_attn.shape[0]):
        wts = _prepare_weights(g_attn[l], w_in[l], g_qa[l], w_qb[l], g_kva[l], w_kvb[l], g_qn_mla[l],
                               g_kn_mla[l], g_q_swa[l], g_k_swa[l], g_out_mla[l], g_out_swa[l], w_out[l],
                               g_mlp[l], w_up[l], w_down[l], g_ple[l], w_ple[l], w_gate[l])
        y_prompt = _layer(y_prompt, p_prompt[l], sink[l], wts, tabs)
        y_sample = _layer(y_sample, p_sample[l], sink[l], wts, tabs)
    return (y_prompt, y_sample)
```

```python
import functools
import math

import jax
import jax.numpy as jnp
from jax import lax
from jax.experimental import pallas as pl
from jax.experimental.pallas import tpu as pltpu

EPS = 1e-6
ROPE_THETA = 10000.0
WINDOW = 128
MLA_HEADS = 8
MLA_Q_LORA = 512
MLA_KV_LORA = 256
MLA_NOPE = 128
MLA_ROPE = 64
MLA_V = 128
MLA_QK = MLA_NOPE + MLA_ROPE
SWA_HEADS = 8
SWA_KV_HEADS = 2
SWA_GROUP = SWA_HEADS // SWA_KV_HEADS
SWA_HEAD_DIM = 128
LANES = 128
SUBLANES = 8
MLA_PAD = 2 * LANES
SAFE_LOG2_RANGE = 100.0
LOG2E = math.log2(math.e)
VMEM_LIMIT = 56 * 1024 * 1024

BF16 = jnp.bfloat16
F32 = jnp.float32
STREAM = jnp.bfloat16
NT = (((1,), (1,)), ((), ()))


def _params(*sem):
    return pltpu.CompilerParams(dimension_semantics=sem, vmem_limit_bytes=VMEM_LIMIT)


def _resident(shape):
    nd = len(shape)
    return pl.BlockSpec(shape, lambda *_: (0,) * nd, pipeline_mode=pl.Buffered(1))


def _rms(x, g):
    return x * lax.rsqrt(jnp.mean(x * x, axis=-1, keepdims=True) + EPS) * g


def _rope_lanes(x, t_cos, t_sin):
    return x * t_cos + pltpu.roll(x, LANES // 2, axis=1) * t_sin


def _rope_rows(x1, x2, cos, sin):
    return x1 * cos - x2 * sin, x2 * cos + x1 * sin


def _proj_kernel(x_ref, g_attn_ref, w_tm_ref, w_fm_ref, g_qa_ref, w_qbt_ref, g_kva_ref, w_kn_ref, w_vt_ref,
                 gq_ref, gk_a_ref, gk_b_ref, gqs_ref, gks_ref,
                 mc_ref, ms_ref, sc_ref, ss_ref, tkc_ref, tks_ref, tsc_ref, tss_ref,
                 qt_ref, k_ref, vt_ref, qst_ref, ks_ref, vst_ref):
    tm = x_ref.shape[0]
    h = _rms(x_ref[...], g_attn_ref[...]).astype(BF16)
    p_tm = jnp.dot(h, w_tm_ref[...], preferred_element_type=F32)
    o_ckv = MLA_Q_LORA
    o_kpe = o_ckv + MLA_KV_LORA
    o_ks = o_kpe + LANES

    k_bound = MLA_QK ** 0.5 * jnp.maximum(jnp.max(jnp.abs(gk_a_ref[...]), axis=-1, keepdims=True),
                                          jnp.max(jnp.abs(gk_b_ref[...]), axis=-1, keepdims=True))
    shift_lane = (lax.broadcasted_iota(jnp.int32, (1, LANES), 1) == MLA_QK - LANES).astype(F32)
    first_row = lax.broadcasted_iota(jnp.int32, (MLA_PAD - MLA_QK, tm), 0) == 0

    cqn = _rms(p_tm[:, :o_ckv], g_qa_ref[...]).astype(BF16)
    q_t = lax.dot_general(w_qbt_ref[...], cqn, NT, preferred_element_type=F32)
    q_scale = MLA_QK ** -0.5 * LOG2E
    half = MLA_ROPE // 2
    for hd in range(MLA_HEADS):
        blk = q_t[hd * MLA_QK:(hd + 1) * MLA_QK]
        r = lax.rsqrt(jnp.sum(blk * blk, axis=0, keepdims=True) * (1.0 / MLA_QK) + EPS) * q_scale
        y = blk * gq_ref[...] * r
        shift = -(jnp.sqrt(jnp.sum(y * y, axis=0, keepdims=True)) * k_bound)
        o1, o2 = _rope_rows(y[MLA_NOPE:MLA_NOPE + half], y[MLA_NOPE + half:], mc_ref[...], ms_ref[...])
        qt_ref[hd, :MLA_NOPE, :] = y[:MLA_NOPE].astype(BF16)
        qt_ref[hd, MLA_NOPE:MLA_NOPE + half, :] = o1.astype(BF16)
        qt_ref[hd, MLA_NOPE + half:MLA_QK, :] = o2.astype(BF16)
        qt_ref[hd, MLA_QK:, :] = jnp.where(first_row, shift, 0.0).astype(BF16)

    ckvn = _rms(p_tm[:, o_ckv:o_kpe], g_kva_ref[...]).astype(BF16)
    kn_all = jnp.dot(ckvn, w_kn_ref[...], preferred_element_type=F32)
    kb = p_tm[:, o_kpe:o_ks]
    s_pe = 0.5 * jnp.sum(kb * kb, axis=-1, keepdims=True)
    kr = _rope_lanes(kb * gk_b_ref[...], tkc_ref[...], tks_ref[...])
    for hd in range(MLA_HEADS):
        kn = kn_all[:, hd * MLA_NOPE:(hd + 1) * MLA_NOPE]
        r = lax.rsqrt((jnp.sum(kn * kn, axis=-1, keepdims=True) + s_pe) * (1.0 / MLA_QK) + EPS)
        k_ref[hd, :, :LANES] = (kn * gk_a_ref[...] * r).astype(BF16)
        k_ref[hd, :, LANES:] = (kr * r + shift_lane).astype(BF16)
    v_t = lax.dot_general(w_vt_ref[...], ckvn, NT, preferred_element_type=F32)
    for hd in range(MLA_HEADS):
        vt_ref[hd] = v_t[hd * MLA_V:(hd + 1) * MLA_V].astype(BF16)

    p_fm = lax.dot_general(w_fm_ref[...], h, NT, preferred_element_type=F32)
    s_scale = SWA_HEAD_DIM ** -0.5 * LOG2E
    sh = SWA_HEAD_DIM // 2
    for hd in range(SWA_HEADS):
        blk = p_fm[hd * SWA_HEAD_DIM:(hd + 1) * SWA_HEAD_DIM]
        r = lax.rsqrt(jnp.mean(blk * blk, axis=0, keepdims=True) + EPS) * s_scale
        y = blk * gqs_ref[...] * r
        o1, o2 = _rope_rows(y[:sh], y[sh:], sc_ref[...], ss_ref[...])
        qst_ref[hd * SWA_HEAD_DIM:hd * SWA_HEAD_DIM + sh, :] = o1.astype(BF16)
        qst_ref[hd * SWA_HEAD_DIM + sh:(hd + 1) * SWA_HEAD_DIM, :] = o2.astype(BF16)
    vst_ref[...] = p_fm[SWA_HEADS * SWA_HEAD_DIM:].astype(BF16)
    for hd in range(SWA_KV_HEADS):
        xs = p_tm[:, o_ks + hd * SWA_HEAD_DIM: o_ks + (hd + 1) * SWA_HEAD_DIM]
        kn = _rope_lanes(_rms(xs, gks_ref[...]), tsc_ref[...], tss_ref[...])
        ks_ref[:, hd * SWA_HEAD_DIM:(hd + 1) * SWA_HEAD_DIM] = kn.astype(BF16)


def _rope_inputs(seq):
    fm_tabs, tm_tabs = [], []
    for dim in (MLA_ROPE, SWA_HEAD_DIM):
        half = dim // 2
        inv = 1.0 / (ROPE_THETA ** (jnp.arange(0, dim, 2, dtype=F32) / dim))
        fm_tabs += list(_cos_sin(inv[:, None], seq, axis=1))
        pad = jnp.zeros((LANES - dim,), F32)
        cos, sin = _cos_sin(jnp.concatenate([inv, inv, pad])[None, :], seq, axis=0)
        ones = jnp.ones((half,), F32)
        tm_tabs += [cos * jnp.concatenate([ones, ones, pad]), sin * jnp.concatenate([-ones, ones, pad])]
    return fm_tabs, tm_tabs


def _cos_sin(inv, seq, axis):
    blk = LANES
    assert seq % blk == 0
    shape = [1, 1]
    shape[axis] = -1
    pos_a = (jnp.arange(seq // blk, dtype=F32) * blk).reshape(shape)
    pos_b = jnp.arange(blk, dtype=F32).reshape(shape)
    ca, sa = jnp.cos(pos_a * inv), jnp.sin(pos_a * inv)
    cb, sb = jnp.cos(pos_b * inv), jnp.sin(pos_b * inv)
    ca, sa = jnp.expand_dims(ca, axis + 1), jnp.expand_dims(sa, axis + 1)
    cb, sb = jnp.expand_dims(cb, axis), jnp.expand_dims(sb, axis)
    out_shape = list(inv.shape)
    out_shape[axis] = seq
    return (ca * cb - sa * sb).reshape(out_shape), (sa * cb + ca * sb).reshape(out_shape)


def _proj(x, seq, wts, tabs, tm):
    n, d = x.shape
    tm = min(tm, seq)
    nt_seq = seq // tm
    row = lambda i: (i, 0)
    col = lambda i: (0, i)
    bcast = lambda g: jnp.broadcast_to(g.reshape(-1, 1), (g.size, tm))
    consts = [wts["g_attn"], wts["w_tm"], wts["w_fm"], wts["g_qa"], wts["w_qbt"], wts["g_kva"], wts["w_kn"],
              wts["w_vt"], bcast(wts["g_qn_mla"]), wts["gk_a"], wts["gk_b"], bcast(wts["g_q_swa"]), wts["g_k_swa"]]
    fm_tabs, tm_tabs = tabs
    in_specs = ([pl.BlockSpec((tm, d), row)] + [_resident(a.shape) for a in consts]
                + [pl.BlockSpec((t.shape[0], tm), lambda i: (0, i % nt_seq)) for t in fm_tabs]
                + [pl.BlockSpec((tm, LANES), lambda i: (i % nt_seq, 0))] * 4)
    n_qs = SWA_HEADS * SWA_HEAD_DIM
    n_ks = SWA_KV_HEADS * SWA_HEAD_DIM
    out_shape = (
        jax.ShapeDtypeStruct((MLA_HEADS, MLA_PAD, n), BF16),
        jax.ShapeDtypeStruct((MLA_HEADS, n, MLA_PAD), BF16),
        jax.ShapeDtypeStruct((MLA_HEADS, MLA_V, n), BF16),
        jax.ShapeDtypeStruct((n_qs, n), BF16),
        jax.ShapeDtypeStruct((n, n_ks), BF16),
        jax.ShapeDtypeStruct((n_ks, n), BF16),
    )
    out_specs = (
        pl.BlockSpec((MLA_HEADS, MLA_PAD, tm), lambda i: (0, 0, i)),
        pl.BlockSpec((MLA_HEADS, tm, MLA_PAD), lambda i: (0, i, 0)),
        pl.BlockSpec((MLA_HEADS, MLA_V, tm), lambda i: (0, 0, i)),
        pl.BlockSpec((n_qs, tm), col),
        pl.BlockSpec((tm, n_ks), row),
        pl.BlockSpec((n_ks, tm), col),
    )
    return pl.pallas_call(
        _proj_kernel, grid=(n // tm,), in_specs=in_specs, out_specs=out_specs,
        out_shape=out_shape, compiler_params=_params("parallel"), name="proj",
    )(x, *consts, *fm_tabs, *tm_tabs)


def _mla_kernel(qt_ref, k_ref, vt_ref, o_ref, *scratch, ck, track_max):
    j = pl.program_id(3)
    hb, tk, tq = k_ref.shape[0], k_ref.shape[1], qt_ref.shape[2]
    if track_max:
        m_ref, l_ref, acc_ref = scratch
    else:
        l_ref, acc_ref, p_ref = scratch

    @pl.when(j == 0)
    def _():
        if track_max:
            m_ref[...] = jnp.full_like(m_ref, -jnp.inf)
        l_ref[...] = jnp.zeros_like(l_ref)
        acc_ref[...] = jnp.zeros_like(acc_ref)

    for hh in range(hb):
        qt = qt_ref[hh]
        if track_max:
            s = jnp.dot(k_ref[hh], qt, preferred_element_type=F32)
            m_prev = m_ref[hh]
            m_new = jnp.maximum(m_prev, jnp.max(s, axis=0, keepdims=True))
            alpha = jnp.exp2(m_prev - m_new)
            p = jnp.exp2(s - m_new)
            l_ref[hh] = alpha * l_ref[hh] + jnp.sum(p, axis=0, keepdims=True)
            acc_ref[hh] = alpha * acc_ref[hh] + jnp.dot(vt_ref[hh], p.astype(BF16), preferred_element_type=F32)
            m_ref[hh] = m_new
        else:
            l_part = l_ref[hh]
            for c in range(tk // ck):
                s = jnp.dot(k_ref[hh, c * ck:(c + 1) * ck, :], qt, preferred_element_type=F32)
                p = jnp.exp2(s)
                l_part = l_part + jnp.sum(p.reshape(ck // SUBLANES, SUBLANES, tq), axis=0)
                p_ref[hh, c * ck:(c + 1) * ck, :] = p.astype(BF16)
            l_ref[hh] = l_part
            acc_ref[hh] += jnp.dot(vt_ref[hh], p_ref[hh], preferred_element_type=F32)

    @pl.when(j == pl.num_programs(3) - 1)
    def _():
        for hh in range(hb):
            l = jnp.sum(l_ref[hh], axis=0, keepdims=True)
            o_ref[:, hh * MLA_V:(hh + 1) * MLA_V] = (acc_ref[hh] / l).T.astype(o_ref.dtype)


def _mla_attention(qt, k, vt, batch, seq, tq, tk, ck, track_max, step_logits):
    n = batch * seq
    tq, tk = min(tq, seq), min(tk, seq)
    nq, nk = seq // tq, seq // tk
    hb = min(MLA_HEADS, max(1, step_logits // (tq * tk)))
    if track_max:
        scratch = [pltpu.VMEM((hb, 1, tq), F32), pltpu.VMEM((hb, 1, tq), F32), pltpu.VMEM((hb, MLA_V, tq), F32)]
    else:
        scratch = [pltpu.VMEM((hb, SUBLANES, tq), F32), pltpu.VMEM((hb, MLA_V, tq), F32),
                   pltpu.VMEM((hb, tk, tq), BF16)]
    return pl.pallas_call(
        functools.partial(_mla_kernel, ck=min(ck, tk), track_max=track_max),
        grid=(batch, MLA_HEADS // hb, nq, nk),
        in_specs=[
            pl.BlockSpec((hb, MLA_PAD, tq), lambda b, h, i, j: (h, 0, b * nq + i)),
            pl.BlockSpec((hb, tk, MLA_PAD), lambda b, h, i, j: (h, b * nk + j, 0)),
            pl.BlockSpec((hb, MLA_V, tk), lambda b, h, i, j: (h, 0, b * nk + j)),
        ],
        out_specs=pl.BlockSpec((tq, hb * MLA_V), lambda b, h, i, j: (b * nq + i, h)),
        out_shape=jax.ShapeDtypeStruct((n, MLA_HEADS * MLA_V), BF16),
        scratch_shapes=scratch,
        compiler_params=_params("parallel", "parallel", "parallel", "arbitrary"),
        name="mla_attn_online" if track_max else "mla_attn",
    )(qt, k, vt)


def _swa_kernel(sink_ref, qt_ref, kp_ref, km_ref, kn_ref, vp_ref, vm_ref, vn_ref, o_ref, kwin_ref, vwin_ref,
                *, nq, sub):
    g = pl.program_id(1)
    i = pl.program_id(2)
    tq = qt_ref.shape[1]
    kwin_ref[:WINDOW] = kp_ref[...]
    kwin_ref[WINDOW:WINDOW + tq] = km_ref[...]
    kwin_ref[WINDOW + tq:] = kn_ref[...]
    vwin_ref[:, :WINDOW] = vp_ref[...]
    vwin_ref[:, WINDOW:WINDOW + tq] = vm_ref[...]
    vwin_ref[:, WINDOW + tq:] = vn_ref[...]
    heads = range(SWA_GROUP)
    ws = sub + 2 * WINDOW
    sink = jnp.concatenate(
        [jnp.full((1, sub), sink_ref[g * SWA_GROUP + hd] * LOG2E, F32) for hd in heads], axis=1)
    key = lax.broadcasted_iota(jnp.int32, (ws, SWA_GROUP * sub), 0)
    qry = lax.broadcasted_iota(jnp.int32, (ws, SWA_GROUP * sub), 1) & (sub - 1)
    band = (key >= qry) & (key <= qry + 2 * WINDOW)
    no_prev = jnp.where(i > 0, 0.0, -jnp.inf)
    no_next = jnp.where(i < nq - 1, 0.0, -jnp.inf)
    for t in range(tq // sub):
        q_all = jnp.concatenate(
            [qt_ref[hd * SWA_HEAD_DIM:(hd + 1) * SWA_HEAD_DIM, t * sub:(t + 1) * sub] for hd in heads], axis=1)
        s = jnp.dot(kwin_ref[t * sub:t * sub + ws, :], q_all, preferred_element_type=F32)
        s = jnp.where(band, s, -jnp.inf)
        if t == 0:
            s = jnp.concatenate([s[:WINDOW] + no_prev, s[WINDOW:]], axis=0)
        if t == tq // sub - 1:
            s = jnp.concatenate([s[:ws - WINDOW], s[ws - WINDOW:] + no_next], axis=0)
        m = jnp.maximum(jnp.max(s, axis=0, keepdims=True), sink)
        p = jnp.exp2(s - m)
        denom = jnp.sum(p, axis=0, keepdims=True) + jnp.exp2(sink - m)
        o_t = jnp.dot(vwin_ref[:, t * sub:t * sub + ws], p.astype(BF16), preferred_element_type=F32) / denom
        for hd in heads:
            o_ref[t * sub:(t + 1) * sub, hd * SWA_HEAD_DIM:(hd + 1) * SWA_HEAD_DIM] = (
                o_t[:, hd * sub:(hd + 1) * sub].T.astype(o_ref.dtype))


def _swa_attention(sink, qt, k, vt, batch, seq, tq, sub):
    n = batch * seq
    tq = min(tq, seq)
    sub = min(sub, tq)
    assert sub & (sub - 1) == 0 and tq % sub == 0, "query index within a sub-tile is taken with a bit mask"
    nq = seq // tq
    r = tq // WINDOW
    nb = seq // WINDOW
    gw = SWA_GROUP * SWA_HEAD_DIM
    prev = lambda b, i: b * nb + jnp.maximum(i * r - 1, 0)
    nxt = lambda b, i: b * nb + jnp.minimum((i + 1) * r, nb - 1)
    edge = (WINDOW, SWA_HEAD_DIM)
    return pl.pallas_call(
        functools.partial(_swa_kernel, nq=nq, sub=sub),
        grid=(batch, SWA_KV_HEADS, nq),
        in_specs=[
            pl.BlockSpec(memory_space=pltpu.SMEM),
            pl.BlockSpec((gw, tq), lambda b, g, i: (g, b * nq + i)),
            pl.BlockSpec(edge, lambda b, g, i: (prev(b, i), g)),
            pl.BlockSpec((tq, SWA_HEAD_DIM), lambda b, g, i: (b * nq + i, g)),
            pl.BlockSpec(edge, lambda b, g, i: (nxt(b, i), g)),
            pl.BlockSpec(edge, lambda b, g, i: (g, prev(b, i))),
            pl.BlockSpec((SWA_HEAD_DIM, tq), lambda b, g, i: (g, b * nq + i)),
            pl.BlockSpec(edge, lambda b, g, i: (g, nxt(b, i))),
        ],
        out_specs=pl.BlockSpec((tq, gw), lambda b, g, i: (b * nq + i, g)),
        out_shape=jax.ShapeDtypeStruct((n, SWA_HEADS * SWA_HEAD_DIM), BF16),
        scratch_shapes=[pltpu.VMEM((tq + 2 * WINDOW, SWA_HEAD_DIM), BF16),
                        pltpu.VMEM((SWA_HEAD_DIM, tq + 2 * WINDOW), BF16)],
        compiler_params=_params("parallel", "parallel", "parallel"),
        name="swa_attn",
    )(sink, qt, k, k, k, vt, vt, vt)


def _out_proj_kernel(x_ref, oa_ref, ob_ref, ga_ref, gb_ref, wa_ref, wb_ref, g_mlp_ref, x1_ref, hm_ref):
    oa = _rms(oa_ref[...].astype(F32), ga_ref[...]).astype(BF16)
    ob = _rms(ob_ref[...].astype(F32), gb_ref[...]).astype(BF16)
    x1 = (x_ref[...] + jnp.dot(oa, wa_ref[...], preferred_element_type=F32)
          + jnp.dot(ob, wb_ref[...], preferred_element_type=F32))
    x1_ref[...] = x1.astype(x1_ref.dtype)
    hm_ref[...] = _rms(x1, g_mlp_ref[...]).astype(BF16)


def _out_proj(x, oa, ob, wts, tm):
    n, d = x.shape
    tm = min(tm, n)
    row = lambda i: (i, 0)
    consts = [wts[k] for k in ("g_out_mla", "g_out_swa", "w_out_a", "w_out_b", "g_mlp")]
    return pl.pallas_call(
        _out_proj_kernel, grid=(n // tm,),
        in_specs=[pl.BlockSpec((tm, d), row), pl.BlockSpec((tm, oa.shape[1]), row),
                  pl.BlockSpec((tm, ob.shape[1]), row)] + [_resident(a.shape) for a in consts],
        out_specs=(pl.BlockSpec((tm, d), row), pl.BlockSpec((tm, d), row)),
        out_shape=(jax.ShapeDtypeStruct((n, d), STREAM), jax.ShapeDtypeStruct((n, d), BF16)),
        compiler_params=_params("parallel"), name="out_proj",
    )(x, oa, ob, *consts)


def _mlp_kernel(x1_ref, hm_ref, wu_ref, wd_ref, o_ref):
    @pl.when(pl.program_id(1) == 0)
    def _():
        o_ref[...] = x1_ref[...].astype(F32)

    up = jnp.dot(hm_ref[...], wu_ref[...], preferred_element_type=F32)
    act = jnp.square(jnp.maximum(up, 0.0)).astype(BF16)
    o_ref[...] += jnp.dot(act, wd_ref[...], preferred_element_type=F32)


def _mlp(x1, hm, w_up, w_down, tm, tf):
    n, d = x1.shape
    f = w_up.shape[1]
    tm, tf = min(tm, n), min(tf, f)
    row = lambda i, j: (i, 0)
    early = lambda i, j: (jnp.minimum(i + jnp.where(j >= 2, 1, 0), n // tm - 1), 0)
    return pl.pallas_call(
        _mlp_kernel, grid=(n // tm, f // tf),
        in_specs=[pl.BlockSpec((tm, d), early), pl.BlockSpec((tm, d), row),
                  pl.BlockSpec((d, tf), lambda i, j: (0, j)), pl.BlockSpec((tf, d), lambda i, j: (j, 0))],
        out_specs=pl.BlockSpec((tm, d), row),
        out_shape=jax.ShapeDtypeStruct((n, d), F32),
        compiler_params=_params("parallel", "arbitrary"), name="mlp",
    )(x1, hm, w_up, w_down)


def _ple_kernel(x_ref, p_ref, g_ref, wg_ref, wp_ref, o_ref):
    x = x_ref[...]
    z = jnp.dot(_rms(x, g_ref[...]).astype(BF16), wg_ref[...], preferred_element_type=F32)
    gate = 1.0 / (1.0 + jnp.exp(-z))
    e = jnp.dot(p_ref[...].astype(BF16), wp_ref[...], preferred_element_type=F32)
    o_ref[...] = x + e * gate


def _ple(x2, p, wts, tm):
    n, d = x2.shape
    tm = min(tm, n)
    row = lambda i: (i, 0)
    consts = [wts[k] for k in ("g_ple", "w_gate", "w_ple")]
    return pl.pallas_call(
        _ple_kernel, grid=(n // tm,),
        in_specs=[pl.BlockSpec((tm, d), row), pl.BlockSpec((tm, p.shape[1]), row)]
                 + [_resident(a.shape) for a in consts],
        out_specs=pl.BlockSpec((tm, d), row),
        out_shape=jax.ShapeDtypeStruct((n, d), F32),
        compiler_params=_params("parallel"), name="ple",
    )(x2, p, *consts)


def _swap_halves(a):
    half = a.shape[-1] // 2
    return jnp.concatenate([a[..., half:], a[..., :half]], axis=-1)


def _prepare_weights(g_attn, w_in, g_qa, w_qb, g_kva, w_kvb, g_qn_mla, g_kn_mla, g_q_swa, g_k_swa,
                     g_out_mla, g_out_swa, w_out, g_mlp, w_up, w_down, g_ple, w_ple, w_gate):
    row = lambda g: g.reshape(1, -1)
    o_kpe = MLA_Q_LORA + MLA_KV_LORA
    o_qs = o_kpe + MLA_ROPE
    o_ks = o_qs + SWA_HEADS * SWA_HEAD_DIM
    o_vs = o_ks + SWA_KV_HEADS * SWA_HEAD_DIM
    kpe = w_in[:, o_kpe:o_qs]
    w_tm = jnp.concatenate([w_in[:, :o_kpe], kpe, _swap_halves(kpe), w_in[:, o_ks:o_vs]], axis=1)
    w_fm = jnp.concatenate([w_in[:, o_qs:o_ks], w_in[:, o_vs:]], axis=1).T
    w_kv = w_kvb.reshape(MLA_KV_LORA, MLA_HEADS, MLA_NOPE + MLA_V)
    w_kn = w_kv[..., :MLA_NOPE].reshape(MLA_KV_LORA, MLA_HEADS * MLA_NOPE)
    w_vt = w_kv[..., MLA_NOPE:].reshape(MLA_KV_LORA, MLA_HEADS * MLA_V).T
    g_rope = g_kn_mla[MLA_NOPE:]
    n_a = MLA_HEADS * MLA_V
    logit_span = 2.0 * MLA_QK ** 0.5 * LOG2E * jnp.max(jnp.abs(g_qn_mla)) * jnp.max(jnp.abs(g_kn_mla))
    return {
        "mla_logit_span": logit_span,
        "g_attn": row(g_attn), "w_tm": w_tm.astype(BF16), "w_fm": w_fm.astype(BF16),
        "g_qa": row(g_qa), "w_qbt": w_qb.T.astype(BF16),
        "g_kva": row(g_kva), "w_kn": w_kn.astype(BF16), "w_vt": w_vt.astype(BF16),
        "g_qn_mla": g_qn_mla, "gk_a": row(g_kn_mla[:MLA_NOPE]),
        "gk_b": row(jnp.concatenate([g_rope, _swap_halves(g_rope)])),
        "g_q_swa": g_q_swa, "g_k_swa": row(g_k_swa),
        "g_out_mla": row(g_out_mla), "g_out_swa": row(g_out_swa),
        "w_out_a": w_out[:n_a].astype(BF16), "w_out_b": w_out[n_a:].astype(BF16),
        "g_mlp": row(g_mlp), "w_up": w_up.astype(BF16), "w_down": w_down.astype(BF16),
        "g_ple": row(g_ple), "w_gate": w_gate.astype(BF16), "w_ple": w_ple.astype(BF16),
    }


def _layer(x, p, sink, wts, tabs):
    batch, seq, d = x.shape
    n = batch * seq
    xf = x.reshape(n, d)
    qt, k, vt, qst, ks, vst = _proj(xf, seq, wts, tabs, tm=512)
    o_a = lax.cond(
        wts["mla_logit_span"] <= SAFE_LOG2_RANGE,
        functools.partial(_mla_attention, batch=batch, seq=seq, tq=1024, tk=4096, ck=256, track_max=False,
                          step_logits=4 * 4096 * 1024),
        functools.partial(_mla_attention, batch=batch, seq=seq, tq=512, tk=512, ck=512, track_max=True,
                          step_logits=512 * 512),
        qt, k, vt)
    o_b = _swa_attention(sink, qst, ks, vst, batch, seq, tq=1024, sub=256)
    x1, hm = _out_proj(xf, o_a, o_b, wts, tm=512)
    x2 = _mlp(x1, hm, wts["w_up"], wts["w_down"], tm=1024, tf=1024)
    y = _ple(x2, p.reshape(n, -1), wts, tm=512)
    return y.reshape(batch, seq, d)


def kernel(x_prompt, x_sample, p_prompt, p_sample, g_attn, w_in, g_qa, w_qb, g_kva, w_kvb, g_qn_mla, g_kn_mla,
           g_q_swa, g_k_swa, sink, g_out_mla, g_out_swa, w_out, g_mlp, w_up, w_down, g_ple, w_ple, w_gate):
    y_prompt, y_sample = x_prompt, x_sample
    tabs = _rope_inputs(max(x_prompt.shape[1], x_sample.shape[1]))
    for l in range(g_attn.shape[0]):
        wts = _prepare_weights(g_attn[l], w_in[l], g_qa[l], w_qb[l], g_kva[l], w_kvb[l], g_qn_mla[l],
                               g_kn_mla[l], g_q_swa[l], g_k_swa[l], g_out_mla[l], g_out_swa[l], w_out[l],
                               g_mlp[l], w_up[l], w_down[l], g_ple[l], w_ple[l], w_gate[l])
        y_prompt = _layer(y_prompt, p_prompt[l], sink[l], wts, tabs)
        y_sample = _layer(y_sample, p_sample[l], sink[l], wts, tabs)
    return (y_prompt, y_sample)
```

```python
import functools
import math

import jax
import jax.numpy as jnp
from jax import lax
from jax.experimental import pallas as pl
from jax.experimental.pallas import tpu as pltpu

EPS = 1e-6
ROPE_THETA = 10000.0
WINDOW = 128
MLA_HEADS = 8
MLA_Q_LORA = 512
MLA_KV_LORA = 256
MLA_NOPE = 128
MLA_ROPE = 64
MLA_V = 128
MLA_QK = MLA_NOPE + MLA_ROPE
SWA_HEADS = 8
SWA_KV_HEADS = 2
SWA_GROUP = SWA_HEADS // SWA_KV_HEADS
SWA_HEAD_DIM = 128
LANES = 128
SUBLANES = 8
MLA_PAD = 2 * LANES
SAFE_LOG2_RANGE = 100.0
LOG2E = math.log2(math.e)
VMEM_LIMIT = 56 * 1024 * 1024

BF16 = jnp.bfloat16
F32 = jnp.float32
STREAM = jnp.bfloat16
NT = (((1,), (1,)), ((), ()))


def _params(*sem):
    return pltpu.CompilerParams(dimension_semantics=sem, vmem_limit_bytes=VMEM_LIMIT)


def _resident(shape):
    nd = len(shape)
    return pl.BlockSpec(shape, lambda *_: (0,) * nd, pipeline_mode=pl.Buffered(1))


def _rms(x, g):
    return x * lax.rsqrt(jnp.mean(x * x, axis=-1, keepdims=True) + EPS) * g


def _rope_lanes(x, t_cos, t_sin):
    return x * t_cos + pltpu.roll(x, LANES // 2, axis=1) * t_sin


def _rope_rows(x1, x2, cos, sin):
    return x1 * cos - x2 * sin, x2 * cos + x1 * sin


def _proj_kernel(x_ref, w_tm_ref, w_fm_ref, g_qa_ref, w_qbt_ref, g_kva_ref, w_kn_ref, w_vt_ref,
                 gq_ref, gk_a_ref, gk_b_ref, gqs_ref, gks_ref,
                 mc_ref, ms_ref, sc_ref, ss_ref, tkc_ref, tks_ref, tsc_ref, tss_ref,
                 qt_ref, k_ref, vt_ref, qst_ref, ks_ref, vst_ref):
    tm = x_ref.shape[0]
    x = x_ref[...]
    h = x.astype(BF16)
    r_x = lax.rsqrt(jnp.mean(x * x, axis=-1, keepdims=True) + EPS)
    r_x_row = jnp.broadcast_to(r_x, (tm, LANES)).T[:1]
    p_tm = jnp.dot(h, w_tm_ref[...], preferred_element_type=F32) * r_x
    o_ckv = MLA_Q_LORA
    o_kpe = o_ckv + MLA_KV_LORA
    o_ks = o_kpe + LANES

    k_bound = MLA_QK ** 0.5 * jnp.maximum(jnp.max(jnp.abs(gk_a_ref[...]), axis=-1, keepdims=True),
                                          jnp.max(jnp.abs(gk_b_ref[...]), axis=-1, keepdims=True))
    shift_lane = (lax.broadcasted_iota(jnp.int32, (1, LANES), 1) == MLA_QK - LANES).astype(F32)
    first_row = lax.broadcasted_iota(jnp.int32, (MLA_PAD - MLA_QK, tm), 0) == 0

    cqn = _rms(p_tm[:, :o_ckv], g_qa_ref[...]).astype(BF16)
    q_t = lax.dot_general(w_qbt_ref[...], cqn, NT, preferred_element_type=F32)
    q_scale = MLA_QK ** -0.5 * LOG2E
    half = MLA_ROPE // 2
    for hd in range(MLA_HEADS):
        blk = q_t[hd * MLA_QK:(hd + 1) * MLA_QK]
        r = lax.rsqrt(jnp.sum(blk * blk, axis=0, keepdims=True) * (1.0 / MLA_QK) + EPS) * q_scale
        y = blk * gq_ref[...] * r
        shift = -(jnp.sqrt(jnp.sum(y * y, axis=0, keepdims=True)) * k_bound)
        o1, o2 = _rope_rows(y[MLA_NOPE:MLA_NOPE + half], y[MLA_NOPE + half:], mc_ref[...], ms_ref[...])
        qt_ref[hd, :MLA_NOPE, :] = y[:MLA_NOPE].astype(BF16)
        qt_ref[hd, MLA_NOPE:MLA_NOPE + half, :] = o1.astype(BF16)
        qt_ref[hd, MLA_NOPE + half:MLA_QK, :] = o2.astype(BF16)
        qt_ref[hd, MLA_QK:, :] = jnp.where(first_row, shift, 0.0).astype(BF16)

    ckvn = _rms(p_tm[:, o_ckv:o_kpe], g_kva_ref[...]).astype(BF16)
    kn_all = jnp.dot(ckvn, w_kn_ref[...], preferred_element_type=F32)
    kb = p_tm[:, o_kpe:o_ks]
    s_pe = 0.5 * jnp.sum(kb * kb, axis=-1, keepdims=True)
    kr = _rope_lanes(kb * gk_b_ref[...], tkc_ref[...], tks_ref[...])
    for hd in range(MLA_HEADS):
        kn = kn_all[:, hd * MLA_NOPE:(hd + 1) * MLA_NOPE]
        r = lax.rsqrt((jnp.sum(kn * kn, axis=-1, keepdims=True) + s_pe) * (1.0 / MLA_QK) + EPS)
        k_ref[hd, :, :LANES] = (kn * gk_a_ref[...] * r).astype(BF16)
        k_ref[hd, :, LANES:] = (kr * r + shift_lane).astype(BF16)
    v_t = lax.dot_general(w_vt_ref[...], ckvn, NT, preferred_element_type=F32)
    for hd in range(MLA_HEADS):
        vt_ref[hd] = v_t[hd * MLA_V:(hd + 1) * MLA_V].astype(BF16)

    p_fm = lax.dot_general(w_fm_ref[...], h, NT, preferred_element_type=F32) * r_x_row
    s_scale = SWA_HEAD_DIM ** -0.5 * LOG2E
    sh = SWA_HEAD_DIM // 2
    for hd in range(SWA_HEADS):
        blk = p_fm[hd * SWA_HEAD_DIM:(hd + 1) * SWA_HEAD_DIM]
        r = lax.rsqrt(jnp.mean(blk * blk, axis=0, keepdims=True) + EPS) * s_scale
        y = blk * gqs_ref[...] * r
        o1, o2 = _rope_rows(y[:sh], y[sh:], sc_ref[...], ss_ref[...])
        qst_ref[hd * SWA_HEAD_DIM:hd * SWA_HEAD_DIM + sh, :] = o1.astype(BF16)
        qst_ref[hd * SWA_HEAD_DIM + sh:(hd + 1) * SWA_HEAD_DIM, :] = o2.astype(BF16)
    vst_ref[...] = p_fm[SWA_HEADS * SWA_HEAD_DIM:].astype(BF16)
    for hd in range(SWA_KV_HEADS):
        xs = p_tm[:, o_ks + hd * SWA_HEAD_DIM: o_ks + (hd + 1) * SWA_HEAD_DIM]
        kn = _rope_lanes(_rms(xs, gks_ref[...]), tsc_ref[...], tss_ref[...])
        ks_ref[:, hd * SWA_HEAD_DIM:(hd + 1) * SWA_HEAD_DIM] = kn.astype(BF16)


def _rope_inputs(seq):
    fm_tabs, tm_tabs = [], []
    for dim in (MLA_ROPE, SWA_HEAD_DIM):
        half = dim // 2
        inv = 1.0 / (ROPE_THETA ** (jnp.arange(0, dim, 2, dtype=F32) / dim))
        fm_tabs += list(_cos_sin(inv[:, None], seq, axis=1))
        pad = jnp.zeros((LANES - dim,), F32)
        cos, sin = _cos_sin(jnp.concatenate([inv, inv, pad])[None, :], seq, axis=0)
        ones = jnp.ones((half,), F32)
        tm_tabs += [cos * jnp.concatenate([ones, ones, pad]), sin * jnp.concatenate([-ones, ones, pad])]
    return fm_tabs, tm_tabs


def _cos_sin(inv, seq, axis):
    blk = LANES
    assert seq % blk == 0
    shape = [1, 1]
    shape[axis] = -1
    pos_a = (jnp.arange(seq // blk, dtype=F32) * blk).reshape(shape)
    pos_b = jnp.arange(blk, dtype=F32).reshape(shape)
    ca, sa = jnp.cos(pos_a * inv), jnp.sin(pos_a * inv)
    cb, sb = jnp.cos(pos_b * inv), jnp.sin(pos_b * inv)
    ca, sa = jnp.expand_dims(ca, axis + 1), jnp.expand_dims(sa, axis + 1)
    cb, sb = jnp.expand_dims(cb, axis), jnp.expand_dims(sb, axis)
    out_shape = list(inv.shape)
    out_shape[axis] = seq
    return (ca * cb - sa * sb).reshape(out_shape), (sa * cb + ca * sb).reshape(out_shape)


def _proj(x, seq, wts, tabs, tm):
    n, d = x.shape
    tm = min(tm, seq)
    nt_seq = seq // tm
    row = lambda i: (i, 0)
    col = lambda i: (0, i)
    bcast = lambda g: jnp.broadcast_to(g.reshape(-1, 1), (g.size, tm))
    consts = [wts["w_tm"], wts["w_fm"], wts["g_qa"], wts["w_qbt"], wts["g_kva"], wts["w_kn"],
              wts["w_vt"], bcast(wts["g_qn_mla"]), wts["gk_a"], wts["gk_b"], bcast(wts["g_q_swa"]), wts["g_k_swa"]]
    fm_tabs, tm_tabs = tabs
    in_specs = ([pl.BlockSpec((tm, d), row)] + [_resident(a.shape) for a in consts]
                + [pl.BlockSpec((t.shape[0], tm), lambda i: (0, i % nt_seq)) for t in fm_tabs]
                + [pl.BlockSpec((tm, LANES), lambda i: (i % nt_seq, 0))] * 4)
    n_qs = SWA_HEADS * SWA_HEAD_DIM
    n_ks = SWA_KV_HEADS * SWA_HEAD_DIM
    out_shape = (
        jax.ShapeDtypeStruct((MLA_HEADS, MLA_PAD, n), BF16),
        jax.ShapeDtypeStruct((MLA_HEADS, n, MLA_PAD), BF16),
        jax.ShapeDtypeStruct((MLA_HEADS, MLA_V, n), BF16),
        jax.ShapeDtypeStruct((n_qs, n), BF16),
        jax.ShapeDtypeStruct((n, n_ks), BF16),
        jax.ShapeDtypeStruct((n_ks, n), BF16),
    )
    out_specs = (
        pl.BlockSpec((MLA_HEADS, MLA_PAD, tm), lambda i: (0, 0, i)),
        pl.BlockSpec((MLA_HEADS, tm, MLA_PAD), lambda i: (0, i, 0)),
        pl.BlockSpec((MLA_HEADS, MLA_V, tm), lambda i: (0, 0, i)),
        pl.BlockSpec((n_qs, tm), col),
        pl.BlockSpec((tm, n_ks), row),
        pl.BlockSpec((n_ks, tm), col),
    )
    return pl.pallas_call(
        _proj_kernel, grid=(n // tm,), in_specs=in_specs, out_specs=out_specs,
        out_shape=out_shape, compiler_params=_params("parallel"), name="proj",
    )(x, *consts, *fm_tabs, *tm_tabs)


def _mla_kernel(qt_ref, k_ref, vt_ref, o_ref, *scratch, ck, track_max):
    j = pl.program_id(3)
    hb, tk, tq = k_ref.shape[0], k_ref.shape[1], qt_ref.shape[2]
    if track_max:
        m_ref, l_ref, acc_ref = scratch
    else:
        l_ref, acc_ref, p_ref = scratch

    @pl.when(j == 0)
    def _():
        if track_max:
            m_ref[...] = jnp.full_like(m_ref, -jnp.inf)
        l_ref[...] = jnp.zeros_like(l_ref)
        acc_ref[...] = jnp.zeros_like(acc_ref)

    for hh in range(hb):
        qt = qt_ref[hh]
        if track_max:
            s = jnp.dot(k_ref[hh], qt, preferred_element_type=F32)
            m_prev = m_ref[hh]
            m_new = jnp.maximum(m_prev, jnp.max(s, axis=0, keepdims=True))
            alpha = jnp.exp2(m_prev - m_new)
            p = jnp.exp2(s - m_new)
            l_ref[hh] = alpha * l_ref[hh] + jnp.sum(p, axis=0, keepdims=True)
            acc_ref[hh] = alpha * acc_ref[hh] + jnp.dot(vt_ref[hh], p.astype(BF16), preferred_element_type=F32)
            m_ref[hh] = m_new
        else:
            l_part = l_ref[hh]
            for c in range(tk // ck):
                s = jnp.dot(k_ref[hh, c * ck:(c + 1) * ck, :], qt, preferred_element_type=F32)
                p = jnp.exp2(s)
                l_part = l_part + jnp.sum(p.reshape(ck // SUBLANES, SUBLANES, tq), axis=0)
                p_ref[hh, c * ck:(c + 1) * ck, :] = p.astype(BF16)
            l_ref[hh] = l_part
            acc_ref[hh] += jnp.dot(vt_ref[hh], p_ref[hh], preferred_element_type=F32)

    @pl.when(j == pl.num_programs(3) - 1)
    def _():
        for hh in range(hb):
            l = jnp.sum(l_ref[hh], axis=0, keepdims=True)
            o_ref[:, hh * MLA_V:(hh + 1) * MLA_V] = (acc_ref[hh] / l).T.astype(o_ref.dtype)


def _mla_attention(qt, k, vt, batch, seq, tq, tk, ck, track_max, step_logits):
    n = batch * seq
    tq, tk = min(tq, seq), min(tk, seq)
    nq, nk = seq // tq, seq // tk
    hb = min(MLA_HEADS, max(1, step_logits // (tq * tk)))
    if track_max:
        scratch = [pltpu.VMEM((hb, 1, tq), F32), pltpu.VMEM((hb, 1, tq), F32), pltpu.VMEM((hb, MLA_V, tq), F32)]
    else:
        scratch = [pltpu.VMEM((hb, SUBLANES, tq), F32), pltpu.VMEM((hb, MLA_V, tq), F32),
                   pltpu.VMEM((hb, tk, tq), BF16)]
    return pl.pallas_call(
        functools.partial(_mla_kernel, ck=min(ck, tk), track_max=track_max),
        grid=(batch, MLA_HEADS // hb, nq, nk),
        in_specs=[
            pl.BlockSpec((hb, MLA_PAD, tq), lambda b, h, i, j: (h, 0, b * nq + i)),
            pl.BlockSpec((hb, tk, MLA_PAD), lambda b, h, i, j: (h, b * nk + j, 0)),
            pl.BlockSpec((hb, MLA_V, tk), lambda b, h, i, j: (h, 0, b * nk + j)),
        ],
        out_specs=pl.BlockSpec((tq, hb * MLA_V), lambda b, h, i, j: (b * nq + i, h)),
        out_shape=jax.ShapeDtypeStruct((n, MLA_HEADS * MLA_V), BF16),
        scratch_shapes=scratch,
        compiler_params=_params("parallel", "parallel", "parallel", "arbitrary"),
        name="mla_attn_online" if track_max else "mla_attn",
    )(qt, k, vt)


def _swa_kernel(sink_ref, qt_ref, kp_ref, km_ref, kn_ref, vp_ref, vm_ref, vn_ref, o_ref, kwin_ref, vwin_ref,
                *, nq, sub):
    g = pl.program_id(1)
    i = pl.program_id(2)
    tq = qt_ref.shape[1]
    kwin_ref[:WINDOW] = kp_ref[...]
    kwin_ref[WINDOW:WINDOW + tq] = km_ref[...]
    kwin_ref[WINDOW + tq:] = kn_ref[...]
    vwin_ref[:, :WINDOW] = vp_ref[...]
    vwin_ref[:, WINDOW:WINDOW + tq] = vm_ref[...]
    vwin_ref[:, WINDOW + tq:] = vn_ref[...]
    heads = range(SWA_GROUP)
    ws = sub + 2 * WINDOW
    sink = jnp.concatenate(
        [jnp.full((1, sub), sink_ref[g * SWA_GROUP + hd] * LOG2E, F32) for hd in heads], axis=1)
    key = lax.broadcasted_iota(jnp.int32, (ws, SWA_GROUP * sub), 0)
    qry = lax.broadcasted_iota(jnp.int32, (ws, SWA_GROUP * sub), 1) & (sub - 1)
    band = (key >= qry) & (key <= qry + 2 * WINDOW)
    no_prev = jnp.where(i > 0, 0.0, -jnp.inf)
    no_next = jnp.where(i < nq - 1, 0.0, -jnp.inf)
    for t in range(tq // sub):
        q_all = jnp.concatenate(
            [qt_ref[hd * SWA_HEAD_DIM:(hd + 1) * SWA_HEAD_DIM, t * sub:(t + 1) * sub] for hd in heads], axis=1)
        s = jnp.dot(kwin_ref[t * sub:t * sub + ws, :], q_all, preferred_element_type=F32)
        s = jnp.where(band, s, -jnp.inf)
        if t == 0:
            s = jnp.concatenate([s[:WINDOW] + no_prev, s[WINDOW:]], axis=0)
        if t == tq // sub - 1:
            s = jnp.concatenate([s[:ws - WINDOW], s[ws - WINDOW:] + no_next], axis=0)
        m = jnp.maximum(jnp.max(s, axis=0, keepdims=True), sink)
        p = jnp.exp2(s - m)
        denom = jnp.sum(p, axis=0, keepdims=True) + jnp.exp2(sink - m)
        o_t = jnp.dot(vwin_ref[:, t * sub:t * sub + ws], p.astype(BF16), preferred_element_type=F32) / denom
        for hd in heads:
            o_ref[t * sub:(t + 1) * sub, hd * SWA_HEAD_DIM:(hd + 1) * SWA_HEAD_DIM] = (
                o_t[:, hd * sub:(hd + 1) * sub].T.astype(o_ref.dtype))


def _swa_attention(sink, qt, k, vt, batch, seq, tq, sub):
    n = batch * seq
    tq = min(tq, seq)
    sub = min(sub, tq)
    assert sub & (sub - 1) == 0 and tq % sub == 0, "query index within a sub-tile is taken with a bit mask"
    nq = seq // tq
    r = tq // WINDOW
    nb = seq // WINDOW
    gw = SWA_GROUP * SWA_HEAD_DIM
    prev = lambda b, i: b * nb + jnp.maximum(i * r - 1, 0)
    nxt = lambda b, i: b * nb + jnp.minimum((i + 1) * r, nb - 1)
    edge = (WINDOW, SWA_HEAD_DIM)
    return pl.pallas_call(
        functools.partial(_swa_kernel, nq=nq, sub=sub),
        grid=(batch, SWA_KV_HEADS, nq),
        in_specs=[
            pl.BlockSpec(memory_space=pltpu.SMEM),
            pl.BlockSpec((gw, tq), lambda b, g, i: (g, b * nq + i)),
            pl.BlockSpec(edge, lambda b, g, i: (prev(b, i), g)),
            pl.BlockSpec((tq, SWA_HEAD_DIM), lambda b, g, i: (b * nq + i, g)),
            pl.BlockSpec(edge, lambda b, g, i: (nxt(b, i), g)),
            pl.BlockSpec(edge, lambda b, g, i: (g, prev(b, i))),
            pl.BlockSpec((SWA_HEAD_DIM, tq), lambda b, g, i: (g, b * nq + i)),
            pl.BlockSpec(edge, lambda b, g, i: (g, nxt(b, i))),
        ],
        out_specs=pl.BlockSpec((tq, gw), lambda b, g, i: (b * nq + i, g)),
        out_shape=jax.ShapeDtypeStruct((n, SWA_HEADS * SWA_HEAD_DIM), BF16),
        scratch_shapes=[pltpu.VMEM((tq + 2 * WINDOW, SWA_HEAD_DIM), BF16),
                        pltpu.VMEM((SWA_HEAD_DIM, tq + 2 * WINDOW), BF16)],
        compiler_params=_params("parallel", "parallel", "parallel"),
        name="swa_attn",
    )(sink, qt, k, k, k, vt, vt, vt)


def _out_proj_kernel(x_ref, oa_ref, ob_ref, ga_ref, gb_ref, wa_ref, wb_ref, x1_ref):
    oa = _rms(oa_ref[...].astype(F32), ga_ref[...]).astype(BF16)
    ob = _rms(ob_ref[...].astype(F32), gb_ref[...]).astype(BF16)
    x1 = (x_ref[...] + jnp.dot(oa, wa_ref[...], preferred_element_type=F32)
          + jnp.dot(ob, wb_ref[...], preferred_element_type=F32))
    x1_ref[...] = x1.astype(x1_ref.dtype)


def _out_proj(x, oa, ob, wts, tm):
    n, d = x.shape
    tm = min(tm, n)
    row = lambda i: (i, 0)
    consts = [wts[k] for k in ("g_out_mla", "g_out_swa", "w_out_a", "w_out_b")]
    return pl.pallas_call(
        _out_proj_kernel, grid=(n // tm,),
        in_specs=[pl.BlockSpec((tm, d), row), pl.BlockSpec((tm, oa.shape[1]), row),
                  pl.BlockSpec((tm, ob.shape[1]), row)] + [_resident(a.shape) for a in consts],
        out_specs=pl.BlockSpec((tm, d), row),
        out_shape=jax.ShapeDtypeStruct((n, d), STREAM),
        compiler_params=_params("parallel"), name="out_proj",
    )(x, oa, ob, *consts)


def _mlp_kernel(x1_ref, wu_ref, wd_ref, o_ref, r_ref):
    @pl.when(pl.program_id(1) == 0)
    def _():
        x1 = x1_ref[...].astype(F32)
        r_ref[...] = lax.rsqrt(jnp.mean(x1 * x1, axis=-1, keepdims=True) + EPS)
        o_ref[...] = x1

    up = jnp.dot(x1_ref[...], wu_ref[...], preferred_element_type=F32) * r_ref[...]
    act = jnp.square(jnp.maximum(up, 0.0)).astype(BF16)
    o_ref[...] += jnp.dot(act, wd_ref[...], preferred_element_type=F32)


def _mlp(x1, w_up, w_down, tm, tf):
    n, d = x1.shape
    f = w_up.shape[1]
    tm, tf = min(tm, n), min(tf, f)
    row = lambda i, j: (i, 0)
    return pl.pallas_call(
        _mlp_kernel, grid=(n // tm, f // tf),
        in_specs=[pl.BlockSpec((tm, d), row),
                  pl.BlockSpec((d, tf), lambda i, j: (0, j)), pl.BlockSpec((tf, d), lambda i, j: (j, 0))],
        out_specs=pl.BlockSpec((tm, d), row),
        out_shape=jax.ShapeDtypeStruct((n, d), F32),
        scratch_shapes=[pltpu.VMEM((tm, 1), F32)],
        compiler_params=_params("parallel", "arbitrary"), name="mlp",
    )(x1, w_up, w_down)


def _ple_kernel(x_ref, p_ref, wg_ref, wp_ref, o_ref):
    x = x_ref[...]
    r = lax.rsqrt(jnp.mean(x * x, axis=-1, keepdims=True) + EPS)
    z = jnp.dot(x.astype(BF16), wg_ref[...], preferred_element_type=F32) * r
    gate = 1.0 / (1.0 + jnp.exp(-z))
    e = jnp.dot(p_ref[...].astype(BF16), wp_ref[...], preferred_element_type=F32)
    o_ref[...] = x + e * gate


def _ple(x2, p, wts, tm):
    n, d = x2.shape
    tm = min(tm, n)
    row = lambda i: (i, 0)
    consts = [wts[k] for k in ("w_gate", "w_ple")]
    return pl.pallas_call(
        _ple_kernel, grid=(n // tm,),
        in_specs=[pl.BlockSpec((tm, d), row), pl.BlockSpec((tm, p.shape[1]), row)]
                 + [_resident(a.shape) for a in consts],
        out_specs=pl.BlockSpec((tm, d), row),
        out_shape=jax.ShapeDtypeStruct((n, d), F32),
        compiler_params=_params("parallel"), name="ple",
    )(x2, p, *consts)


def _swap_halves(a):
    half = a.shape[-1] // 2
    return jnp.concatenate([a[..., half:], a[..., :half]], axis=-1)


def _prepare_weights(g_attn, w_in, g_qa, w_qb, g_kva, w_kvb, g_qn_mla, g_kn_mla, g_q_swa, g_k_swa,
                     g_out_mla, g_out_swa, w_out, g_mlp, w_up, w_down, g_ple, w_ple, w_gate):
    row = lambda g: g.reshape(1, -1)
    o_kpe = MLA_Q_LORA + MLA_KV_LORA
    o_qs = o_kpe + MLA_ROPE
    o_ks = o_qs + SWA_HEADS * SWA_HEAD_DIM
    o_vs = o_ks + SWA_KV_HEADS * SWA_HEAD_DIM
    kpe = w_in[:, o_kpe:o_qs]
    w_tm = jnp.concatenate([w_in[:, :o_kpe], kpe, _swap_halves(kpe), w_in[:, o_ks:o_vs]], axis=1)
    w_fm = jnp.concatenate([w_in[:, o_qs:o_ks], w_in[:, o_vs:]], axis=1).T
    w_kv = w_kvb.reshape(MLA_KV_LORA, MLA_HEADS, MLA_NOPE + MLA_V)
    w_kn = w_kv[..., :MLA_NOPE].reshape(MLA_KV_LORA, MLA_HEADS * MLA_NOPE)
    w_vt = w_kv[..., MLA_NOPE:].reshape(MLA_KV_LORA, MLA_HEADS * MLA_V).T
    g_rope = g_kn_mla[MLA_NOPE:]
    n_a = MLA_HEADS * MLA_V
    logit_span = 2.0 * MLA_QK ** 0.5 * LOG2E * jnp.max(jnp.abs(g_qn_mla)) * jnp.max(jnp.abs(g_kn_mla))
    return {
        "mla_logit_span": logit_span,
        "w_tm": (g_attn[:, None] * w_tm).astype(BF16), "w_fm": (w_fm * g_attn[None, :]).astype(BF16),
        "g_qa": row(g_qa), "w_qbt": w_qb.T.astype(BF16),
        "g_kva": row(g_kva), "w_kn": w_kn.astype(BF16), "w_vt": w_vt.astype(BF16),
        "g_qn_mla": g_qn_mla, "gk_a": row(g_kn_mla[:MLA_NOPE]),
        "gk_b": row(jnp.concatenate([g_rope, _swap_halves(g_rope)])),
        "g_q_swa": g_q_swa, "g_k_swa": row(g_k_swa),
        "g_out_mla": row(g_out_mla), "g_out_swa": row(g_out_swa),
        "w_out_a": w_out[:n_a].astype(BF16), "w_out_b": w_out[n_a:].astype(BF16),
        "w_up": (g_mlp[:, None] * w_up).astype(BF16), "w_down": w_down.astype(BF16),
        "w_gate": (g_ple[:, None] * w_gate).astype(BF16), "w_ple": w_ple.astype(BF16),
    }


def _layer(x, p, sink, wts, tabs):
    batch, seq, d = x.shape
    n = batch * seq
    xf = x.reshape(n, d)
    qt, k, vt, qst, ks, vst = _proj(xf, seq, wts, tabs, tm=512)
    o_a = lax.cond(
        wts["mla_logit_span"] <= SAFE_LOG2_RANGE,
        functools.partial(_mla_attention, batch=batch, seq=seq, tq=1024, tk=4096, ck=256, track_max=False,
                          step_logits=4 * 4096 * 1024),
        functools.partial(_mla_attention, batch=batch, seq=seq, tq=512, tk=512, ck=512, track_max=True,
                          step_logits=512 * 512),
        qt, k, vt)
    o_b = _swa_attention(sink, qst, ks, vst, batch, seq, tq=1024, sub=256)
    x1 = _out_proj(xf, o_a, o_b, wts, tm=512)
    x2 = _mlp(x1, wts["w_up"], wts["w_down"], tm=1024, tf=1024)
    y = _ple(x2, p.reshape(n, -1), wts, tm=512)
    return y.reshape(batch, seq, d)


def kernel(x_prompt, x_sample, p_prompt, p_sample, g_attn, w_in, g_qa, w_qb, g_kva, w_kvb, g_qn_mla, g_kn_mla,
           g_q_swa, g_k_swa, sink, g_out_mla, g_out_swa, w_out, g_mlp, w_up, w_down, g_ple, w_ple, w_gate):
    y_prompt, y_sample = x_prompt, x_sample
    tabs = _rope_inputs(max(x_prompt.shape[1], x_sample.shape[1]))
    for l in range(g_attn.shape[0]):
        wts = _prepare_weights(g_attn[l], w_in[l], g_qa[l], w_qb[l], g_kva[l], w_kvb[l], g_qn_mla[l],
                               g_kn_mla[l], g_q_swa[l], g_k_swa[l], g_out_mla[l], g_out_swa[l], w_out[l],
                               g_mlp[l], w_up[l], w_down[l], g_ple[l], w_ple[l], w_gate[l])
        y_prompt = _layer(y_prompt, p_prompt[l], sink[l], wts, tabs)
        y_sample = _layer(y_sample, p_sample[l], sink[l], wts, tabs)
    return (y_prompt, y_sample)
```

```python
import functools
import math

import jax
import jax.numpy as jnp
from jax import lax
from jax.experimental import pallas as pl
from jax.experimental.pallas import tpu as pltpu

EPS = 1e-6
ROPE_THETA = 10000.0
WINDOW = 128
MLA_HEADS = 8
MLA_Q_LORA = 512
MLA_KV_LORA = 256
MLA_NOPE = 128
MLA_ROPE = 64
MLA_V = 128
MLA_QK = MLA_NOPE + MLA_ROPE
SWA_HEADS = 8
SWA_KV_HEADS = 2
SWA_GROUP = SWA_HEADS // SWA_KV_HEADS
SWA_HEAD_DIM = 128
LANES = 128
SUBLANES = 8
MLA_PAD = 2 * LANES
SAFE_LOG2_RANGE = 100.0
LOG2E = math.log2(math.e)
VMEM_LIMIT = 56 * 1024 * 1024
TILES = {
    "proj": 512,
    "mla": dict(tq=1024, tk=4096, ck=256, step_logits=4 * 4096 * 1024),
    "mla_online": dict(tq=512, tk=512, ck=512, step_logits=512 * 512),
    "swa": dict(tq=1024, sub=256),
    "out_proj": 512,
    "mlp": dict(tm=1024, tf=1024),
    "ple": 512,
}

BF16 = jnp.bfloat16
F32 = jnp.float32
STREAM = jnp.bfloat16
NT = (((1,), (1,)), ((), ()))


def _params(*sem):
    return pltpu.CompilerParams(dimension_semantics=sem, vmem_limit_bytes=VMEM_LIMIT)


def _resident(shape):
    nd = len(shape)
    return pl.BlockSpec(shape, lambda *_: (0,) * nd, pipeline_mode=pl.Buffered(1))


def _rms(x, g):
    return x * lax.rsqrt(jnp.mean(x * x, axis=-1, keepdims=True) + EPS) * g


def _rope_lanes(x, t_cos, t_sin):
    return x * t_cos + pltpu.roll(x, LANES // 2, axis=1) * t_sin


def _rope_rows(x1, x2, cos, sin):
    return x1 * cos - x2 * sin, x2 * cos + x1 * sin


def _proj_kernel(x_ref, g_attn_ref, w_tm_ref, w_fm_ref, g_qa_ref, w_qbt_ref, g_kva_ref, w_kn_ref, w_vt_ref,
                 gq_ref, gk_a_ref, gk_b_ref, gqs_ref, gks_ref,
                 mc_ref, ms_ref, sc_ref, ss_ref, tkc_ref, tks_ref, tsc_ref, tss_ref,
                 qt_ref, k_ref, vt_ref, qst_ref, ks_ref, vst_ref):
    tm = x_ref.shape[0]
    h = _rms(x_ref[...], g_attn_ref[...]).astype(BF16)
    p_tm = jnp.dot(h, w_tm_ref[...], preferred_element_type=F32)
    o_ckv = MLA_Q_LORA
    o_kpe = o_ckv + MLA_KV_LORA
    o_ks = o_kpe + LANES

    k_bound = MLA_QK ** 0.5 * jnp.maximum(jnp.max(jnp.abs(gk_a_ref[...]), axis=-1, keepdims=True),
                                          jnp.max(jnp.abs(gk_b_ref[...]), axis=-1, keepdims=True))
    shift_lane = (lax.broadcasted_iota(jnp.int32, (1, LANES), 1) == MLA_QK - LANES).astype(F32)
    first_row = lax.broadcasted_iota(jnp.int32, (MLA_PAD - MLA_QK, tm), 0) == 0

    cqn = _rms(p_tm[:, :o_ckv], g_qa_ref[...]).astype(BF16)
    q_t = lax.dot_general(w_qbt_ref[...], cqn, NT, preferred_element_type=F32)
    q_scale = MLA_QK ** -0.5 * LOG2E
    half = MLA_ROPE // 2
    for hd in range(MLA_HEADS):
        blk = q_t[hd * MLA_QK:(hd + 1) * MLA_QK]
        r = lax.rsqrt(jnp.sum(blk * blk, axis=0, keepdims=True) * (1.0 / MLA_QK) + EPS) * q_scale
        y = blk * gq_ref[...] * r
        shift = -(jnp.sqrt(jnp.sum(y * y, axis=0, keepdims=True)) * k_bound)
        o1, o2 = _rope_rows(y[MLA_NOPE:MLA_NOPE + half], y[MLA_NOPE + half:], mc_ref[...], ms_ref[...])
        qt_ref[hd, :MLA_NOPE, :] = y[:MLA_NOPE].astype(BF16)
        qt_ref[hd, MLA_NOPE:MLA_NOPE + half, :] = o1.astype(BF16)
        qt_ref[hd, MLA_NOPE + half:MLA_QK, :] = o2.astype(BF16)
        qt_ref[hd, MLA_QK:, :] = jnp.where(first_row, shift, 0.0).astype(BF16)

    ckvn = _rms(p_tm[:, o_ckv:o_kpe], g_kva_ref[...]).astype(BF16)
    kn_all = jnp.dot(ckvn, w_kn_ref[...], preferred_element_type=F32)
    kb = p_tm[:, o_kpe:o_ks]
    s_pe = 0.5 * jnp.sum(kb * kb, axis=-1, keepdims=True)
    kr = _rope_lanes(kb * gk_b_ref[...], tkc_ref[...], tks_ref[...])
    for hd in range(MLA_HEADS):
        kn = kn_all[:, hd * MLA_NOPE:(hd + 1) * MLA_NOPE]
        r = lax.rsqrt((jnp.sum(kn * kn, axis=-1, keepdims=True) + s_pe) * (1.0 / MLA_QK) + EPS)
        k_ref[hd, :, :LANES] = (kn * gk_a_ref[...] * r).astype(BF16)
        k_ref[hd, :, LANES:] = (kr * r + shift_lane).astype(BF16)
    v_t = lax.dot_general(w_vt_ref[...], ckvn, NT, preferred_element_type=F32)
    for hd in range(MLA_HEADS):
        vt_ref[hd] = v_t[hd * MLA_V:(hd + 1) * MLA_V].astype(BF16)

    p_fm = lax.dot_general(w_fm_ref[...], h, NT, preferred_element_type=F32)
    s_scale = SWA_HEAD_DIM ** -0.5 * LOG2E
    sh = SWA_HEAD_DIM // 2
    for hd in range(SWA_HEADS):
        blk = p_fm[hd * SWA_HEAD_DIM:(hd + 1) * SWA_HEAD_DIM]
        r = lax.rsqrt(jnp.mean(blk * blk, axis=0, keepdims=True) + EPS) * s_scale
        y = blk * gqs_ref[...] * r
        o1, o2 = _rope_rows(y[:sh], y[sh:], sc_ref[...], ss_ref[...])
        qst_ref[hd * SWA_HEAD_DIM:hd * SWA_HEAD_DIM + sh, :] = o1.astype(BF16)
        qst_ref[hd * SWA_HEAD_DIM + sh:(hd + 1) * SWA_HEAD_DIM, :] = o2.astype(BF16)
    vst_ref[...] = p_fm[SWA_HEADS * SWA_HEAD_DIM:].astype(BF16)
    for hd in range(SWA_KV_HEADS):
        xs = p_tm[:, o_ks + hd * SWA_HEAD_DIM: o_ks + (hd + 1) * SWA_HEAD_DIM]
        kn = _rope_lanes(_rms(xs, gks_ref[...]), tsc_ref[...], tss_ref[...])
        ks_ref[:, hd * SWA_HEAD_DIM:(hd + 1) * SWA_HEAD_DIM] = kn.astype(BF16)


def _rope_inputs(seq):
    fm_tabs, tm_tabs = [], []
    for dim in (MLA_ROPE, SWA_HEAD_DIM):
        half = dim // 2
        inv = 1.0 / (ROPE_THETA ** (jnp.arange(0, dim, 2, dtype=F32) / dim))
        fm_tabs += list(_cos_sin(inv[:, None], seq, axis=1))
        pad = jnp.zeros((LANES - dim,), F32)
        cos, sin = _cos_sin(jnp.concatenate([inv, inv, pad])[None, :], seq, axis=0)
        ones = jnp.ones((half,), F32)
        tm_tabs += [cos * jnp.concatenate([ones, ones, pad]), sin * jnp.concatenate([-ones, ones, pad])]
    return fm_tabs, tm_tabs


def _cos_sin(inv, seq, axis):
    blk = LANES
    assert seq % blk == 0
    shape = [1, 1]
    shape[axis] = -1
    pos_a = (jnp.arange(seq // blk, dtype=F32) * blk).reshape(shape)
    pos_b = jnp.arange(blk, dtype=F32).reshape(shape)
    ca, sa = jnp.cos(pos_a * inv), jnp.sin(pos_a * inv)
    cb, sb = jnp.cos(pos_b * inv), jnp.sin(pos_b * inv)
    ca, sa = jnp.expand_dims(ca, axis + 1), jnp.expand_dims(sa, axis + 1)
    cb, sb = jnp.expand_dims(cb, axis), jnp.expand_dims(sb, axis)
    out_shape = list(inv.shape)
    out_shape[axis] = seq
    return (ca * cb - sa * sb).reshape(out_shape), (sa * cb + ca * sb).reshape(out_shape)


def _proj(x, seq, wts, tabs, tm):
    n, d = x.shape
    tm = min(tm, seq)
    nt_seq = seq // tm
    row = lambda i: (i, 0)
    col = lambda i: (0, i)
    bcast = lambda g: jnp.broadcast_to(g.reshape(-1, 1), (g.size, tm))
    consts = [wts["g_attn"], wts["w_tm"], wts["w_fm"], wts["g_qa"], wts["w_qbt"], wts["g_kva"], wts["w_kn"],
              wts["w_vt"], bcast(wts["g_qn_mla"]), wts["gk_a"], wts["gk_b"], bcast(wts["g_q_swa"]), wts["g_k_swa"]]
    fm_tabs, tm_tabs = tabs
    in_specs = ([pl.BlockSpec((tm, d), row)] + [_resident(a.shape) for a in consts]
                + [pl.BlockSpec((t.shape[0], tm), lambda i: (0, i % nt_seq)) for t in fm_tabs]
                + [pl.BlockSpec((tm, LANES), lambda i: (i % nt_seq, 0))] * 4)
    n_qs = SWA_HEADS * SWA_HEAD_DIM
    n_ks = SWA_KV_HEADS * SWA_HEAD_DIM
    out_shape = (
        jax.ShapeDtypeStruct((MLA_HEADS, MLA_PAD, n), BF16),
        jax.ShapeDtypeStruct((MLA_HEADS, n, MLA_PAD), BF16),
        jax.ShapeDtypeStruct((MLA_HEADS, MLA_V, n), BF16),
        jax.ShapeDtypeStruct((n_qs, n), BF16),
        jax.ShapeDtypeStruct((n, n_ks), BF16),
        jax.ShapeDtypeStruct((n_ks, n), BF16),
    )
    out_specs = (
        pl.BlockSpec((MLA_HEADS, MLA_PAD, tm), lambda i: (0, 0, i)),
        pl.BlockSpec((MLA_HEADS, tm, MLA_PAD), lambda i: (0, i, 0)),
        pl.BlockSpec((MLA_HEADS, MLA_V, tm), lambda i: (0, 0, i)),
        pl.BlockSpec((n_qs, tm), col),
        pl.BlockSpec((tm, n_ks), row),
        pl.BlockSpec((n_ks, tm), col),
    )
    return pl.pallas_call(
        _proj_kernel, grid=(n // tm,), in_specs=in_specs, out_specs=out_specs,
        out_shape=out_shape, compiler_params=_params("parallel"), name="proj",
    )(x, *consts, *fm_tabs, *tm_tabs)


def _mla_kernel(qt_ref, k_ref, vt_ref, o_ref, *scratch, ck, track_max):
    j = pl.program_id(3)
    hb, tk, tq = k_ref.shape[0], k_ref.shape[1], qt_ref.shape[2]
    if track_max:
        m_ref, l_ref, acc_ref = scratch
    else:
        l_ref, acc_ref, p_ref = scratch

    @pl.when(j == 0)
    def _():
        if track_max:
            m_ref[...] = jnp.full_like(m_ref, -jnp.inf)
        l_ref[...] = jnp.zeros_like(l_ref)
        acc_ref[...] = jnp.zeros_like(acc_ref)

    for hh in range(hb):
        qt = qt_ref[hh]
        if track_max:
            s = jnp.dot(k_ref[hh], qt, preferred_element_type=F32)
            m_prev = m_ref[hh]
            m_new = jnp.maximum(m_prev, jnp.max(s, axis=0, keepdims=True))
            alpha = jnp.exp2(m_prev - m_new)
            p = jnp.exp2(s - m_new)
            l_ref[hh] = alpha * l_ref[hh] + jnp.sum(p, axis=0, keepdims=True)
            acc_ref[hh] = alpha * acc_ref[hh] + jnp.dot(vt_ref[hh], p.astype(BF16), preferred_element_type=F32)
            m_ref[hh] = m_new
        else:
            l_part = l_ref[hh]
            for c in range(tk // ck):
                s = jnp.dot(k_ref[hh, c * ck:(c + 1) * ck, :], qt, preferred_element_type=F32)
                p = jnp.exp2(s)
                l_part = l_part + jnp.sum(p.reshape(ck // SUBLANES, SUBLANES, tq), axis=0)
                p_ref[hh, c * ck:(c + 1) * ck, :] = p.astype(BF16)
            l_ref[hh] = l_part
            acc_ref[hh] += jnp.dot(vt_ref[hh], p_ref[hh], preferred_element_type=F32)

    @pl.when(j == pl.num_programs(3) - 1)
    def _():
        for hh in range(hb):
            l = jnp.sum(l_ref[hh], axis=0, keepdims=True)
            o_ref[:, hh * MLA_V:(hh + 1) * MLA_V] = (acc_ref[hh] / l).T.astype(o_ref.dtype)


def _mla_attention(qt, k, vt, batch, seq, tq, tk, ck, track_max, step_logits):
    n = batch * seq
    tq, tk = min(tq, seq), min(tk, seq)
    nq, nk = seq // tq, seq // tk
    hb = min(MLA_HEADS, max(1, step_logits // (tq * tk)))
    if track_max:
        scratch = [pltpu.VMEM((hb, 1, tq), F32), pltpu.VMEM((hb, 1, tq), F32), pltpu.VMEM((hb, MLA_V, tq), F32)]
    else:
        scratch = [pltpu.VMEM((hb, SUBLANES, tq), F32), pltpu.VMEM((hb, MLA_V, tq), F32),
                   pltpu.VMEM((hb, tk, tq), BF16)]
    return pl.pallas_call(
        functools.partial(_mla_kernel, ck=min(ck, tk), track_max=track_max),
        grid=(batch, MLA_HEADS // hb, nq, nk),
        in_specs=[
            pl.BlockSpec((hb, MLA_PAD, tq), lambda b, h, i, j: (h, 0, b * nq + i)),
            pl.BlockSpec((hb, tk, MLA_PAD), lambda b, h, i, j: (h, b * nk + j, 0)),
            pl.BlockSpec((hb, MLA_V, tk), lambda b, h, i, j: (h, 0, b * nk + j)),
        ],
        out_specs=pl.BlockSpec((tq, hb * MLA_V), lambda b, h, i, j: (b * nq + i, h)),
        out_shape=jax.ShapeDtypeStruct((n, MLA_HEADS * MLA_V), BF16),
        scratch_shapes=scratch,
        compiler_params=_params("parallel", "parallel", "parallel", "arbitrary"),
        name="mla_attn_online" if track_max else "mla_attn",
    )(qt, k, vt)


def _swa_kernel(sink_ref, qt_ref, kp_ref, km_ref, kn_ref, vp_ref, vm_ref, vn_ref, o_ref, kwin_ref, vwin_ref,
                *, nq, sub):
    i = pl.program_id(1)
    tq = qt_ref.shape[1]
    kwin_ref[:WINDOW] = kp_ref[...]
    kwin_ref[WINDOW:WINDOW + tq] = km_ref[...]
    kwin_ref[WINDOW + tq:] = kn_ref[...]
    vwin_ref[:, :WINDOW] = vp_ref[...]
    vwin_ref[:, WINDOW:WINDOW + tq] = vm_ref[...]
    vwin_ref[:, WINDOW + tq:] = vn_ref[...]
    ws = sub + 2 * WINDOW
    key = lax.broadcasted_iota(jnp.int32, (ws, SWA_GROUP * sub), 0)
    qry = lax.broadcasted_iota(jnp.int32, (ws, SWA_GROUP * sub), 1) & (sub - 1)
    band = (key >= qry) & (key <= qry + 2 * WINDOW)
    no_prev = jnp.where(i > 0, 0.0, -jnp.inf)
    no_next = jnp.where(i < nq - 1, 0.0, -jnp.inf)
    for g in range(SWA_KV_HEADS):
        heads = range(g * SWA_GROUP, (g + 1) * SWA_GROUP)
        kv = slice(g * SWA_HEAD_DIM, (g + 1) * SWA_HEAD_DIM)
        sink = jnp.concatenate([jnp.full((1, sub), sink_ref[hd] * LOG2E, F32) for hd in heads], axis=1)
        for t in range(tq // sub):
            q_all = jnp.concatenate(
                [qt_ref[hd * SWA_HEAD_DIM:(hd + 1) * SWA_HEAD_DIM, t * sub:(t + 1) * sub] for hd in heads], axis=1)
            s = jnp.dot(kwin_ref[t * sub:t * sub + ws, kv], q_all, preferred_element_type=F32)
            s = jnp.where(band, s, -jnp.inf)
            if t == 0:
                s = jnp.concatenate([s[:WINDOW] + no_prev, s[WINDOW:]], axis=0)
            if t == tq // sub - 1:
                s = jnp.concatenate([s[:ws - WINDOW], s[ws - WINDOW:] + no_next], axis=0)
            m = jnp.maximum(jnp.max(s, axis=0, keepdims=True), sink)
            p = jnp.exp2(s - m)
            denom = jnp.sum(p, axis=0, keepdims=True) + jnp.exp2(sink - m)
            o_t = jnp.dot(vwin_ref[kv, t * sub:t * sub + ws], p.astype(BF16), preferred_element_type=F32) / denom
            for n_hd, hd in enumerate(heads):
                o_ref[t * sub:(t + 1) * sub, hd * SWA_HEAD_DIM:(hd + 1) * SWA_HEAD_DIM] = (
                    o_t[:, n_hd * sub:(n_hd + 1) * sub].T.astype(o_ref.dtype))


def _swa_attention(sink, qt, k, vt, batch, seq, tq, sub):
    n = batch * seq
    tq = min(tq, seq)
    sub = min(sub, tq)
    assert sub & (sub - 1) == 0 and tq % sub == 0, "query index within a sub-tile is taken with a bit mask"
    nq = seq // tq
    r = tq // WINDOW
    nb = seq // WINDOW
    qw = SWA_HEADS * SWA_HEAD_DIM
    kw = SWA_KV_HEADS * SWA_HEAD_DIM
    prev = lambda b, i: b * nb + jnp.maximum(i * r - 1, 0)
    nxt = lambda b, i: b * nb + jnp.minimum((i + 1) * r, nb - 1)
    return pl.pallas_call(
        functools.partial(_swa_kernel, nq=nq, sub=sub),
        grid=(batch, nq),
        in_specs=[
            pl.BlockSpec(memory_space=pltpu.SMEM),
            pl.BlockSpec((qw, tq), lambda b, i: (0, b * nq + i)),
            pl.BlockSpec((WINDOW, kw), lambda b, i: (prev(b, i), 0)),
            pl.BlockSpec((tq, kw), lambda b, i: (b * nq + i, 0)),
            pl.BlockSpec((WINDOW, kw), lambda b, i: (nxt(b, i), 0)),
            pl.BlockSpec((kw, WINDOW), lambda b, i: (0, prev(b, i))),
            pl.BlockSpec((kw, tq), lambda b, i: (0, b * nq + i)),
            pl.BlockSpec((kw, WINDOW), lambda b, i: (0, nxt(b, i))),
        ],
        out_specs=pl.BlockSpec((tq, qw), lambda b, i: (b * nq + i, 0)),
        out_shape=jax.ShapeDtypeStruct((n, qw), BF16),
        scratch_shapes=[pltpu.VMEM((tq + 2 * WINDOW, kw), BF16), pltpu.VMEM((kw, tq + 2 * WINDOW), BF16)],
        compiler_params=_params("parallel", "parallel"),
        name="swa_attn",
    )(sink, qt, k, k, k, vt, vt, vt)


def _out_proj_kernel(x_ref, oa_ref, ob_ref, ga_ref, gb_ref, wa_ref, wb_ref, x1_ref):
    oa = _rms(oa_ref[...].astype(F32), ga_ref[...]).astype(BF16)
    ob = _rms(ob_ref[...].astype(F32), gb_ref[...]).astype(BF16)
    x1 = (x_ref[...] + jnp.dot(oa, wa_ref[...], preferred_element_type=F32)
          + jnp.dot(ob, wb_ref[...], preferred_element_type=F32))
    x1_ref[...] = x1.astype(x1_ref.dtype)


def _out_proj(x, oa, ob, wts, tm):
    n, d = x.shape
    tm = min(tm, n)
    row = lambda i: (i, 0)
    consts = [wts[k] for k in ("g_out_mla", "g_out_swa", "w_out_a", "w_out_b")]
    return pl.pallas_call(
        _out_proj_kernel, grid=(n // tm,),
        in_specs=[pl.BlockSpec((tm, d), row), pl.BlockSpec((tm, oa.shape[1]), row),
                  pl.BlockSpec((tm, ob.shape[1]), row)] + [_resident(a.shape) for a in consts],
        out_specs=pl.BlockSpec((tm, d), row),
        out_shape=jax.ShapeDtypeStruct((n, d), STREAM),
        compiler_params=_params("parallel"), name="out_proj",
    )(x, oa, ob, *consts)


def _mlp_kernel(x1_ref, wu_ref, wd_ref, o_ref, r_ref):
    @pl.when(pl.program_id(1) == 0)
    def _():
        x1 = x1_ref[...].astype(F32)
        r_ref[...] = lax.rsqrt(jnp.mean(x1 * x1, axis=-1, keepdims=True) + EPS)
        o_ref[...] = x1

    up = jnp.dot(x1_ref[...], wu_ref[...], preferred_element_type=F32) * r_ref[...]
    act = jnp.square(jnp.maximum(up, 0.0)).astype(BF16)
    o_ref[...] += jnp.dot(act, wd_ref[...], preferred_element_type=F32)


def _mlp(x1, w_up, w_down, tm, tf):
    n, d = x1.shape
    f = w_up.shape[1]
    tm, tf = min(tm, n), min(tf, f)
    row = lambda i, j: (i, 0)
    return pl.pallas_call(
        _mlp_kernel, grid=(n // tm, f // tf),
        in_specs=[pl.BlockSpec((tm, d), row),
                  pl.BlockSpec((d, tf), lambda i, j: (0, j)), pl.BlockSpec((tf, d), lambda i, j: (j, 0))],
        out_specs=pl.BlockSpec((tm, d), row),
        out_shape=jax.ShapeDtypeStruct((n, d), F32),
        scratch_shapes=[pltpu.VMEM((tm, 1), F32)],
        compiler_params=_params("parallel", "arbitrary"), name="mlp",
    )(x1, w_up, w_down)


def _ple_kernel(x_ref, p_ref, g_ref, wg_ref, wp_ref, o_ref):
    x = x_ref[...]
    z = jnp.dot(_rms(x, g_ref[...]).astype(BF16), wg_ref[...], preferred_element_type=F32)
    gate = 1.0 / (1.0 + jnp.exp(-z))
    e = jnp.dot(p_ref[...].astype(BF16), wp_ref[...], preferred_element_type=F32)
    o_ref[...] = x + e * gate


def _ple(x2, p, wts, tm):
    n, d = x2.shape
    tm = min(tm, n)
    row = lambda i: (i, 0)
    consts = [wts[k] for k in ("g_ple", "w_gate", "w_ple")]
    return pl.pallas_call(
        _ple_kernel, grid=(n // tm,),
        in_specs=[pl.BlockSpec((tm, d), row), pl.BlockSpec((tm, p.shape[1]), row)]
                 + [_resident(a.shape) for a in consts],
        out_specs=pl.BlockSpec((tm, d), row),
        out_shape=jax.ShapeDtypeStruct((n, d), F32),
        compiler_params=_params("parallel"), name="ple",
    )(x2, p, *consts)


def _swap_halves(a):
    half = a.shape[-1] // 2
    return jnp.concatenate([a[..., half:], a[..., :half]], axis=-1)


def _prepare_weights(g_attn, w_in, g_qa, w_qb, g_kva, w_kvb, g_qn_mla, g_kn_mla, g_q_swa, g_k_swa,
                     g_out_mla, g_out_swa, w_out, g_mlp, w_up, w_down, g_ple, w_ple, w_gate):
    row = lambda g: g.reshape(1, -1)
    o_kpe = MLA_Q_LORA + MLA_KV_LORA
    o_qs = o_kpe + MLA_ROPE
    o_ks = o_qs + SWA_HEADS * SWA_HEAD_DIM
    o_vs = o_ks + SWA_KV_HEADS * SWA_HEAD_DIM
    kpe = w_in[:, o_kpe:o_qs]
    w_tm = jnp.concatenate([w_in[:, :o_kpe], kpe, _swap_halves(kpe), w_in[:, o_ks:o_vs]], axis=1)
    w_fm = jnp.concatenate([w_in[:, o_qs:o_ks], w_in[:, o_vs:]], axis=1).T
    w_kv = w_kvb.reshape(MLA_KV_LORA, MLA_HEADS, MLA_NOPE + MLA_V)
    w_kn = w_kv[..., :MLA_NOPE].reshape(MLA_KV_LORA, MLA_HEADS * MLA_NOPE)
    w_vt = w_kv[..., MLA_NOPE:].reshape(MLA_KV_LORA, MLA_HEADS * MLA_V).T
    g_rope = g_kn_mla[MLA_NOPE:]
    n_a = MLA_HEADS * MLA_V
    logit_span = 2.0 * MLA_QK ** 0.5 * LOG2E * jnp.max(jnp.abs(g_qn_mla)) * jnp.max(jnp.abs(g_kn_mla))
    return {
        "mla_logit_span": logit_span,
        "g_attn": row(g_attn), "w_tm": w_tm.astype(BF16), "w_fm": w_fm.astype(BF16),
        "g_qa": row(g_qa), "w_qbt": w_qb.T.astype(BF16),
        "g_kva": row(g_kva), "w_kn": w_kn.astype(BF16), "w_vt": w_vt.astype(BF16),
        "g_qn_mla": g_qn_mla, "gk_a": row(g_kn_mla[:MLA_NOPE]),
        "gk_b": row(jnp.concatenate([g_rope, _swap_halves(g_rope)])),
        "g_q_swa": g_q_swa, "g_k_swa": row(g_k_swa),
        "g_out_mla": row(g_out_mla), "g_out_swa": row(g_out_swa),
        "w_out_a": w_out[:n_a].astype(BF16), "w_out_b": w_out[n_a:].astype(BF16),
        "w_up": (g_mlp[:, None] * w_up).astype(BF16), "w_down": w_down.astype(BF16),
        "g_ple": row(g_ple), "w_gate": w_gate.astype(BF16), "w_ple": w_ple.astype(BF16),
    }


def _layer(x, p, sink, wts, tabs):
    batch, seq, d = x.shape
    n = batch * seq
    xf = x.reshape(n, d)
    qt, k, vt, qst, ks, vst = _proj(xf, seq, wts, tabs, tm=TILES["proj"])
    o_a = lax.cond(
        wts["mla_logit_span"] <= SAFE_LOG2_RANGE,
        functools.partial(_mla_attention, batch=batch, seq=seq, track_max=False, **TILES["mla"]),
        functools.partial(_mla_attention, batch=batch, seq=seq, track_max=True, **TILES["mla_online"]),
        qt, k, vt)
    o_b = _swa_attention(sink, qst, ks, vst, batch, seq, **TILES["swa"])
    x1 = _out_proj(xf, o_a, o_b, wts, tm=TILES["out_proj"])
    x2 = _mlp(x1, wts["w_up"], wts["w_down"], **TILES["mlp"])
    y = _ple(x2, p.reshape(n, -1), wts, tm=TILES["ple"])
    return y.reshape(batch, seq, d)


def kernel(x_prompt, x_sample, p_prompt, p_sample, g_attn, w_in, g_qa, w_qb, g_kva, w_kvb, g_qn_mla, g_kn_mla,
           g_q_swa, g_k_swa, sink, g_out_mla, g_out_swa, w_out, g_mlp, w_up, w_down, g_ple, w_ple, w_gate):
    y_prompt, y_sample = x_prompt, x_sample
    tabs = _rope_inputs(max(x_prompt.shape[1], x_sample.shape[1]))
    for l in range(g_attn.shape[0]):
        wts = _prepare_weights(g_attn[l], w_in[l], g_qa[l], w_qb[l], g_kva[l], w_kvb[l], g_qn_mla[l],
                               g_kn_mla[l], g_q_swa[l], g_k_swa[l], g_out_mla[l], g_out_swa[l], w_out[l],
                               g_mlp[l], w_up[l], w_down[l], g_ple[l], w_ple[l], w_gate[l])
        y_prompt = _layer(y_prompt, p_prompt[l], sink[l], wts, tabs)
        y_sample = _layer(y_sample, p_sample[l], sink[l], wts, tabs)
    return (y_prompt, y_sample)
```

```python
import functools
import math

import jax
import jax.numpy as jnp
from jax import lax
from jax.experimental import pallas as pl
from jax.experimental.pallas import tpu as pltpu

EPS = 1e-6
ROPE_THETA = 10000.0
WINDOW = 128
MLA_HEADS = 8
MLA_Q_LORA = 512
MLA_KV_LORA = 256
MLA_NOPE = 128
MLA_ROPE = 64
MLA_V = 128
MLA_QK = MLA_NOPE + MLA_ROPE
SWA_HEADS = 8
SWA_KV_HEADS = 2
SWA_GROUP = SWA_HEADS // SWA_KV_HEADS
SWA_HEAD_DIM = 128
LANES = 128
SUBLANES = 8
MLA_PAD = 2 * LANES
SAFE_LOG2_RANGE = 100.0
LOG2E = math.log2(math.e)
VMEM_LIMIT = 56 * 1024 * 1024
TILES = {
    "proj": 512,
    "mla": dict(tq=1024, tk=4096, ck=256, step_logits=4 * 4096 * 1024),
    "mla_online": dict(tq=512, tk=512, ck=512, step_logits=512 * 512),
    "swa": dict(tq=1024, sub=256),
    "out_proj": 512,
    "mlp": dict(tm=1024, tf=1024),
    "attn_mlp": dict(tm=512, tf=1024),
    "ple": 512,
}

BF16 = jnp.bfloat16
F32 = jnp.float32
STREAM = jnp.bfloat16
NT = (((1,), (1,)), ((), ()))


def _params(*sem):
    return pltpu.CompilerParams(dimension_semantics=sem, vmem_limit_bytes=VMEM_LIMIT)


def _resident(shape):
    nd = len(shape)
    return pl.BlockSpec(shape, lambda *_: (0,) * nd, pipeline_mode=pl.Buffered(1))


def _rms(x, g):
    return x * lax.rsqrt(jnp.mean(x * x, axis=-1, keepdims=True) + EPS) * g


def _rope_lanes(x, t_cos, t_sin):
    return x * t_cos + pltpu.roll(x, LANES // 2, axis=1) * t_sin


def _rope_rows(x1, x2, cos, sin):
    return x1 * cos - x2 * sin, x2 * cos + x1 * sin


def _proj_kernel(x_ref, g_attn_ref, w_tm_ref, w_fm_ref, g_qa_ref, w_qbt_ref, g_kva_ref, w_kn_ref, w_vt_ref,
                 gq_ref, gk_a_ref, gk_b_ref, gqs_ref, gks_ref,
                 mc_ref, ms_ref, sc_ref, ss_ref, tkc_ref, tks_ref, tsc_ref, tss_ref,
                 qt_ref, k_ref, vt_ref, qst_ref, ks_ref, vst_ref):
    tm = x_ref.shape[0]
    h = _rms(x_ref[...], g_attn_ref[...]).astype(BF16)
    p_tm = jnp.dot(h, w_tm_ref[...], preferred_element_type=F32)
    o_ckv = MLA_Q_LORA
    o_kpe = o_ckv + MLA_KV_LORA
    o_ks = o_kpe + LANES

    k_bound = MLA_QK ** 0.5 * jnp.maximum(jnp.max(jnp.abs(gk_a_ref[...]), axis=-1, keepdims=True),
                                          jnp.max(jnp.abs(gk_b_ref[...]), axis=-1, keepdims=True))
    shift_lane = (lax.broadcasted_iota(jnp.int32, (1, LANES), 1) == MLA_QK - LANES).astype(F32)
    first_row = lax.broadcasted_iota(jnp.int32, (MLA_PAD - MLA_QK, tm), 0) == 0

    cqn = _rms(p_tm[:, :o_ckv], g_qa_ref[...]).astype(BF16)
    q_t = lax.dot_general(w_qbt_ref[...], cqn, NT, preferred_element_type=F32)
    q_scale = MLA_QK ** -0.5 * LOG2E
    half = MLA_ROPE // 2
    for hd in range(MLA_HEADS):
        blk = q_t[hd * MLA_QK:(hd + 1) * MLA_QK]
        r = lax.rsqrt(jnp.sum(blk * blk, axis=0, keepdims=True) * (1.0 / MLA_QK) + EPS) * q_scale
        y = blk * gq_ref[...] * r
        shift = -(jnp.sqrt(jnp.sum(y * y, axis=0, keepdims=True)) * k_bound)
        o1, o2 = _rope_rows(y[MLA_NOPE:MLA_NOPE + half], y[MLA_NOPE + half:], mc_ref[...], ms_ref[...])
        qt_ref[hd, :MLA_NOPE, :] = y[:MLA_NOPE].astype(BF16)
        qt_ref[hd, MLA_NOPE:MLA_NOPE + half, :] = o1.astype(BF16)
        qt_ref[hd, MLA_NOPE + half:MLA_QK, :] = o2.astype(BF16)
        qt_ref[hd, MLA_QK:, :] = jnp.where(first_row, shift, 0.0).astype(BF16)

    ckvn = _rms(p_tm[:, o_ckv:o_kpe], g_kva_ref[...]).astype(BF16)
    kn_all = jnp.dot(ckvn, w_kn_ref[...], preferred_element_type=F32)
    kb = p_tm[:, o_kpe:o_ks]
    s_pe = 0.5 * jnp.sum(kb * kb, axis=-1, keepdims=True)
    kr = _rope_lanes(kb * gk_b_ref[...], tkc_ref[...], tks_ref[...])
    for hd in range(MLA_HEADS):
        kn = kn_all[:, hd * MLA_NOPE:(hd + 1) * MLA_NOPE]
        r = lax.rsqrt((jnp.sum(kn * kn, axis=-1, keepdims=True) + s_pe) * (1.0 / MLA_QK) + EPS)
        k_ref[hd, :, :LANES] = (kn * gk_a_ref[...] * r).astype(BF16)
        k_ref[hd, :, LANES:] = (kr * r + shift_lane).astype(BF16)
    v_t = lax.dot_general(w_vt_ref[...], ckvn, NT, preferred_element_type=F32)
    for hd in range(MLA_HEADS):
        vt_ref[hd] = v_t[hd * MLA_V:(hd + 1) * MLA_V].astype(BF16)

    p_fm = lax.dot_general(w_fm_ref[...], h, NT, preferred_element_type=F32)
    s_scale = SWA_HEAD_DIM ** -0.5 * LOG2E
    sh = SWA_HEAD_DIM // 2
    for hd in range(SWA_HEADS):
        blk = p_fm[hd * SWA_HEAD_DIM:(hd + 1) * SWA_HEAD_DIM]
        r = lax.rsqrt(jnp.mean(blk * blk, axis=0, keepdims=True) + EPS) * s_scale
        y = blk * gqs_ref[...] * r
        o1, o2 = _rope_rows(y[:sh], y[sh:], sc_ref[...], ss_ref[...])
        qst_ref[hd * SWA_HEAD_DIM:hd * SWA_HEAD_DIM + sh, :] = o1.astype(BF16)
        qst_ref[hd * SWA_HEAD_DIM + sh:(hd + 1) * SWA_HEAD_DIM, :] = o2.astype(BF16)
    vst_ref[...] = p_fm[SWA_HEADS * SWA_HEAD_DIM:].astype(BF16)
    for hd in range(SWA_KV_HEADS):
        xs = p_tm[:, o_ks + hd * SWA_HEAD_DIM: o_ks + (hd + 1) * SWA_HEAD_DIM]
        kn = _rope_lanes(_rms(xs, gks_ref[...]), tsc_ref[...], tss_ref[...])
        ks_ref[:, hd * SWA_HEAD_DIM:(hd + 1) * SWA_HEAD_DIM] = kn.astype(BF16)


def _rope_inputs(seq):
    fm_tabs, tm_tabs = [], []
    for dim in (MLA_ROPE, SWA_HEAD_DIM):
        half = dim // 2
        inv = 1.0 / (ROPE_THETA ** (jnp.arange(0, dim, 2, dtype=F32) / dim))
        fm_tabs += list(_cos_sin(inv[:, None], seq, axis=1))
        pad = jnp.zeros((LANES - dim,), F32)
        cos, sin = _cos_sin(jnp.concatenate([inv, inv, pad])[None, :], seq, axis=0)
        ones = jnp.ones((half,), F32)
        tm_tabs += [cos * jnp.concatenate([ones, ones, pad]), sin * jnp.concatenate([-ones, ones, pad])]
    return fm_tabs, tm_tabs


def _cos_sin(inv, seq, axis):
    blk = LANES
    assert seq % blk == 0
    shape = [1, 1]
    shape[axis] = -1
    pos_a = (jnp.arange(seq // blk, dtype=F32) * blk).reshape(shape)
    pos_b = jnp.arange(blk, dtype=F32).reshape(shape)
    ca, sa = jnp.cos(pos_a * inv), jnp.sin(pos_a * inv)
    cb, sb = jnp.cos(pos_b * inv), jnp.sin(pos_b * inv)
    ca, sa = jnp.expand_dims(ca, axis + 1), jnp.expand_dims(sa, axis + 1)
    cb, sb = jnp.expand_dims(cb, axis), jnp.expand_dims(sb, axis)
    out_shape = list(inv.shape)
    out_shape[axis] = seq
    return (ca * cb - sa * sb).reshape(out_shape), (sa * cb + ca * sb).reshape(out_shape)


def _proj(x, seq, wts, tabs, tm):
    n, d = x.shape
    tm = min(tm, seq)
    nt_seq = seq // tm
    row = lambda i: (i, 0)
    col = lambda i: (0, i)
    bcast = lambda g: jnp.broadcast_to(g.reshape(-1, 1), (g.size, tm))
    consts = [wts["g_attn"], wts["w_tm"], wts["w_fm"], wts["g_qa"], wts["w_qbt"], wts["g_kva"], wts["w_kn"],
              wts["w_vt"], bcast(wts["g_qn_mla"]), wts["gk_a"], wts["gk_b"], bcast(wts["g_q_swa"]), wts["g_k_swa"]]
    fm_tabs, tm_tabs = tabs
    in_specs = ([pl.BlockSpec((tm, d), row)] + [_resident(a.shape) for a in consts]
                + [pl.BlockSpec((t.shape[0], tm), lambda i: (0, i % nt_seq)) for t in fm_tabs]
                + [pl.BlockSpec((tm, LANES), lambda i: (i % nt_seq, 0))] * 4)
    n_qs = SWA_HEADS * SWA_HEAD_DIM
    n_ks = SWA_KV_HEADS * SWA_HEAD_DIM
    out_shape = (
        jax.ShapeDtypeStruct((MLA_HEADS, MLA_PAD, n), BF16),
        jax.ShapeDtypeStruct((MLA_HEADS, n, MLA_PAD), BF16),
        jax.ShapeDtypeStruct((MLA_HEADS, MLA_V, n), BF16),
        jax.ShapeDtypeStruct((n_qs, n), BF16),
        jax.ShapeDtypeStruct((n, n_ks), BF16),
        jax.ShapeDtypeStruct((n_ks, n), BF16),
    )
    out_specs = (
        pl.BlockSpec((MLA_HEADS, MLA_PAD, tm), lambda i: (0, 0, i)),
        pl.BlockSpec((MLA_HEADS, tm, MLA_PAD), lambda i: (0, i, 0)),
        pl.BlockSpec((MLA_HEADS, MLA_V, tm), lambda i: (0, 0, i)),
        pl.BlockSpec((n_qs, tm), col),
        pl.BlockSpec((tm, n_ks), row),
        pl.BlockSpec((n_ks, tm), col),
    )
    return pl.pallas_call(
        _proj_kernel, grid=(n // tm,), in_specs=in_specs, out_specs=out_specs,
        out_shape=out_shape, compiler_params=_params("parallel"), name="proj",
    )(x, *consts, *fm_tabs, *tm_tabs)


def _mla_kernel(qt_ref, k_ref, vt_ref, o_ref, *scratch, ck, track_max):
    j = pl.program_id(3)
    hb, tk, tq = k_ref.shape[0], k_ref.shape[1], qt_ref.shape[2]
    if track_max:
        m_ref, l_ref, acc_ref = scratch
    else:
        l_ref, acc_ref, p_ref = scratch

    @pl.when(j == 0)
    def _():
        if track_max:
            m_ref[...] = jnp.full_like(m_ref, -jnp.inf)
        l_ref[...] = jnp.zeros_like(l_ref)
        acc_ref[...] = jnp.zeros_like(acc_ref)

    for hh in range(hb):
        qt = qt_ref[hh]
        if track_max:
            s = jnp.dot(k_ref[hh], qt, preferred_element_type=F32)
            m_prev = m_ref[hh]
            m_new = jnp.maximum(m_prev, jnp.max(s, axis=0, keepdims=True))
            alpha = jnp.exp2(m_prev - m_new)
            p = jnp.exp2(s - m_new)
            l_ref[hh] = alpha * l_ref[hh] + jnp.sum(p, axis=0, keepdims=True)
            acc_ref[hh] = alpha * acc_ref[hh] + jnp.dot(vt_ref[hh], p.astype(BF16), preferred_element_type=F32)
            m_ref[hh] = m_new
        else:
            l_part = l_ref[hh]
            for c in range(tk // ck):
                s = jnp.dot(k_ref[hh, c * ck:(c + 1) * ck, :], qt, preferred_element_type=F32)
                p = jnp.exp2(s)
                l_part = l_part + jnp.sum(p.reshape(ck // SUBLANES, SUBLANES, tq), axis=0)
                p_ref[hh, c * ck:(c + 1) * ck, :] = p.astype(BF16)
            l_ref[hh] = l_part
            acc_ref[hh] += jnp.dot(vt_ref[hh], p_ref[hh], preferred_element_type=F32)

    @pl.when(j == pl.num_programs(3) - 1)
    def _():
        for hh in range(hb):
            l = jnp.sum(l_ref[hh], axis=0, keepdims=True)
            o_ref[:, hh * MLA_V:(hh + 1) * MLA_V] = (acc_ref[hh] / l).T.astype(o_ref.dtype)


def _mla_attention(qt, k, vt, batch, seq, tq, tk, ck, track_max, step_logits):
    n = batch * seq
    tq, tk = min(tq, seq), min(tk, seq)
    nq, nk = seq // tq, seq // tk
    hb = min(MLA_HEADS, max(1, step_logits // (tq * tk)))
    if track_max:
        scratch = [pltpu.VMEM((hb, 1, tq), F32), pltpu.VMEM((hb, 1, tq), F32), pltpu.VMEM((hb, MLA_V, tq), F32)]
    else:
        scratch = [pltpu.VMEM((hb, SUBLANES, tq), F32), pltpu.VMEM((hb, MLA_V, tq), F32),
                   pltpu.VMEM((hb, tk, tq), BF16)]
    return pl.pallas_call(
        functools.partial(_mla_kernel, ck=min(ck, tk), track_max=track_max),
        grid=(batch, MLA_HEADS // hb, nq, nk),
        in_specs=[
            pl.BlockSpec((hb, MLA_PAD, tq), lambda b, h, i, j: (h, 0, b * nq + i)),
            pl.BlockSpec((hb, tk, MLA_PAD), lambda b, h, i, j: (h, b * nk + j, 0)),
            pl.BlockSpec((hb, MLA_V, tk), lambda b, h, i, j: (h, 0, b * nk + j)),
        ],
        out_specs=pl.BlockSpec((tq, hb * MLA_V), lambda b, h, i, j: (b * nq + i, h)),
        out_shape=jax.ShapeDtypeStruct((n, MLA_HEADS * MLA_V), BF16),
        scratch_shapes=scratch,
        compiler_params=_params("parallel", "parallel", "parallel", "arbitrary"),
        name="mla_attn_online" if track_max else "mla_attn",
    )(qt, k, vt)


def _swa_kernel(sink_ref, qt_ref, kp_ref, km_ref, kn_ref, vp_ref, vm_ref, vn_ref, o_ref, kwin_ref, vwin_ref,
                *, nq, sub):
    i = pl.program_id(1)
    tq = qt_ref.shape[1]
    kwin_ref[:WINDOW] = kp_ref[...]
    kwin_ref[WINDOW:WINDOW + tq] = km_ref[...]
    kwin_ref[WINDOW + tq:] = kn_ref[...]
    vwin_ref[:, :WINDOW] = vp_ref[...]
    vwin_ref[:, WINDOW:WINDOW + tq] = vm_ref[...]
    vwin_ref[:, WINDOW + tq:] = vn_ref[...]
    ws = sub + 2 * WINDOW
    key = lax.broadcasted_iota(jnp.int32, (ws, SWA_GROUP * sub), 0)
    qry = lax.broadcasted_iota(jnp.int32, (ws, SWA_GROUP * sub), 1) & (sub - 1)
    band = (key >= qry) & (key <= qry + 2 * WINDOW)
    no_prev = jnp.where(i > 0, 0.0, -jnp.inf)
    no_next = jnp.where(i < nq - 1, 0.0, -jnp.inf)
    for g in range(SWA_KV_HEADS):
        heads = range(g * SWA_GROUP, (g + 1) * SWA_GROUP)
        kv = slice(g * SWA_HEAD_DIM, (g + 1) * SWA_HEAD_DIM)
        sink = jnp.concatenate([jnp.full((1, sub), sink_ref[hd] * LOG2E, F32) for hd in heads], axis=1)
        for t in range(tq // sub):
            q_all = jnp.concatenate(
                [qt_ref[hd * SWA_HEAD_DIM:(hd + 1) * SWA_HEAD_DIM, t * sub:(t + 1) * sub] for hd in heads], axis=1)
            s = jnp.dot(kwin_ref[t * sub:t * sub + ws, kv], q_all, preferred_element_type=F32)
            s = jnp.where(band, s, -jnp.inf)
            if t == 0:
                s = jnp.concatenate([s[:WINDOW] + no_prev, s[WINDOW:]], axis=0)
            if t == tq // sub - 1:
                s = jnp.concatenate([s[:ws - WINDOW], s[ws - WINDOW:] + no_next], axis=0)
            m = jnp.maximum(jnp.max(s, axis=0, keepdims=True), sink)
            p = jnp.exp2(s - m)
            denom = jnp.sum(p, axis=0, keepdims=True) + jnp.exp2(sink - m)
            o_t = jnp.dot(vwin_ref[kv, t * sub:t * sub + ws], p.astype(BF16), preferred_element_type=F32) / denom
            for n_hd, hd in enumerate(heads):
                o_ref[t * sub:(t + 1) * sub, hd * SWA_HEAD_DIM:(hd + 1) * SWA_HEAD_DIM] = (
                    o_t[:, n_hd * sub:(n_hd + 1) * sub].T.astype(o_ref.dtype))


def _swa_attention(sink, qt, k, vt, batch, seq, tq, sub):
    n = batch * seq
    tq = min(tq, seq)
    sub = min(sub, tq)
    assert sub & (sub - 1) == 0 and tq % sub == 0, "query index within a sub-tile is taken with a bit mask"
    nq = seq // tq
    r = tq // WINDOW
    nb = seq // WINDOW
    qw = SWA_HEADS * SWA_HEAD_DIM
    kw = SWA_KV_HEADS * SWA_HEAD_DIM
    prev = lambda b, i: b * nb + jnp.maximum(i * r - 1, 0)
    nxt = lambda b, i: b * nb + jnp.minimum((i + 1) * r, nb - 1)
    return pl.pallas_call(
        functools.partial(_swa_kernel, nq=nq, sub=sub),
        grid=(batch, nq),
        in_specs=[
            pl.BlockSpec(memory_space=pltpu.SMEM),
            pl.BlockSpec((qw, tq), lambda b, i: (0, b * nq + i)),
            pl.BlockSpec((WINDOW, kw), lambda b, i: (prev(b, i), 0)),
            pl.BlockSpec((tq, kw), lambda b, i: (b * nq + i, 0)),
            pl.BlockSpec((WINDOW, kw), lambda b, i: (nxt(b, i), 0)),
            pl.BlockSpec((kw, WINDOW), lambda b, i: (0, prev(b, i))),
            pl.BlockSpec((kw, tq), lambda b, i: (0, b * nq + i)),
            pl.BlockSpec((kw, WINDOW), lambda b, i: (0, nxt(b, i))),
        ],
        out_specs=pl.BlockSpec((tq, qw), lambda b, i: (b * nq + i, 0)),
        out_shape=jax.ShapeDtypeStruct((n, qw), BF16),
        scratch_shapes=[pltpu.VMEM((tq + 2 * WINDOW, kw), BF16), pltpu.VMEM((kw, tq + 2 * WINDOW), BF16)],
        compiler_params=_params("parallel", "parallel"),
        name="swa_attn",
    )(sink, qt, k, k, k, vt, vt, vt)


def _out_proj_kernel(x_ref, oa_ref, ob_ref, ga_ref, gb_ref, wa_ref, wb_ref, x1_ref):
    oa = _rms(oa_ref[...].astype(F32), ga_ref[...]).astype(BF16)
    ob = _rms(ob_ref[...].astype(F32), gb_ref[...]).astype(BF16)
    x1 = (x_ref[...] + jnp.dot(oa, wa_ref[...], preferred_element_type=F32)
          + jnp.dot(ob, wb_ref[...], preferred_element_type=F32))
    x1_ref[...] = x1.astype(x1_ref.dtype)


def _out_proj(x, oa, ob, wts, tm):
    n, d = x.shape
    tm = min(tm, n)
    row = lambda i: (i, 0)
    consts = [wts[k] for k in ("g_out_mla", "g_out_swa", "w_out_a", "w_out_b")]
    return pl.pallas_call(
        _out_proj_kernel, grid=(n // tm,),
        in_specs=[pl.BlockSpec((tm, d), row), pl.BlockSpec((tm, oa.shape[1]), row),
                  pl.BlockSpec((tm, ob.shape[1]), row)] + [_resident(a.shape) for a in consts],
        out_specs=pl.BlockSpec((tm, d), row),
        out_shape=jax.ShapeDtypeStruct((n, d), STREAM),
        compiler_params=_params("parallel"), name="out_proj",
    )(x, oa, ob, *consts)


def _mlp_kernel(x1_ref, wu_ref, wd_ref, o_ref, r_ref):
    @pl.when(pl.program_id(1) == 0)
    def _():
        x1 = x1_ref[...].astype(F32)
        r_ref[...] = lax.rsqrt(jnp.mean(x1 * x1, axis=-1, keepdims=True) + EPS)
        o_ref[...] = x1

    up = jnp.dot(x1_ref[...], wu_ref[...], preferred_element_type=F32) * r_ref[...]
    act = jnp.square(jnp.maximum(up, 0.0)).astype(BF16)
    o_ref[...] += jnp.dot(act, wd_ref[...], preferred_element_type=F32)


def _mlp(x1, w_up, w_down, tm, tf):
    n, d = x1.shape
    f = w_up.shape[1]
    tm, tf = min(tm, n), min(tf, f)
    row = lambda i, j: (i, 0)
    return pl.pallas_call(
        _mlp_kernel, grid=(n // tm, f // tf),
        in_specs=[pl.BlockSpec((tm, d), row),
                  pl.BlockSpec((d, tf), lambda i, j: (0, j)), pl.BlockSpec((tf, d), lambda i, j: (j, 0))],
        out_specs=pl.BlockSpec((tm, d), row),
        out_shape=jax.ShapeDtypeStruct((n, d), F32),
        scratch_shapes=[pltpu.VMEM((tm, 1), F32)],
        compiler_params=_params("parallel", "arbitrary"), name="mlp",
    )(x1, w_up, w_down)


def _attn_mlp_kernel(x_ref, oa_ref, ob_ref, ga_ref, gb_ref, wa_ref, wb_ref, wu_ref, wd_ref, o_ref, x1_ref, r_ref):
    @pl.when(pl.program_id(1) == 0)
    def _():
        oa = _rms(oa_ref[...].astype(F32), ga_ref[...]).astype(BF16)
        ob = _rms(ob_ref[...].astype(F32), gb_ref[...]).astype(BF16)
        x1 = (x_ref[...] + jnp.dot(oa, wa_ref[...], preferred_element_type=F32)
              + jnp.dot(ob, wb_ref[...], preferred_element_type=F32))
        r_ref[...] = lax.rsqrt(jnp.mean(x1 * x1, axis=-1, keepdims=True) + EPS)
        x1_ref[...] = x1.astype(BF16)
        o_ref[...] = x1

    up = jnp.dot(x1_ref[...], wu_ref[...], preferred_element_type=F32) * r_ref[...]
    act = jnp.square(jnp.maximum(up, 0.0)).astype(BF16)
    o_ref[...] += jnp.dot(act, wd_ref[...], preferred_element_type=F32)


def _attn_mlp(x, oa, ob, wts, tm, tf):
    n, d = x.shape
    f = wts["w_up"].shape[1]
    tm, tf = min(tm, n), min(tf, f)
    row = lambda i, j: (i, 0)
    consts = [wts[k] for k in ("g_out_mla", "g_out_swa", "w_out_a", "w_out_b")]
    return pl.pallas_call(
        _attn_mlp_kernel, grid=(n // tm, f // tf),
        in_specs=[pl.BlockSpec((tm, d), row), pl.BlockSpec((tm, oa.shape[1]), row),
                  pl.BlockSpec((tm, ob.shape[1]), row)] + [_resident(a.shape) for a in consts]
                 + [pl.BlockSpec((d, tf), lambda i, j: (0, j)), pl.BlockSpec((tf, d), lambda i, j: (j, 0))],
        out_specs=pl.BlockSpec((tm, d), row),
        out_shape=jax.ShapeDtypeStruct((n, d), F32),
        scratch_shapes=[pltpu.VMEM((tm, d), BF16), pltpu.VMEM((tm, 1), F32)],
        compiler_params=_params("parallel", "arbitrary"), name="attn_mlp",
    )(x, oa, ob, *consts, wts["w_up"], wts["w_down"])


def _ple_kernel(x_ref, p_ref, g_ref, wg_ref, wp_ref, o_ref):
    x = x_ref[...]
    z = jnp.dot(_rms(x, g_ref[...]).astype(BF16), wg_ref[...], preferred_element_type=F32)
    gate = 1.0 / (1.0 + jnp.exp(-z))
    e = jnp.dot(p_ref[...].astype(BF16), wp_ref[...], preferred_element_type=F32)
    o_ref[...] = x + e * gate


def _ple(x2, p, wts, tm):
    n, d = x2.shape
    tm = min(tm, n)
    row = lambda i: (i, 0)
    consts = [wts[k] for k in ("g_ple", "w_gate", "w_ple")]
    return pl.pallas_call(
        _ple_kernel, grid=(n // tm,),
        in_specs=[pl.BlockSpec((tm, d), row), pl.BlockSpec((tm, p.shape[1]), row)]
                 + [_resident(a.shape) for a in consts],
        out_specs=pl.BlockSpec((tm, d), row),
        out_shape=jax.ShapeDtypeStruct((n, d), F32),
        compiler_params=_params("parallel"), name="ple",
    )(x2, p, *consts)


def _swap_halves(a):
    half = a.shape[-1] // 2
    return jnp.concatenate([a[..., half:], a[..., :half]], axis=-1)


def _prepare_weights(g_attn, w_in, g_qa, w_qb, g_kva, w_kvb, g_qn_mla, g_kn_mla, g_q_swa, g_k_swa,
                     g_out_mla, g_out_swa, w_out, g_mlp, w_up, w_down, g_ple, w_ple, w_gate):
    row = lambda g: g.reshape(1, -1)
    o_kpe = MLA_Q_LORA + MLA_KV_LORA
    o_qs = o_kpe + MLA_ROPE
    o_ks = o_qs + SWA_HEADS * SWA_HEAD_DIM
    o_vs = o_ks + SWA_KV_HEADS * SWA_HEAD_DIM
    kpe = w_in[:, o_kpe:o_qs]
    w_tm = jnp.concatenate([w_in[:, :o_kpe], kpe, _swap_halves(kpe), w_in[:, o_ks:o_vs]], axis=1)
    w_fm = jnp.concatenate([w_in[:, o_qs:o_ks], w_in[:, o_vs:]], axis=1).T
    w_kv = w_kvb.reshape(MLA_KV_LORA, MLA_HEADS, MLA_NOPE + MLA_V)
    w_kn = w_kv[..., :MLA_NOPE].reshape(MLA_KV_LORA, MLA_HEADS * MLA_NOPE)
    w_vt = w_kv[..., MLA_NOPE:].reshape(MLA_KV_LORA, MLA_HEADS * MLA_V).T
    g_rope = g_kn_mla[MLA_NOPE:]
    n_a = MLA_HEADS * MLA_V
    logit_span = 2.0 * MLA_QK ** 0.5 * LOG2E * jnp.max(jnp.abs(g_qn_mla)) * jnp.max(jnp.abs(g_kn_mla))
    return {
        "mla_logit_span": logit_span,
        "g_attn": row(g_attn), "w_tm": w_tm.astype(BF16), "w_fm": w_fm.astype(BF16),
        "g_qa": row(g_qa), "w_qbt": w_qb.T.astype(BF16),
        "g_kva": row(g_kva), "w_kn": w_kn.astype(BF16), "w_vt": w_vt.astype(BF16),
        "g_qn_mla": g_qn_mla, "gk_a": row(g_kn_mla[:MLA_NOPE]),
        "gk_b": row(jnp.concatenate([g_rope, _swap_halves(g_rope)])),
        "g_q_swa": g_q_swa, "g_k_swa": row(g_k_swa),
        "g_out_mla": row(g_out_mla), "g_out_swa": row(g_out_swa),
        "w_out_a": w_out[:n_a].astype(BF16), "w_out_b": w_out[n_a:].astype(BF16),
        "w_up": (g_mlp[:, None] * w_up).astype(BF16), "w_down": w_down.astype(BF16),
        "g_ple": row(g_ple), "w_gate": w_gate.astype(BF16), "w_ple": w_ple.astype(BF16),
    }


def _layer(x, p, sink, wts, tabs):
    batch, seq, d = x.shape
    n = batch * seq
    xf = x.reshape(n, d)
    qt, k, vt, qst, ks, vst = _proj(xf, seq, wts, tabs, tm=TILES["proj"])
    o_a = lax.cond(
        wts["mla_logit_span"] <= SAFE_LOG2_RANGE,
        functools.partial(_mla_attention, batch=batch, seq=seq, track_max=False, **TILES["mla"]),
        functools.partial(_mla_attention, batch=batch, seq=seq, track_max=True, **TILES["mla_online"]),
        qt, k, vt)
    o_b = _swa_attention(sink, qst, ks, vst, batch, seq, **TILES["swa"])
    x2 = _attn_mlp(xf, o_a, o_b, wts, **TILES["attn_mlp"])
    y = _ple(x2, p.reshape(n, -1), wts, tm=TILES["ple"])
    return y.reshape(batch, seq, d)


def kernel(x_prompt, x_sample, p_prompt, p_sample, g_attn, w_in, g_qa, w_qb, g_kva, w_kvb, g_qn_mla, g_kn_mla,
           g_q_swa, g_k_swa, sink, g_out_mla, g_out_swa, w_out, g_mlp, w_up, w_down, g_ple, w_ple, w_gate):
    y_prompt, y_sample = x_prompt, x_sample
    tabs = _rope_inputs(max(x_prompt.shape[1], x_sample.shape[1]))
    for l in range(g_attn.shape[0]):
        wts = _prepare_weights(g_attn[l], w_in[l], g_qa[l], w_qb[l], g_kva[l], w_kvb[l], g_qn_mla[l],
                               g_kn_mla[l], g_q_swa[l], g_k_swa[l], g_out_mla[l], g_out_swa[l], w_out[l],
                               g_mlp[l], w_up[l], w_down[l], g_ple[l], w_ple[l], w_gate[l])
        y_prompt = _layer(y_prompt, p_prompt[l], sink[l], wts, tabs)
        y_sample = _layer(y_sample, p_sample[l], sink[l], wts, tabs)
    return (y_prompt, y_sample)
```
